```python
import math
import jax, jax.numpy as jnp
from jax import lax
import numpy as np

D_MODEL = 2048
BATCH = 4
SEQ = 2048
DEPTH = 2

N_A = DEPTH // 2
N_B = DEPTH - N_A
GROUP_CH = 16
N_GROUPS = D_MODEL // GROUP_CH
STATE = 64
DT_MIN = 1e-3
DT_MAX = 1e-1
HEAD_DIM = 128
N_HEADS = D_MODEL // HEAD_DIM
BLOCK = 256
TOP_K_BLOCKS = 3
Q_CHUNK = 32
D_FF = int(math.ceil(8 * D_MODEL / 3 / 256) * 256)
EPS = 1e-6

kernel_name = "yoco_s5_moba_hybrid"


def rmsnorm(x, g):
    xf = x.astype(jnp.float32)
    y = xf * lax.rsqrt(jnp.mean(xf * xf, axis=-1, keepdims=True) + EPS)
    return (y * g.astype(jnp.float32)).astype(x.dtype)


def swiglu(u, w_gate, w_up, w_down):
    return (jax.nn.silu(u @ w_gate) * (u @ w_up)) @ w_down


def _ssm_combine(e1, e2):
    a1, b1 = e1
    a2, b2 = e2
    return a2 * a1, a2 * b1 + b2


def s5_mixer(u, lam_re, lam_im, log_dt, b_re, b_im, c_re, c_im, d_skip, w_glu, b_glu):
    bsz, L, D = u.shape
    f32 = jnp.float32
    lam = lax.complex(lam_re.astype(f32), lam_im.astype(f32))
    dt = jnp.exp(log_dt.astype(f32))[:, None]
    lam_bar = jnp.exp(lam * dt)
    b_c = lax.complex(b_re.astype(f32), b_im.astype(f32))
    b_bar = ((lam_bar - 1.0) / lam)[..., None] * b_c
    c_c = lax.complex(c_re.astype(f32), c_im.astype(f32))
    uf = u.astype(f32)
    ug = uf.reshape(bsz, L, N_GROUPS, GROUP_CH).astype(jnp.complex64)
    bu = jnp.einsum('blgc,gpc->blgp', ug, b_bar)
    a = jnp.broadcast_to(lam_bar, (1, L) + lam_bar.shape)
    _, states = lax.associative_scan(_ssm_combine, (a, bu), axis=1)
    y = jnp.einsum('blgp,gcp->blgc', states, c_c).real.reshape(bsz, L, D)
    y = y + d_skip.astype(f32) * uf
    z = jax.nn.gelu(y).astype(u.dtype)
    return z * jax.nn.sigmoid(z @ w_glu + b_glu)


def shared_kv(h, g_kv, w_kv):
    bsz, L, D = h.shape
    nb = -(-L // BLOCK)
    pad = nb * BLOCK - L
    kv = rmsnorm(h, g_kv) @ w_kv
    k, v = jnp.split(kv, 2, axis=-1)

    def to_blocks(t):
        t = jnp.pad(t, ((0, 0), (0, pad), (0, 0)))
        return t.reshape(bsz, nb, BLOCK, N_HEADS, HEAD_DIM).transpose(0, 3, 1, 2, 4)

    k_blocks = to_blocks(k)
    v_blocks = to_blocks(v)
    k_mean = jnp.mean(k_blocks.astype(jnp.float32), axis=3).astype(h.dtype)
    return k_blocks, v_blocks, k_mean


def alibi_slopes(n):
    return jnp.exp2(-8.0 * jnp.arange(1, n + 1, dtype=jnp.float32) / n)


def moba_attention(q, k_blocks, v_blocks, k_mean):
    bsz, H, L, Dh = q.shape
    nb = k_blocks.shape[2]
    k_sel = min(TOP_K_BLOCKS, nb)
    scale = Dh ** -0.5
    slopes = alibi_slopes(H)
    b_ix = jnp.arange(bsz)[:, None, None, None]
    h_ix = jnp.arange(H)[None, :, None, None]
    blk_ar = jnp.arange(BLOCK)

    def chunk(c):
        t0 = c * Q_CHUNK
        qc = lax.dynamic_slice_in_dim(q, t0, Q_CHUNK, axis=2)
        t = t0 + jnp.arange(Q_CHUNK)
        own = t0 // BLOCK
        gate = jnp.einsum('bhtd,bhnd->bhtn', qc, k_mean).astype(jnp.float32)
        gate = jnp.where(jnp.arange(nb) < own, gate, -jnp.inf)
        _, idx = lax.top_k(gate, k_sel)
        valid = jnp.arange(k_sel) < own
        k_g = k_blocks[b_ix, h_ix, idx]
        v_g = v_blocks[b_ix, h_ix, idx]
        s_sel = jnp.einsum('bhtd,bhtjsd->bhtjs', qc, k_g).astype(jnp.float32) * scale
        pos_sel = idx[..., None] * BLOCK + blk_ar
        dist_sel = (t[:, None, None] - pos_sel).astype(jnp.float32)
        s_sel = s_sel - slopes.reshape(1, H, 1, 1, 1) * dist_sel
        s_sel = jnp.where(valid[:, None], s_sel, -jnp.inf)
        k_own = lax.dynamic_slice_in_dim(k_blocks, own, 1, axis=2)[:, :, 0]
        v_own = lax.dynamic_slice_in_dim(v_blocks, own, 1, axis=2)[:, :, 0]
        s_own = jnp.einsum('bhtd,bhsd->bhts', qc, k_own).astype(jnp.float32) * scale
        dist_own = t[:, None] - (own * BLOCK + blk_ar)[None, :]
        s_own = jnp.where(dist_own >= 0,
                          s_own - slopes.reshape(1, H, 1, 1) * dist_own.astype(jnp.float32),
                          -jnp.inf)
        s_all = jnp.concatenate([s_sel.reshape(bsz, H, Q_CHUNK, k_sel * BLOCK), s_own], axis=-1)
        p = jax.nn.softmax(s_all, axis=-1).astype(q.dtype)
        p_sel = p[..., :k_sel * BLOCK].reshape(bsz, H, Q_CHUNK, k_sel, BLOCK)
        p_own = p[..., k_sel * BLOCK:]
        return (jnp.einsum('bhtjs,bhtjsd->bhtd', p_sel, v_g)
                + jnp.einsum('bhts,bhsd->bhtd', p_own, v_own))

    out = lax.map(chunk, jnp.arange(L // Q_CHUNK))
    return out.transpose(1, 0, 3, 2, 4).reshape(bsz, L, H * Dh)


def setup_inputs(seed: int = 0) -> dict:
    key = jax.random.key(seed)
    ks = jax.random.split(key, 24)
    D, F, G, P, C = D_MODEL, D_FF, N_GROUPS, STATE, GROUP_CH
    f32 = jnp.float32

    def nrm(k, shape, scale):
        return jax.random.normal(k, shape, f32) * scale

    x = jax.random.normal(ks[0], (BATCH, SEQ, D), f32)
    g_mix = 1.0 + nrm(ks[1], (DEPTH, D), 0.02)
    lambda_re = -0.5 + nrm(ks[2], (N_A, G, P), 0.01)
    lambda_im = jnp.pi * jnp.broadcast_to(jnp.arange(P, dtype=f32), (N_A, G, P)) + nrm(ks[3], (N_A, G, P), 0.01)
    log_dt = jax.random.uniform(ks[4], (N_A, G), f32, math.log(DT_MIN), math.log(DT_MAX))
    b_re = nrm(ks[5], (N_A, G, P, C), (2 * C) ** -0.5)
    b_im = nrm(ks[6], (N_A, G, P, C), (2 * C) ** -0.5)
    c_re = nrm(ks[7], (N_A, G, C, P), P ** -0.5)
    c_im = nrm(ks[8], (N_A, G, C, P), P ** -0.5)
    d_skip = nrm(ks[9], (N_A, D), 1.0)
    w_glu = nrm(ks[10], (N_A, D, D), D ** -0.5)
    b_glu = nrm(ks[11], (N_A, D), 0.02)
    g_kv = 1.0 + nrm(ks[12], (D,), 0.02)
    w_kv = nrm(ks[13], (D, 2 * D), D ** -0.5)
    w_q = nrm(ks[14], (N_B, D, D), D ** -0.5)
    w_o = nrm(ks[15], (N_B, D, D), D ** -0.5)
    g_ffn = 1.0 + nrm(ks[16], (DEPTH, D), 0.02)
    w_gate = nrm(ks[17], (DEPTH, D, F), D ** -0.5)
    w_up = nrm(ks[18], (DEPTH, D, F), D ** -0.5)
    w_down = nrm(ks[19], (DEPTH, F, D), F ** -0.5)
    g_final = 1.0 + nrm(ks[20], (D,), 0.02)
    return {"x": x, "g_mix": g_mix, "lambda_re": lambda_re, "lambda_im": lambda_im,
            "log_dt": log_dt, "b_re": b_re, "b_im": b_im, "c_re": c_re, "c_im": c_im,
            "d_skip": d_skip, "w_glu": w_glu, "b_glu": b_glu, "g_kv": g_kv, "w_kv": w_kv,
            "w_q": w_q, "w_o": w_o, "g_ffn": g_ffn, "w_gate": w_gate, "w_up": w_up,
            "w_down": w_down, "g_final": g_final}


def reference(x, g_mix, lambda_re, lambda_im, log_dt, b_re, b_im, c_re, c_im, d_skip, w_glu, b_glu,
              g_kv, w_kv, w_q, w_o, g_ffn, w_gate, w_up, w_down, g_final):
    bsz, L, D = x.shape
    h = x
    k_blocks = v_blocks = k_mean = None
    for i in range(DEPTH):
        u = rmsnorm(h, g_mix[i])
        if i < N_A:
            h = h + s5_mixer(u, lambda_re[i], lambda_im[i], log_dt[i], b_re[i], b_im[i],
                             c_re[i], c_im[i], d_skip[i], w_glu[i], b_glu[i])
        else:
            if i == N_A:
                k_blocks, v_blocks, k_mean = shared_kv(h, g_kv, w_kv)
            j = i - N_A
            q = (u @ w_q[j]).reshape(bsz, L, N_HEADS, HEAD_DIM).transpose(0, 2, 1, 3)
            h = h + moba_attention(q, k_blocks, v_blocks, k_mean) @ w_o[j]
        u = rmsnorm(h, g_ffn[i])
        h = h + swiglu(u, w_gate[i], w_up[i], w_down[i])
    return rmsnorm(h, g_final)
```

```python
import functools
import math

import jax
import jax.numpy as jnp
from jax import lax
from jax.experimental import pallas as pl
from jax.experimental.pallas import tpu as pltpu

F32 = jnp.float32
BF16 = jnp.bfloat16

EPS = 1e-6
LANES = 128
HEAD_DIM = 128
KEY_BLOCK = 256
TOP_K_BLOCKS = 3
SSM_CHUNK = 8
VMEM_LIMIT = 56 * 1024 * 1024


def _cparams(sem):
    return pltpu.CompilerParams(dimension_semantics=sem, vmem_limit_bytes=VMEM_LIMIT)


def _rms_scale(x):
    return lax.rsqrt(jnp.mean(x * x, axis=-1, keepdims=True) + EPS)


def _rstd_kernel(x_ref, o_ref):
    o_ref[...] = _rms_scale(x_ref[...])


def _row_rstd(x2d, tm=512):
    m, d = x2d.shape
    return pl.pallas_call(
        _rstd_kernel,
        grid=(m // tm,),
        in_specs=[pl.BlockSpec((tm, d), lambda i: (i, 0))],
        out_specs=pl.BlockSpec((tm, 1), lambda i: (i, 0)),
        out_shape=jax.ShapeDtypeStruct((m, 1), F32),
        compiler_params=_cparams(("parallel",)),
        name="row_rstd",
    )(x2d)


def _s5_tables(lam_re, lam_im, log_dt, b_re, b_im, c_re, c_im):
    g, p, c = b_re.shape
    t = SSM_CHUNK
    gl = LANES // c
    nj = g // gl
    lam = lax.complex(lam_re.astype(F32), lam_im.astype(F32))
    dt = jnp.exp(log_dt.astype(F32))[:, None]
    lam_dt = lam * dt
    lam_bar = jnp.exp(lam_dt)
    b_bar = ((lam_bar - 1.0) / lam)[..., None] * lax.complex(b_re.astype(F32), b_im.astype(F32))
    c_c = lax.complex(c_re.astype(F32), c_im.astype(F32))
    pw = jnp.exp(lam_dt[None] * jnp.arange(t + 1, dtype=F32)[:, None, None])
    eye = jnp.eye(gl, dtype=F32)

    kd = jnp.real(jnp.einsum('gop,dgp,gpi->dgio', c_c, pw[:t], b_bar))
    kd = kd.reshape(t, nj, gl, c, c)
    ii = jnp.arange(t)
    dmat = ii[None, :] - ii[:, None]
    ktoe = jnp.where((dmat >= 0)[:, :, None, None, None, None], kd[jnp.clip(dmat, 0, t - 1)], 0.0)
    toep = jnp.einsum('abjgxy,gh->jagxbhy', ktoe, eye).reshape(nj, t * LANES, t * LANES)

    vin = pw[t - 1 - ii][..., None] * b_bar[None]
    vin = vin.reshape(t, nj, gl, p, c)
    m_in = jnp.concatenate(
        [jnp.einsum('ajgpc,gh->jagchp', part, eye).reshape(nj, t * LANES, gl * p)
         for part in (jnp.real(vin), jnp.imag(vin))], axis=-1)

    wout = c_c[None] * pw[1:t + 1][:, :, None, :]
    wout = wout.reshape(t, nj, gl, c, p)
    m_out = jnp.concatenate(
        [jnp.einsum('ijgcp,gh->jgpihc', part, eye).reshape(nj, gl * p, t * LANES)
         for part in (jnp.real(wout), -jnp.imag(wout))], axis=1)

    w1 = jnp.concatenate([toep, m_in], axis=-1).astype(BF16)
    w2 = m_out.astype(BF16)
    a = jnp.stack([jnp.real(pw[t]).reshape(nj, gl * p), jnp.imag(pw[t]).reshape(nj, gl * p)], axis=1)
    return w1, w2, a


def _s5_kernel(h_ref, rstd_ref, g_ref, d_ref, w1_ref, w2_ref, a_ref, z_ref,
               u_scr, lhs_scr, s_scr, yc_scr, y_scr):
    t = SSM_CHUNK
    nk = lhs_scr.shape[0]
    ns = s_scr.shape[1] // 2
    yw = t * LANES

    u_scr[...] = h_ref[0] * rstd_ref[0] * g_ref[...]
    for i in range(t):
        lhs_scr[:, i * LANES:(i + 1) * LANES] = u_scr[pl.ds(i, nk, stride=t), :].astype(BF16)

    r1 = jnp.dot(lhs_scr[...], w1_ref[0], preferred_element_type=F32)
    yc_scr[...] = r1[:, :yw]
    s_scr[...] = r1[:, yw:]

    a_re = a_ref[0, 0:1, :]
    a_im = a_ref[0, 1:2, :]

    def scan_step(k, carry):
        e_re, e_im = carry
        row = s_scr[pl.ds(k, 1), :]
        s_scr[pl.ds(k, 1), :ns] = e_re
        s_scr[pl.ds(k, 1), ns:] = e_im
        n_re = a_re * e_re - a_im * e_im + row[:, :ns]
        n_im = a_re * e_im + a_im * e_re + row[:, ns:]
        return n_re, n_im

    zero = jnp.zeros((1, ns), F32)
    lax.fori_loop(0, nk, scan_step, (zero, zero))

    y = yc_scr[...] + jnp.dot(s_scr[...].astype(BF16), w2_ref[0], preferred_element_type=F32)
    for i in range(t):
        y_scr[pl.ds(i, nk, stride=t), :] = y[:, i * LANES:(i + 1) * LANES]

    yy = y_scr[...] + d_ref[...] * u_scr[...]
    z_ref[0] = jax.nn.gelu(yy).astype(z_ref.dtype)


def _s5_mixer(h3d, rstd3d, g_row, d_row, w1, w2, a):
    b, l, d = h3d.shape
    nj = d // LANES
    nk = l // SSM_CHUNK
    ns2 = w2.shape[1]
    yw = SSM_CHUNK * LANES
    return pl.pallas_call(
        _s5_kernel,
        grid=(nj, b),
        in_specs=[
            pl.BlockSpec((1, l, LANES), lambda j, i: (i, 0, j)),
            pl.BlockSpec((1, l, 1), lambda j, i: (i, 0, 0)),
            pl.BlockSpec((1, LANES), lambda j, i: (0, j)),
            pl.BlockSpec((1, LANES), lambda j, i: (0, j)),
            pl.BlockSpec((1, yw, yw + ns2), lambda j, i: (j, 0, 0)),
            pl.BlockSpec((1, ns2, yw), lambda j, i: (j, 0, 0)),
            pl.BlockSpec((1, 2, ns2 // 2), lambda j, i: (j, 0, 0)),
        ],
        out_specs=pl.BlockSpec((1, l, LANES), lambda j, i: (i, 0, j)),
        out_shape=jax.ShapeDtypeStruct((b, l, d), BF16),
        scratch_shapes=[
            pltpu.VMEM((l, LANES), F32),
            pltpu.VMEM((nk, yw), BF16),
            pltpu.VMEM((nk, ns2), F32),
            pltpu.VMEM((nk, yw), F32),
            pltpu.VMEM((l, LANES), F32),
        ],
        compiler_params=_cparams(("parallel", "parallel")),
        name="s5_mixer",
    )(h3d, rstd3d, g_row, d_row, w1, w2, a)


def _glu_kernel(z_ref, w_ref, b_ref, zt_ref, res_ref, o_ref):
    a = jnp.dot(z_ref[...], w_ref[...], preferred_element_type=F32) + b_ref[...]
    o_ref[...] = res_ref[...] + zt_ref[...].astype(F32) * jax.nn.sigmoid(a)


def _glu_residual(z, w, b_row, res, tm=1024, tn=512):
    m, k = z.shape
    n = w.shape[1]
    return pl.pallas_call(
        _glu_kernel,
        grid=(m // tm, n // tn),
        in_specs=[
            pl.BlockSpec((tm, k), lambda i, j: (i, 0)),
            pl.BlockSpec((k, tn), lambda i, j: (0, j)),
            pl.BlockSpec((1, tn), lambda i, j: (0, j)),
            pl.BlockSpec((tm, tn), lambda i, j: (i, j)),
            pl.BlockSpec((tm, tn), lambda i, j: (i, j)),
        ],
        out_specs=pl.BlockSpec((tm, tn), lambda i, j: (i, j)),
        out_shape=jax.ShapeDtypeStruct((m, n), F32),
        compiler_params=_cparams(("parallel", "parallel")),
        name="glu_residual",
    )(z, w, b_row, z, res)


def _proj_res_kernel(x_ref, w_ref, res_ref, o_ref):
    o_ref[...] = res_ref[...] + jnp.dot(x_ref[...], w_ref[...], preferred_element_type=F32)


def _proj_residual(x, w, res, tm=1024, tn=512):
    m, k = x.shape
    n = w.shape[1]
    return pl.pallas_call(
        _proj_res_kernel,
        grid=(m // tm, n // tn),
        in_specs=[
            pl.BlockSpec((tm, k), lambda i, j: (i, 0)),
            pl.BlockSpec((k, tn), lambda i, j: (0, j)),
            pl.BlockSpec((tm, tn), lambda i, j: (i, j)),
        ],
        out_specs=pl.BlockSpec((tm, tn), lambda i, j: (i, j)),
        out_shape=jax.ShapeDtypeStruct((m, n), F32),
        compiler_params=_cparams(("parallel", "parallel")),
        name="proj_residual",
    )(x, w, res)


def _qkv_kernel(x_ref, gq_ref, gkv_ref, w_ref, o_ref, uq_scr, ukv_scr, *, n_q_tiles, q_scale):
    j = pl.program_id(1)

    @pl.when(j == 0)
    def _():
        x = x_ref[...]
        xn = x * _rms_scale(x)
        uq_scr[...] = (xn * gq_ref[...]).astype(BF16)
        ukv_scr[...] = (xn * gkv_ref[...]).astype(BF16)

    @pl.when(j < n_q_tiles)
    def _():
        o_ref[...] = (jnp.dot(uq_scr[...], w_ref[...], preferred_element_type=F32) * q_scale).astype(o_ref.dtype)

    @pl.when(j >= n_q_tiles)
    def _():
        o_ref[...] = jnp.dot(ukv_scr[...], w_ref[...], preferred_element_type=F32).astype(o_ref.dtype)


def _qkv_proj(h2d, gq_row, gkv_row, w_qkv, d_q, q_scale, tm=1024, tn=512):
    m, k = h2d.shape
    n = w_qkv.shape[1]
    return pl.pallas_call(
        functools.partial(_qkv_kernel, n_q_tiles=d_q // tn, q_scale=q_scale),
        grid=(m // tm, n // tn),
        in_specs=[
            pl.BlockSpec((tm, k), lambda i, j: (i, 0)),
            pl.BlockSpec((1, k), lambda i, j: (0, 0)),
            pl.BlockSpec((1, k), lambda i, j: (0, 0)),
            pl.BlockSpec((k, tn), lambda i, j: (0, j)),
        ],
        out_specs=pl.BlockSpec((tm, tn), lambda i, j: (i, j)),
        out_shape=jax.ShapeDtypeStruct((m, n), BF16),
        scratch_shapes=[pltpu.VMEM((tm, k), BF16), pltpu.VMEM((tm, k), BF16)],
        compiler_params=_cparams(("parallel", "arbitrary")),
        name="qkv_proj",
    )(h2d, gq_row, gkv_row, w_qkv)


def _ffn_kernel(h_ref, g_ref, wg_ref, wu_ref, wd_ref, gfin_ref, o_ref, u_scr, *, final_norm):
    f = pl.program_id(1)

    @pl.when(f == 0)
    def _():
        x = h_ref[...]
        u_scr[...] = (x * _rms_scale(x) * g_ref[...]).astype(BF16)
        o_ref[...] = x

    u = u_scr[...]
    gate = jnp.dot(u, wg_ref[...], preferred_element_type=F32)
    up = jnp.dot(u, wu_ref[...], preferred_element_type=F32)
    act = (jax.nn.silu(gate) * up).astype(BF16)
    o_ref[...] += jnp.dot(act, wd_ref[...], preferred_element_type=F32)

    if final_norm:
        @pl.when(f == pl.num_programs(1) - 1)
        def _():
            y = o_ref[...]
            o_ref[...] = y * _rms_scale(y) * gfin_ref[...]


def _ffn_residual(h2d, g_row, w_gate, w_up, w_down, gfin_row, final_norm, tm=1024, tf=512):
    m, d = h2d.shape
    ff = w_gate.shape[1]
    return pl.pallas_call(
        functools.partial(_ffn_kernel, final_norm=final_norm),
        grid=(m // tm, ff // tf),
        in_specs=[
            pl.BlockSpec((tm, d), lambda i, f: (i, 0), pipeline_mode=pl.Buffered(1)),
            pl.BlockSpec((1, d), lambda i, f: (0, 0)),
            pl.BlockSpec((d, tf), lambda i, f: (0, f)),
            pl.BlockSpec((d, tf), lambda i, f: (0, f)),
            pl.BlockSpec((tf, d), lambda i, f: (f, 0)),
            pl.BlockSpec((1, d), lambda i, f: (0, 0)),
        ],
        out_specs=pl.BlockSpec((tm, d), lambda i, f: (i, 0)),
        out_shape=jax.ShapeDtypeStruct((m, d), F32),
        scratch_shapes=[pltpu.VMEM((tm, d), BF16)],
        compiler_params=_cparams(("parallel", "arbitrary")),
        name="ffn_residual",
    )(h2d, g_row, w_gate, w_up, w_down, gfin_row)


def _moba_kernel(slopes_ref, q_ref, k_ref, v_ref, o_ref, kmean_scr, alibi_scr):
    own = pl.program_id(2)
    tq = q_ref.shape[1]
    nb = k_ref.shape[1] // KEY_BLOCK
    neg_inf = -jnp.inf
    slope = slopes_ref[pl.program_id(1)]

    @pl.when(own == 0)
    def _():
        kmean_scr[...] = jnp.zeros_like(kmean_scr)
        for n in range(nb):
            kb = k_ref[0, n * KEY_BLOCK:(n + 1) * KEY_BLOCK, :].astype(F32)
            kmean_scr[n:n + 1, :] = jnp.mean(kb, axis=0, keepdims=True)
        r = lax.broadcasted_iota(jnp.int32, (tq, KEY_BLOCK), 0)
        c = lax.broadcasted_iota(jnp.int32, (tq, KEY_BLOCK), 1)
        alibi_scr[...] = -slope * (r - c).astype(F32)

    q = q_ref[0]

    gate = lax.dot_general(q.astype(F32), kmean_scr[...], (((1,), (1,)), ((), ())),
                           precision=lax.Precision.HIGHEST, preferred_element_type=F32)
    col = lax.broadcasted_iota(jnp.int32, gate.shape, 1)
    colf = col.astype(F32)
    gate = jnp.where(col < own, gate, neg_inf)
    offset = -slope * ((own - col) * KEY_BLOCK).astype(F32)
    rowterm = jnp.full(gate.shape, neg_inf, F32)
    for r in range(TOP_K_BLOCKS):
        mx = jnp.max(gate, axis=-1, keepdims=True)
        idx = jnp.min(jnp.where(gate == mx, colf, float(LANES)), axis=-1, keepdims=True)
        hit = colf == idx
        rowterm = jnp.where(hit & (jnp.full_like(col, r) < own), offset, rowterm)
        gate = jnp.where(hit, neg_inf, gate)

    k_own = k_ref[0, pl.ds(pl.multiple_of(own * KEY_BLOCK, KEY_BLOCK), KEY_BLOCK), :]
    v_own = v_ref[0, pl.ds(pl.multiple_of(own * KEY_BLOCK, KEY_BLOCK), KEY_BLOCK), :]
    s = lax.dot_general(q, k_own, (((1,), (1,)), ((), ())), preferred_element_type=F32)
    ab = alibi_scr[...]
    s = jnp.where(ab <= 0.0, s + ab, neg_inf)
    m0 = jnp.max(s, axis=-1, keepdims=True)
    p = jnp.exp(s - m0)
    l0 = jnp.sum(p, axis=-1, keepdims=True)
    acc0 = jnp.dot(p.astype(BF16), v_own, preferred_element_type=F32)

    def past_block(n, carry):
        m_i, l_i, acc = carry
        start = pl.multiple_of(n * KEY_BLOCK, KEY_BLOCK)
        k_n = k_ref[0, pl.ds(start, KEY_BLOCK), :]
        v_n = v_ref[0, pl.ds(start, KEY_BLOCK), :]
        rt = jnp.max(jnp.where(col == n, rowterm, neg_inf), axis=-1, keepdims=True)
        s_n = lax.dot_general(q, k_n, (((1,), (1,)), ((), ())), preferred_element_type=F32)
        s_n = s_n + alibi_scr[...] + rt
        m_new = jnp.maximum(m_i, jnp.max(s_n, axis=-1, keepdims=True))
        alpha = jnp.exp(m_i - m_new)
        p_n = jnp.exp(s_n - m_new)
        l_new = alpha * l_i + jnp.sum(p_n, axis=-1, keepdims=True)
        acc_new = alpha * acc + jnp.dot(p_n.astype(BF16), v_n, preferred_element_type=F32)
        return m_new, l_new, acc_new

    _, l_f, acc_f = lax.fori_loop(0, own, past_block, (m0, l0, acc0))
    o_ref[0] = (acc_f / l_f).astype(o_ref.dtype)


def _moba_attention(qkv3d, d_model):
    b, l, _ = qkv3d.shape
    n_heads = d_model // HEAD_DIM
    tq = KEY_BLOCK
    slopes = jnp.exp2(-8.0 * jnp.arange(1, n_heads + 1, dtype=F32) / n_heads)
    return pl.pallas_call(
        _moba_kernel,
        grid=(b, n_heads, l // tq),
        in_specs=[
            pl.BlockSpec(memory_space=pltpu.SMEM),
            pl.BlockSpec((1, tq, HEAD_DIM), lambda i, h, t: (i, t, h)),
            pl.BlockSpec((1, l, HEAD_DIM), lambda i, h, t: (i, 0, n_heads + h)),
            pl.BlockSpec((1, l, HEAD_DIM), lambda i, h, t: (i, 0, 2 * n_heads + h)),
        ],
        out_specs=pl.BlockSpec((1, tq, HEAD_DIM), lambda i, h, t: (i, t, h)),
        out_shape=jax.ShapeDtypeStruct((b, l, d_model), BF16),
        scratch_shapes=[
            pltpu.VMEM((LANES, HEAD_DIM), F32),
            pltpu.VMEM((tq, KEY_BLOCK), F32),
        ],
        compiler_params=_cparams(("parallel", "parallel", "arbitrary")),
        name="moba_attention",
    )(slopes, qkv3d, qkv3d, qkv3d)


def kernel(x, g_mix, lambda_re, lambda_im, log_dt, b_re, b_im, c_re, c_im, d_skip, w_glu, b_glu, g_kv, w_kv,
           w_q, w_o, g_ffn, w_gate, w_up, w_down, g_final):
    bsz, l, d = x.shape
    depth = g_mix.shape[0]
    n_a = lambda_re.shape[0]
    m = bsz * l
    assert l % KEY_BLOCK == 0 and d % HEAD_DIM == 0 and l // KEY_BLOCK <= LANES
    row = lambda v: v.reshape(1, -1).astype(F32)

    h = x.astype(F32).reshape(m, d)
    qkv = None
    for i in range(depth):
        if i < n_a:
            w1, w2, a = _s5_tables(lambda_re[i], lambda_im[i], log_dt[i], b_re[i], b_im[i], c_re[i], c_im[i])
            rstd = _row_rstd(h)
            z = _s5_mixer(h.reshape(bsz, l, d), rstd.reshape(bsz, l, 1), row(g_mix[i]), row(d_skip[i]), w1, w2, a)
            h = _glu_residual(z.reshape(m, d), w_glu[i].astype(BF16), row(b_glu[i]), h)
        else:
            j = i - n_a
            w_qkv = jnp.concatenate([w_q[j], w_kv], axis=1).astype(BF16)
            qkv_i = _qkv_proj(h, row(g_mix[i]), row(g_kv), w_qkv, d, HEAD_DIM ** -0.5)
            if qkv is None:
                qkv = qkv_i
            else:
                qkv = jnp.concatenate([qkv_i[:, :d], qkv[:, d:]], axis=1)
            attn = _moba_attention(qkv.reshape(bsz, l, 3 * d), d)
            h = _proj_residual(attn.reshape(m, d), w_o[j].astype(BF16), h)
        h = _ffn_residual(h, row(g_ffn[i]), w_gate[i].astype(BF16), w_up[i].astype(BF16), w_down[i].astype(BF16),
                          row(g_final), final_norm=(i == depth - 1))
    return h.reshape(bsz, l, d).astype(x.dtype)
```

```python
import functools
import math

import jax
import jax.numpy as jnp
from jax import lax
from jax.experimental import pallas as pl
from jax.experimental.pallas import tpu as pltpu

F32 = jnp.float32
BF16 = jnp.bfloat16

EPS = 1e-6
LANES = 128
HEAD_DIM = 128
KEY_BLOCK = 256
TOP_K_BLOCKS = 3
SSM_CHUNK = 8
VMEM_LIMIT = 56 * 1024 * 1024


def _cparams(sem):
    return pltpu.CompilerParams(dimension_semantics=sem, vmem_limit_bytes=VMEM_LIMIT)


def _rms_scale(x):
    return lax.rsqrt(jnp.mean(x * x, axis=-1, keepdims=True) + EPS)


def _rstd_kernel(x_ref, o_ref):
    o_ref[...] = _rms_scale(x_ref[...])


def _row_rstd(x2d, tm=512):
    m, d = x2d.shape
    return pl.pallas_call(
        _rstd_kernel,
        grid=(m // tm,),
        in_specs=[pl.BlockSpec((tm, d), lambda i: (i, 0))],
        out_specs=pl.BlockSpec((tm, 1), lambda i: (i, 0)),
        out_shape=jax.ShapeDtypeStruct((m, 1), F32),
        compiler_params=_cparams(("parallel",)),
        name="row_rstd",
    )(x2d)


def _s5_tables(lam_re, lam_im, log_dt, b_re, b_im, c_re, c_im):
    g, p, c = b_re.shape
    t = SSM_CHUNK
    gl = LANES // c
    nj = g // gl
    lam = lax.complex(lam_re.astype(F32), lam_im.astype(F32))
    dt = jnp.exp(log_dt.astype(F32))[:, None]
    lam_dt = lam * dt
    lam_bar = jnp.exp(lam_dt)
    b_bar = ((lam_bar - 1.0) / lam)[..., None] * lax.complex(b_re.astype(F32), b_im.astype(F32))
    c_c = lax.complex(c_re.astype(F32), c_im.astype(F32))
    pw = jnp.exp(lam_dt[None] * jnp.arange(t + 1, dtype=F32)[:, None, None])
    eye = jnp.eye(gl, dtype=F32)

    kd = jnp.real(jnp.einsum('gop,dgp,gpi->dgio', c_c, pw[:t], b_bar))
    kd = kd.reshape(t, nj, gl, c, c)
    ii = jnp.arange(t)
    dmat = ii[None, :] - ii[:, None]
    ktoe = jnp.where((dmat >= 0)[:, :, None, None, None, None], kd[jnp.clip(dmat, 0, t - 1)], 0.0)
    toep = jnp.einsum('abjgxy,gh->jagxbhy', ktoe, eye).reshape(nj, t * LANES, t * LANES)

    vin = pw[t - 1 - ii][..., None] * b_bar[None]
    vin = vin.reshape(t, nj, gl, p, c)
    m_in = jnp.concatenate(
        [jnp.einsum('ajgpc,gh->jagchp', part, eye).reshape(nj, t * LANES, gl * p)
         for part in (jnp.real(vin), jnp.imag(vin))], axis=-1)

    wout = c_c[None] * pw[1:t + 1][:, :, None, :]
    wout = wout.reshape(t, nj, gl, c, p)
    m_out = jnp.concatenate(
        [jnp.einsum('ijgcp,gh->jgpihc', part, eye).reshape(nj, gl * p, t * LANES)
         for part in (jnp.real(wout), -jnp.imag(wout))], axis=1)

    w1 = jnp.concatenate([toep, m_in], axis=-1).astype(BF16)
    w2 = m_out.astype(BF16)
    a = jnp.stack([jnp.real(pw[t]).reshape(nj, gl * p), jnp.imag(pw[t]).reshape(nj, gl * p)], axis=1)
    return w1, w2, a


def _s5_kernel(h_ref, rstd_ref, g_ref, d_ref, w1_ref, w2_ref, a_ref, z_ref,
               u_scr, lhs_scr, s_scr, yc_scr, y_scr):
    t = SSM_CHUNK
    nk = lhs_scr.shape[0]
    ns = s_scr.shape[1] // 2
    yw = t * LANES

    u_scr[...] = h_ref[0] * rstd_ref[0] * g_ref[...]
    for i in range(t):
        lhs_scr[:, i * LANES:(i + 1) * LANES] = u_scr[pl.ds(i, nk, stride=t), :].astype(BF16)

    r1 = jnp.dot(lhs_scr[...], w1_ref[0], preferred_element_type=F32)
    yc_scr[...] = r1[:, :yw]
    s_scr[...] = r1[:, yw:]

    a_re = a_ref[0, 0:1, :]
    a_im = a_ref[0, 1:2, :]

    def scan_step(k, carry):
        e_re, e_im = carry
        row = s_scr[pl.ds(k, 1), :]
        s_scr[pl.ds(k, 1), :ns] = e_re
        s_scr[pl.ds(k, 1), ns:] = e_im
        n_re = a_re * e_re - a_im * e_im + row[:, :ns]
        n_im = a_re * e_im + a_im * e_re + row[:, ns:]
        return n_re, n_im

    zero = jnp.zeros((1, ns), F32)
    lax.fori_loop(0, nk, scan_step, (zero, zero))

    y = yc_scr[...] + jnp.dot(s_scr[...].astype(BF16), w2_ref[0], preferred_element_type=F32)
    for i in range(t):
        y_scr[pl.ds(i, nk, stride=t), :] = y[:, i * LANES:(i + 1) * LANES]

    yy = y_scr[...] + d_ref[...] * u_scr[...]
    z_ref[0] = jax.nn.gelu(yy).astype(z_ref.dtype)


def _s5_mixer(h3d, rstd3d, g_row, d_row, w1, w2, a):
    b, l, d = h3d.shape
    nj = d // LANES
    nk = l // SSM_CHUNK
    ns2 = w2.shape[1]
    yw = SSM_CHUNK * LANES
    return pl.pallas_call(
        _s5_kernel,
        grid=(nj, b),
        in_specs=[
            pl.BlockSpec((1, l, LANES), lambda j, i: (i, 0, j)),
            pl.BlockSpec((1, l, 1), lambda j, i: (i, 0, 0)),
            pl.BlockSpec((1, LANES), lambda j, i: (0, j)),
            pl.BlockSpec((1, LANES), lambda j, i: (0, j)),
            pl.BlockSpec((1, yw, yw + ns2), lambda j, i: (j, 0, 0)),
            pl.BlockSpec((1, ns2, yw), lambda j, i: (j, 0, 0)),
            pl.BlockSpec((1, 2, ns2 // 2), lambda j, i: (j, 0, 0)),
        ],
        out_specs=pl.BlockSpec((1, l, LANES), lambda j, i: (i, 0, j)),
        out_shape=jax.ShapeDtypeStruct((b, l, d), BF16),
        scratch_shapes=[
            pltpu.VMEM((l, LANES), F32),
            pltpu.VMEM((nk, yw), BF16),
            pltpu.VMEM((nk, ns2), F32),
            pltpu.VMEM((nk, yw), F32),
            pltpu.VMEM((l, LANES), F32),
        ],
        compiler_params=_cparams(("parallel", "parallel")),
        name="s5_mixer",
    )(h3d, rstd3d, g_row, d_row, w1, w2, a)


def _glu_kernel(z_ref, w_ref, b_ref, zt_ref, res_ref, o_ref):
    a = jnp.dot(z_ref[...], w_ref[...], preferred_element_type=F32) + b_ref[...]
    o_ref[...] = res_ref[...] + zt_ref[...].astype(F32) * jax.nn.sigmoid(a)


def _glu_residual(z, w, b_row, res, tm=1024, tn=512):
    m, k = z.shape
    n = w.shape[1]
    return pl.pallas_call(
        _glu_kernel,
        grid=(m // tm, n // tn),
        in_specs=[
            pl.BlockSpec((tm, k), lambda i, j: (i, 0)),
            pl.BlockSpec((k, tn), lambda i, j: (0, j)),
            pl.BlockSpec((1, tn), lambda i, j: (0, j)),
            pl.BlockSpec((tm, tn), lambda i, j: (i, j)),
            pl.BlockSpec((tm, tn), lambda i, j: (i, j)),
        ],
        out_specs=pl.BlockSpec((tm, tn), lambda i, j: (i, j)),
        out_shape=jax.ShapeDtypeStruct((m, n), F32),
        compiler_params=_cparams(("parallel", "parallel")),
        name="glu_residual",
    )(z, w, b_row, z, res)


def _proj_res_kernel(x_ref, w_ref, res_ref, o_ref):
    o_ref[...] = res_ref[...] + jnp.dot(x_ref[...], w_ref[...], preferred_element_type=F32)


def _proj_residual(x, w, res, tm=1024, tn=512):
    m, k = x.shape
    n = w.shape[1]
    return pl.pallas_call(
        _proj_res_kernel,
        grid=(m // tm, n // tn),
        in_specs=[
            pl.BlockSpec((tm, k), lambda i, j: (i, 0)),
            pl.BlockSpec((k, tn), lambda i, j: (0, j)),
            pl.BlockSpec((tm, tn), lambda i, j: (i, j)),
        ],
        out_specs=pl.BlockSpec((tm, tn), lambda i, j: (i, j)),
        out_shape=jax.ShapeDtypeStruct((m, n), F32),
        compiler_params=_cparams(("parallel", "parallel")),
        name="proj_residual",
    )(x, w, res)


def _qkv_kernel(x_ref, gq_ref, gkv_ref, w_ref, o_ref, uq_scr, ukv_scr, *, n_q_tiles, q_scale):
    j = pl.program_id(1)

    @pl.when(j == 0)
    def _():
        x = x_ref[...]
        xn = x * _rms_scale(x)
        uq_scr[...] = (xn * gq_ref[...]).astype(BF16)
        ukv_scr[...] = (xn * gkv_ref[...]).astype(BF16)

    @pl.when(j < n_q_tiles)
    def _():
        o_ref[...] = (jnp.dot(uq_scr[...], w_ref[...], preferred_element_type=F32) * q_scale).astype(o_ref.dtype)

    @pl.when(j >= n_q_tiles)
    def _():
        o_ref[...] = jnp.dot(ukv_scr[...], w_ref[...], preferred_element_type=F32).astype(o_ref.dtype)


def _qkv_proj(h2d, gq_row, gkv_row, w_qkv, d_q, q_scale, tm=1024, tn=512):
    m, k = h2d.shape
    n = w_qkv.shape[1]
    return pl.pallas_call(
        functools.partial(_qkv_kernel, n_q_tiles=d_q // tn, q_scale=q_scale),
        grid=(m // tm, n // tn),
        in_specs=[
            pl.BlockSpec((tm, k), lambda i, j: (i, 0)),
            pl.BlockSpec((1, k), lambda i, j: (0, 0)),
            pl.BlockSpec((1, k), lambda i, j: (0, 0)),
            pl.BlockSpec((k, tn), lambda i, j: (0, j)),
        ],
        out_specs=pl.BlockSpec((tm, tn), lambda i, j: (i, j)),
        out_shape=jax.ShapeDtypeStruct((m, n), BF16),
        scratch_shapes=[pltpu.VMEM((tm, k), BF16), pltpu.VMEM((tm, k), BF16)],
        compiler_params=_cparams(("parallel", "arbitrary")),
        name="qkv_proj",
    )(h2d, gq_row, gkv_row, w_qkv)


def _ffn_kernel(h_ref, g_ref, wg_ref, wu_ref, wd_ref, gfin_ref, o_ref, u_scr, *, final_norm):
    f = pl.program_id(1)

    @pl.when(f == 0)
    def _():
        x = h_ref[...]
        u_scr[...] = (x * _rms_scale(x) * g_ref[...]).astype(BF16)
        o_ref[...] = x

    u = u_scr[...]
    gate = jnp.dot(u, wg_ref[...], preferred_element_type=F32)
    up = jnp.dot(u, wu_ref[...], preferred_element_type=F32)
    act = (jax.nn.silu(gate) * up).astype(BF16)
    o_ref[...] += jnp.dot(act, wd_ref[...], preferred_element_type=F32)

    if final_norm:
        @pl.when(f == pl.num_programs(1) - 1)
        def _():
            y = o_ref[...]
            o_ref[...] = y * _rms_scale(y) * gfin_ref[...]


def _ffn_residual(h2d, g_row, w_gate, w_up, w_down, gfin_row, final_norm, tm=1024, tf=512):
    m, d = h2d.shape
    ff = w_gate.shape[1]
    return pl.pallas_call(
        functools.partial(_ffn_kernel, final_norm=final_norm),
        grid=(m // tm, ff // tf),
        in_specs=[
            pl.BlockSpec((tm, d), lambda i, f: (i, 0), pipeline_mode=pl.Buffered(1)),
            pl.BlockSpec((1, d), lambda i, f: (0, 0)),
            pl.BlockSpec((d, tf), lambda i, f: (0, f)),
            pl.BlockSpec((d, tf), lambda i, f: (0, f)),
            pl.BlockSpec((tf, d), lambda i, f: (f, 0)),
            pl.BlockSpec((1, d), lambda i, f: (0, 0)),
        ],
        out_specs=pl.BlockSpec((tm, d), lambda i, f: (i, 0)),
        out_shape=jax.ShapeDtypeStruct((m, d), F32),
        scratch_shapes=[pltpu.VMEM((tm, d), BF16)],
        compiler_params=_cparams(("parallel", "arbitrary")),
        name="ffn_residual",
    )(h2d, g_row, w_gate, w_up, w_down, gfin_row)


def _moba_kernel(slopes_ref, q_ref, k_ref, v_ref, o_ref, kmean_scr):
    kb = KEY_BLOCK
    nb = k_ref.shape[1] // kb
    neg_inf = -jnp.inf
    nt = (((1,), (1,)), ((), ()))
    slope = slopes_ref[pl.program_id(1)]

    kmean_scr[...] = jnp.zeros_like(kmean_scr)
    for n in range(nb):
        kmean_scr[n:n + 1, :] = jnp.mean(k_ref[0, n * kb:(n + 1) * kb, :].astype(F32), axis=0, keepdims=True)
    row = lax.broadcasted_iota(jnp.int32, (kb, kb), 0)
    colk = lax.broadcasted_iota(jnp.int32, (kb, kb), 1)
    alibi = -slope * (row - colk).astype(F32)
    col = lax.broadcasted_iota(jnp.int32, (kb, LANES), 1)
    colf = col.astype(F32)

    for own in range(nb):
        q = q_ref[0, own * kb:(own + 1) * kb, :]
        if own <= TOP_K_BLOCKS:
            rowterms = [-slope * float((own - n) * kb) for n in range(own)]
        else:
            offset = -slope * ((own - col) * kb).astype(F32)
            gate = lax.dot_general(q.astype(F32), kmean_scr[...], nt,
                                   precision=lax.Precision.HIGHEST, preferred_element_type=F32)
            gate = jnp.where(col < own, gate, neg_inf)
            rowterm = jnp.full(gate.shape, neg_inf, F32)
            for _ in range(TOP_K_BLOCKS):
                mx = jnp.max(gate, axis=-1, keepdims=True)
                idx = jnp.min(jnp.where(gate == mx, colf, float(LANES)), axis=-1, keepdims=True)
                hit = colf == idx
                rowterm = jnp.where(hit, offset, rowterm)
                gate = jnp.where(hit, neg_inf, gate)
            rowterms = [rowterm[:, n:n + 1] for n in range(own)]

        s = lax.dot_general(q, k_ref[0, 0:(own + 1) * kb, :], nt, preferred_element_type=F32)
        parts = [s[:, n * kb:(n + 1) * kb] + alibi + rowterms[n] for n in range(own)]
        parts.append(jnp.where(alibi <= 0.0, s[:, own * kb:] + alibi, neg_inf))
        s = jnp.concatenate(parts, axis=1)
        m = jnp.max(s, axis=-1, keepdims=True)
        p = jnp.exp(s - m)
        l = jnp.sum(p, axis=-1, keepdims=True)
        acc = jnp.dot(p.astype(BF16), v_ref[0, 0:(own + 1) * kb, :], preferred_element_type=F32)
        o_ref[0, own * kb:(own + 1) * kb, :] = (acc / l).astype(o_ref.dtype)


def _moba_attention(qkv3d, d_model):
    b, l, _ = qkv3d.shape
    n_heads = d_model // HEAD_DIM
    slopes = jnp.exp2(-8.0 * jnp.arange(1, n_heads + 1, dtype=F32) / n_heads)
    return pl.pallas_call(
        _moba_kernel,
        grid=(b, n_heads),
        in_specs=[
            pl.BlockSpec(memory_space=pltpu.SMEM),
            pl.BlockSpec((1, l, HEAD_DIM), lambda i, h: (i, 0, h)),
            pl.BlockSpec((1, l, HEAD_DIM), lambda i, h: (i, 0, n_heads + h)),
            pl.BlockSpec((1, l, HEAD_DIM), lambda i, h: (i, 0, 2 * n_heads + h)),
        ],
        out_specs=pl.BlockSpec((1, l, HEAD_DIM), lambda i, h: (i, 0, h)),
        out_shape=jax.ShapeDtypeStruct((b, l, d_model), BF16),
        scratch_shapes=[pltpu.VMEM((LANES, HEAD_DIM), F32)],
        compiler_params=_cparams(("parallel", "parallel")),
        name="moba_attention",
    )(slopes, qkv3d, qkv3d, qkv3d)


def kernel(x, g_mix, lambda_re, lambda_im, log_dt, b_re, b_im, c_re, c_im, d_skip, w_glu, b_glu, g_kv, w_kv,
           w_q, w_o, g_ffn, w_gate, w_up, w_down, g_final):
    bsz, l, d = x.shape
    depth = g_mix.shape[0]
    n_a = lambda_re.shape[0]
    m = bsz * l
    assert l % KEY_BLOCK == 0 and d % HEAD_DIM == 0 and l // KEY_BLOCK <= LANES
    row = lambda v: v.reshape(1, -1).astype(F32)

    h = x.astype(F32).reshape(m, d)
    qkv = None
    for i in range(depth):
        if i < n_a:
            w1, w2, a = _s5_tables(lambda_re[i], lambda_im[i], log_dt[i], b_re[i], b_im[i], c_re[i], c_im[i])
            rstd = _row_rstd(h)
            z = _s5_mixer(h.reshape(bsz, l, d), rstd.reshape(bsz, l, 1), row(g_mix[i]), row(d_skip[i]), w1, w2, a)
            h = _glu_residual(z.reshape(m, d), w_glu[i].astype(BF16), row(b_glu[i]), h)
        else:
            j = i - n_a
            w_qkv = jnp.concatenate([w_q[j], w_kv], axis=1).astype(BF16)
            qkv_i = _qkv_proj(h, row(g_mix[i]), row(g_kv), w_qkv, d, HEAD_DIM ** -0.5)
            if qkv is None:
                qkv = qkv_i
            else:
                qkv = jnp.concatenate([qkv_i[:, :d], qkv[:, d:]], axis=1)
            attn = _moba_attention(qkv.reshape(bsz, l, 3 * d), d)
            h = _proj_residual(attn.reshape(m, d), w_o[j].astype(BF16), h)
        h = _ffn_residual(h, row(g_ffn[i]), w_gate[i].astype(BF16), w_up[i].astype(BF16), w_down[i].astype(BF16),
                          row(g_final), final_norm=(i == depth - 1))
    return h.reshape(bsz, l, d).astype(x.dtype)
```

```python
import functools
import math

import jax
import jax.numpy as jnp
from jax import lax
from jax.experimental import pallas as pl
from jax.experimental.pallas import tpu as pltpu

F32 = jnp.float32
BF16 = jnp.bfloat16

EPS = 1e-6
LANES = 128
HEAD_DIM = 128
KEY_BLOCK = 256
TOP_K_BLOCKS = 3
SSM_CHUNK = 8
VMEM_LIMIT = 56 * 1024 * 1024


def _cparams(sem):
    return pltpu.CompilerParams(dimension_semantics=sem, vmem_limit_bytes=VMEM_LIMIT)


def _rms_scale(x):
    return lax.rsqrt(jnp.mean(x * x, axis=-1, keepdims=True) + EPS)


def _rstd_kernel(x_ref, o_ref):
    o_ref[...] = _rms_scale(x_ref[...])


def _row_rstd(x2d, tm=512):
    m, d = x2d.shape
    return pl.pallas_call(
        _rstd_kernel,
        grid=(m // tm,),
        in_specs=[pl.BlockSpec((tm, d), lambda i: (i, 0))],
        out_specs=pl.BlockSpec((tm, 1), lambda i: (i, 0)),
        out_shape=jax.ShapeDtypeStruct((m, 1), F32),
        compiler_params=_cparams(("parallel",)),
        name="row_rstd",
    )(x2d)


def _s5_tables(lam_re, lam_im, log_dt, b_re, b_im, c_re, c_im):
    g, p, c = b_re.shape
    t = SSM_CHUNK
    gl = LANES // c
    nj = g // gl
    ns = gl * p
    lr, li = lam_re.astype(F32), lam_im.astype(F32)
    dt = jnp.exp(log_dt.astype(F32))[:, None]
    ar, ai = lr * dt, li * dt
    mag = jnp.exp(ar)
    nr, ni = mag * jnp.cos(ai) - 1.0, mag * jnp.sin(ai)
    den = lr * lr + li * li
    fr, fi = (nr * lr + ni * li) / den, (ni * lr - nr * li) / den
    br, bi = b_re.astype(F32), b_im.astype(F32)
    bbr = fr[..., None] * br - fi[..., None] * bi
    bbi = fr[..., None] * bi + fi[..., None] * br
    dd = jnp.arange(t + 1, dtype=F32)[:, None, None]
    pmag = jnp.exp(ar[None] * dd)
    pr, pi = pmag * jnp.cos(ai[None] * dd), pmag * jnp.sin(ai[None] * dd)

    def block_diag(x):
        x = jnp.tile(x.reshape(nj, gl * c, p), (1, 1, gl))
        rg = lax.broadcasted_iota(jnp.int32, (gl * c, ns), 0) // c
        cg = lax.broadcasted_iota(jnp.int32, (gl * c, ns), 1) // p
        return jnp.where(rg == cg, x, 0.0)

    bblk = jnp.concatenate([block_diag(jnp.swapaxes(bbr, 1, 2)), block_diag(jnp.swapaxes(bbi, 1, 2))], axis=-1)
    cblk = jnp.concatenate([block_diag(c_re.astype(F32)), block_diag(c_im.astype(F32))], axis=-1)
    pw = jnp.concatenate([pr.reshape(t + 1, nj, ns), pi.reshape(t + 1, nj, ns)], axis=-1)
    pw = jnp.pad(jnp.swapaxes(pw, 0, 1), ((0, 0), (0, 16 - (t + 1)), (0, 0)))
    return bblk, cblk, pw


def _s5_build_tables(bblk_ref, cblk_ref, pw_ref, w1_scr, w2_scr):
    t = SSM_CHUNK
    ns = bblk_ref.shape[2] // 2
    yw = t * LANES
    bcat = bblk_ref[0]
    bre, bim = bcat[:, :ns], bcat[:, ns:]
    cre, cim = cblk_ref[0, :, :ns], cblk_ref[0, :, ns:]
    for d in range(t + 1):
        pr, pi = pw_ref[0, d:d + 1, :ns], pw_ref[0, d:d + 1, ns:]
        mo = jnp.concatenate([cre * pr - cim * pi, -(cre * pi + cim * pr)], axis=1).T
        if d >= 1:
            w2_scr[:, (d - 1) * LANES:d * LANES] = mo.astype(BF16)
        if d < t:
            kd = jnp.dot(bcat, mo, precision=lax.Precision.HIGHEST, preferred_element_type=F32).astype(BF16)
            for ip in range(t - d):
                w1_scr[ip * LANES:(ip + 1) * LANES, (ip + d) * LANES:(ip + d + 1) * LANES] = kd
            ip = t - 1 - d
            w1_scr[ip * LANES:(ip + 1) * LANES, yw:yw + ns] = (bre * pr - bim * pi).astype(BF16)
            w1_scr[ip * LANES:(ip + 1) * LANES, yw + ns:] = (bre * pi + bim * pr).astype(BF16)
    for ip in range(1, t):
        w1_scr[ip * LANES:(ip + 1) * LANES, 0:ip * LANES] = jnp.zeros((LANES, ip * LANES), BF16)


def _s5_kernel(h_ref, rstd_ref, g_ref, d_ref, bblk_ref, cblk_ref, pw_ref, z_ref,
               w1_scr, w2_scr, u_scr, lhs_scr, s_scr, yc_scr, y_scr):
    t = SSM_CHUNK
    nk = lhs_scr.shape[0]
    ns = s_scr.shape[1] // 2
    yw = t * LANES

    @pl.when(pl.program_id(1) == 0)
    def _():
        _s5_build_tables(bblk_ref, cblk_ref, pw_ref, w1_scr, w2_scr)

    u_scr[...] = h_ref[0] * rstd_ref[0] * g_ref[...]
    for i in range(t):
        lhs_scr[:, i * LANES:(i + 1) * LANES] = u_scr[pl.ds(i, nk, stride=t), :].astype(BF16)

    r1 = jnp.dot(lhs_scr[...], w1_scr[...], preferred_element_type=F32)
    yc_scr[...] = r1[:, :yw]
    s_scr[...] = r1[:, yw:]

    a_re = pw_ref[0, t:t + 1, :ns]
    a_im = pw_ref[0, t:t + 1, ns:]

    def scan_step(k, carry):
        e_re, e_im = carry
        row = s_scr[pl.ds(k, 1), :]
        s_scr[pl.ds(k, 1), :ns] = e_re
        s_scr[pl.ds(k, 1), ns:] = e_im
        n_re = a_re * e_re - a_im * e_im + row[:, :ns]
        n_im = a_re * e_im + a_im * e_re + row[:, ns:]
        return n_re, n_im

    zero = jnp.zeros((1, ns), F32)
    lax.fori_loop(0, nk, scan_step, (zero, zero))

    y = yc_scr[...] + jnp.dot(s_scr[...].astype(BF16), w2_scr[...], preferred_element_type=F32)
    for i in range(t):
        y_scr[pl.ds(i, nk, stride=t), :] = y[:, i * LANES:(i + 1) * LANES]

    yy = y_scr[...] + d_ref[...] * u_scr[...]
    z_ref[0] = jax.nn.gelu(yy).astype(z_ref.dtype)


def _s5_mixer(h3d, rstd3d, g_row, d_row, bblk, cblk, pw):
    b, l, d = h3d.shape
    nj = d // LANES
    nk = l // SSM_CHUNK
    ns2 = bblk.shape[2]
    yw = SSM_CHUNK * LANES
    return pl.pallas_call(
        _s5_kernel,
        grid=(nj, b),
        in_specs=[
            pl.BlockSpec((1, l, LANES), lambda j, i: (i, 0, j)),
            pl.BlockSpec((1, l, 1), lambda j, i: (i, 0, 0)),
            pl.BlockSpec((1, LANES), lambda j, i: (0, j)),
            pl.BlockSpec((1, LANES), lambda j, i: (0, j)),
            pl.BlockSpec((1, LANES, ns2), lambda j, i: (j, 0, 0)),
            pl.BlockSpec((1, LANES, ns2), lambda j, i: (j, 0, 0)),
            pl.BlockSpec((1, 16, ns2), lambda j, i: (j, 0, 0)),
        ],
        out_specs=pl.BlockSpec((1, l, LANES), lambda j, i: (i, 0, j)),
        out_shape=jax.ShapeDtypeStruct((b, l, d), BF16),
        scratch_shapes=[
            pltpu.VMEM((yw, yw + ns2), BF16),
            pltpu.VMEM((ns2, yw), BF16),
            pltpu.VMEM((l, LANES), F32),
            pltpu.VMEM((nk, yw), BF16),
            pltpu.VMEM((nk, ns2), F32),
            pltpu.VMEM((nk, yw), F32),
            pltpu.VMEM((l, LANES), F32),
        ],
        compiler_params=_cparams(("parallel", "arbitrary")),
        name="s5_mixer",
    )(h3d, rstd3d, g_row, d_row, bblk, cblk, pw)


def _glu_kernel(z_ref, w_ref, b_ref, zt_ref, res_ref, o_ref):
    a = jnp.dot(z_ref[...], w_ref[...], preferred_element_type=F32) + b_ref[...]
    o_ref[...] = res_ref[...] + zt_ref[...].astype(F32) * jax.nn.sigmoid(a)


def _glu_residual(z, w, b_row, res, tm=1024, tn=512):
    m, k = z.shape
    n = w.shape[1]
    return pl.pallas_call(
        _glu_kernel,
        grid=(m // tm, n // tn),
        in_specs=[
            pl.BlockSpec((tm, k), lambda i, j: (i, 0)),
            pl.BlockSpec((k, tn), lambda i, j: (0, j)),
            pl.BlockSpec((1, tn), lambda i, j: (0, j)),
            pl.BlockSpec((tm, tn), lambda i, j: (i, j)),
            pl.BlockSpec((tm, tn), lambda i, j: (i, j)),
        ],
        out_specs=pl.BlockSpec((tm, tn), lambda i, j: (i, j)),
        out_shape=jax.ShapeDtypeStruct((m, n), F32),
        compiler_params=_cparams(("parallel", "parallel")),
        name="glu_residual",
    )(z, w, b_row, z, res)


def _proj_res_kernel(x_ref, w_ref, res_ref, o_ref):
    o_ref[...] = res_ref[...] + jnp.dot(x_ref[...], w_ref[...], preferred_element_type=F32)


def _proj_residual(x, w, res, tm=1024, tn=512):
    m, k = x.shape
    n = w.shape[1]
    return pl.pallas_call(
        _proj_res_kernel,
        grid=(m // tm, n // tn),
        in_specs=[
            pl.BlockSpec((tm, k), lambda i, j: (i, 0)),
            pl.BlockSpec((k, tn), lambda i, j: (0, j)),
            pl.BlockSpec((tm, tn), lambda i, j: (i, j)),
        ],
        out_specs=pl.BlockSpec((tm, tn), lambda i, j: (i, j)),
        out_shape=jax.ShapeDtypeStruct((m, n), F32),
        compiler_params=_cparams(("parallel", "parallel")),
        name="proj_residual",
    )(x, w, res)


def _qkv_kernel(x_ref, gq_ref, gkv_ref, w_ref, o_ref, uq_scr, ukv_scr, *, n_q_tiles, q_scale):
    j = pl.program_id(1)

    @pl.when(j == 0)
    def _():
        x = x_ref[...]
        xn = x * _rms_scale(x)
        uq_scr[...] = (xn * gq_ref[...]).astype(BF16)
        ukv_scr[...] = (xn * gkv_ref[...]).astype(BF16)

    @pl.when(j < n_q_tiles)
    def _():
        o_ref[...] = (jnp.dot(uq_scr[...], w_ref[...], preferred_element_type=F32) * q_scale).astype(o_ref.dtype)

    @pl.when(j >= n_q_tiles)
    def _():
        o_ref[...] = jnp.dot(ukv_scr[...], w_ref[...], preferred_element_type=F32).astype(o_ref.dtype)


def _qkv_proj(h2d, gq_row, gkv_row, w_qkv, d_q, q_scale, tm=1024, tn=512):
    m, k = h2d.shape
    n = w_qkv.shape[1]
    return pl.pallas_call(
        functools.partial(_qkv_kernel, n_q_tiles=d_q // tn, q_scale=q_scale),
        grid=(m // tm, n // tn),
        in_specs=[
            pl.BlockSpec((tm, k), lambda i, j: (i, 0)),
            pl.BlockSpec((1, k), lambda i, j: (0, 0)),
            pl.BlockSpec((1, k), lambda i, j: (0, 0)),
            pl.BlockSpec((k, tn), lambda i, j: (0, j)),
        ],
        out_specs=pl.BlockSpec((tm, tn), lambda i, j: (i, j)),
        out_shape=jax.ShapeDtypeStruct((m, n), BF16),
        scratch_shapes=[pltpu.VMEM((tm, k), BF16), pltpu.VMEM((tm, k), BF16)],
        compiler_params=_cparams(("parallel", "arbitrary")),
        name="qkv_proj",
    )(h2d, gq_row, gkv_row, w_qkv)


def _ffn_kernel(h_ref, g_ref, wg_ref, wu_ref, wd_ref, gfin_ref, o_ref, u_scr, *, final_norm):
    f = pl.program_id(1)

    @pl.when(f == 0)
    def _():
        x = h_ref[...]
        u_scr[...] = (x * _rms_scale(x) * g_ref[...]).astype(BF16)
        o_ref[...] = x

    u = u_scr[...]
    gate = jnp.dot(u, wg_ref[...], preferred_element_type=F32)
    up = jnp.dot(u, wu_ref[...], preferred_element_type=F32)
    act = (jax.nn.silu(gate) * up).astype(BF16)
    o_ref[...] += jnp.dot(act, wd_ref[...], preferred_element_type=F32)

    if final_norm:
        @pl.when(f == pl.num_programs(1) - 1)
        def _():
            y = o_ref[...]
            o_ref[...] = y * _rms_scale(y) * gfin_ref[...]


def _ffn_residual(h2d, g_row, w_gate, w_up, w_down, gfin_row, final_norm, tm=1024, tf=512):
    m, d = h2d.shape
    ff = w_gate.shape[1]
    return pl.pallas_call(
        functools.partial(_ffn_kernel, final_norm=final_norm),
        grid=(m // tm, ff // tf),
        in_specs=[
            pl.BlockSpec((tm, d), lambda i, f: (i, 0), pipeline_mode=pl.Buffered(1)),
            pl.BlockSpec((1, d), lambda i, f: (0, 0)),
            pl.BlockSpec((d, tf), lambda i, f: (0, f)),
            pl.BlockSpec((d, tf), lambda i, f: (0, f)),
            pl.BlockSpec((tf, d), lambda i, f: (f, 0)),
            pl.BlockSpec((1, d), lambda i, f: (0, 0)),
        ],
        out_specs=pl.BlockSpec((tm, d), lambda i, f: (i, 0)),
        out_shape=jax.ShapeDtypeStruct((m, d), F32),
        scratch_shapes=[pltpu.VMEM((tm, d), BF16)],
        compiler_params=_cparams(("parallel", "arbitrary")),
        name="ffn_residual",
    )(h2d, g_row, w_gate, w_up, w_down, gfin_row)


def _moba_kernel(slopes_ref, q_ref, k_ref, v_ref, o_ref, kmean_scr):
    kb = KEY_BLOCK
    nb = k_ref.shape[1] // kb
    neg_inf = -jnp.inf
    nt = (((1,), (1,)), ((), ()))
    slope = slopes_ref[pl.program_id(1)]

    kmean_scr[...] = jnp.zeros_like(kmean_scr)
    for n in range(nb):
        kmean_scr[n:n + 1, :] = jnp.mean(k_ref[0, n * kb:(n + 1) * kb, :].astype(F32), axis=0, keepdims=True)
    row = lax.broadcasted_iota(jnp.int32, (kb, kb), 0)
    colk = lax.broadcasted_iota(jnp.int32, (kb, kb), 1)
    alibi = -slope * (row - colk).astype(F32)
    col = lax.broadcasted_iota(jnp.int32, (kb, LANES), 1)
    colf = col.astype(F32)

    for own in range(nb):
        q = q_ref[0, own * kb:(own + 1) * kb, :]
        if own <= TOP_K_BLOCKS:
            rowterms = [-slope * float((own - n) * kb) for n in range(own)]
        else:
            offset = -slope * ((own - col) * kb).astype(F32)
            gate = lax.dot_general(q.astype(F32), kmean_scr[...], nt,
                                   precision=lax.Precision.HIGHEST, preferred_element_type=F32)
            gate = jnp.where(col < own, gate, neg_inf)
            rowterm = jnp.full(gate.shape, neg_inf, F32)
            for _ in range(TOP_K_BLOCKS):
                mx = jnp.max(gate, axis=-1, keepdims=True)
                idx = jnp.min(jnp.where(gate == mx, colf, float(LANES)), axis=-1, keepdims=True)
                hit = colf == idx
                rowterm = jnp.where(hit, offset, rowterm)
                gate = jnp.where(hit, neg_inf, gate)
            rowterms = [rowterm[:, n:n + 1] for n in range(own)]

        s = lax.dot_general(q, k_ref[0, 0:(own + 1) * kb, :], nt, preferred_element_type=F32)
        parts = [s[:, n * kb:(n + 1) * kb] + alibi + rowterms[n] for n in range(own)]
        parts.append(jnp.where(alibi <= 0.0, s[:, own * kb:] + alibi, neg_inf))
        s = jnp.concatenate(parts, axis=1)
        m = jnp.max(s, axis=-1, keepdims=True)
        p = jnp.exp(s - m)
        l = jnp.sum(p, axis=-1, keepdims=True)
        acc = jnp.dot(p.astype(BF16), v_ref[0, 0:(own + 1) * kb, :], preferred_element_type=F32)
        o_ref[0, own * kb:(own + 1) * kb, :] = (acc / l).astype(o_ref.dtype)


def _moba_attention(qkv3d, d_model):
    b, l, _ = qkv3d.shape
    n_heads = d_model // HEAD_DIM
    slopes = jnp.exp2(-8.0 * jnp.arange(1, n_heads + 1, dtype=F32) / n_heads)
    return pl.pallas_call(
        _moba_kernel,
        grid=(b, n_heads),
        in_specs=[
            pl.BlockSpec(memory_space=pltpu.SMEM),
            pl.BlockSpec((1, l, HEAD_DIM), lambda i, h: (i, 0, h)),
            pl.BlockSpec((1, l, HEAD_DIM), lambda i, h: (i, 0, n_heads + h)),
            pl.BlockSpec((1, l, HEAD_DIM), lambda i, h: (i, 0, 2 * n_heads + h)),
        ],
        out_specs=pl.BlockSpec((1, l, HEAD_DIM), lambda i, h: (i, 0, h)),
        out_shape=jax.ShapeDtypeStruct((b, l, d_model), BF16),
        scratch_shapes=[pltpu.VMEM((LANES, HEAD_DIM), F32)],
        compiler_params=_cparams(("parallel", "parallel")),
        name="moba_attention",
    )(slopes, qkv3d, qkv3d, qkv3d)


def kernel(x, g_mix, lambda_re, lambda_im, log_dt, b_re, b_im, c_re, c_im, d_skip, w_glu, b_glu, g_kv, w_kv,
           w_q, w_o, g_ffn, w_gate, w_up, w_down, g_final):
    bsz, l, d = x.shape
    depth = g_mix.shape[0]
    n_a = lambda_re.shape[0]
    m = bsz * l
    assert l % KEY_BLOCK == 0 and d % HEAD_DIM == 0 and l // KEY_BLOCK <= LANES
    row = lambda v: v.reshape(1, -1).astype(F32)

    h = x.astype(F32).reshape(m, d)
    qkv = None
    for i in range(depth):
        if i < n_a:
            tables = _s5_tables(lambda_re[i], lambda_im[i], log_dt[i], b_re[i], b_im[i], c_re[i], c_im[i])
            rstd = _row_rstd(h)
            z = _s5_mixer(h.reshape(bsz, l, d), rstd.reshape(bsz, l, 1), row(g_mix[i]), row(d_skip[i]), *tables)
            h = _glu_residual(z.reshape(m, d), w_glu[i].astype(BF16), row(b_glu[i]), h)
        else:
            j = i - n_a
            w_qkv = jnp.concatenate([w_q[j], w_kv], axis=1).astype(BF16)
            qkv_i = _qkv_proj(h, row(g_mix[i]), row(g_kv), w_qkv, d, HEAD_DIM ** -0.5)
            if qkv is None:
                qkv = qkv_i
            else:
                qkv = jnp.concatenate([qkv_i[:, :d], qkv[:, d:]], axis=1)
            attn = _moba_attention(qkv.reshape(bsz, l, 3 * d), d)
            h = _proj_residual(attn.reshape(m, d), w_o[j].astype(BF16), h)
        h = _ffn_residual(h, row(g_ffn[i]), w_gate[i].astype(BF16), w_up[i].astype(BF16), w_down[i].astype(BF16),
                          row(g_final), final_norm=(i == depth - 1))
    return h.reshape(bsz, l, d).astype(x.dtype)
```

```python
import functools
import math

import jax
import jax.numpy as jnp
from jax import lax
from jax.experimental import pallas as pl
from jax.experimental.pallas import tpu as pltpu

F32 = jnp.float32
BF16 = jnp.bfloat16

EPS = 1e-6
LANES = 128
HEAD_DIM = 128
KEY_BLOCK = 256
TOP_K_BLOCKS = 3
SSM_CHUNK = 8
VMEM_LIMIT = 56 * 1024 * 1024


def _cparams(sem):
    return pltpu.CompilerParams(dimension_semantics=sem, vmem_limit_bytes=VMEM_LIMIT)


def _rms_scale(x):
    return lax.rsqrt(jnp.mean(x * x, axis=-1, keepdims=True) + EPS)


def _rstd_kernel(x_ref, o_ref):
    o_ref[...] = _rms_scale(x_ref[...])


def _row_rstd(x2d, tm=512):
    m, d = x2d.shape
    return pl.pallas_call(
        _rstd_kernel,
        grid=(m // tm,),
        in_specs=[pl.BlockSpec((tm, d), lambda i: (i, 0))],
        out_specs=pl.BlockSpec((tm, 1), lambda i: (i, 0)),
        out_shape=jax.ShapeDtypeStruct((m, 1), F32),
        compiler_params=_cparams(("parallel",)),
        name="row_rstd",
    )(x2d)


def _s5_tables(lam_re, lam_im, log_dt, b_re, b_im, c_re, c_im):
    g, p, c = b_re.shape
    t = SSM_CHUNK
    gl = LANES // c
    nj = g // gl
    ns = gl * p
    lr, li = lam_re.astype(F32), lam_im.astype(F32)
    dt = jnp.exp(log_dt.astype(F32))[:, None]
    ar, ai = lr * dt, li * dt
    mag = jnp.exp(ar)
    nr, ni = mag * jnp.cos(ai) - 1.0, mag * jnp.sin(ai)
    den = lr * lr + li * li
    fr, fi = (nr * lr + ni * li) / den, (ni * lr - nr * li) / den
    br, bi = b_re.astype(F32), b_im.astype(F32)
    bbr = fr[..., None] * br - fi[..., None] * bi
    bbi = fr[..., None] * bi + fi[..., None] * br
    dd = jnp.arange(t + 1, dtype=F32)[:, None, None]
    pmag = jnp.exp(ar[None] * dd)
    pr, pi = pmag * jnp.cos(ai[None] * dd), pmag * jnp.sin(ai[None] * dd)

    def block_diag(x):
        x = jnp.tile(x.reshape(nj, gl * c, p), (1, 1, gl))
        rg = lax.broadcasted_iota(jnp.int32, (gl * c, ns), 0) // c
        cg = lax.broadcasted_iota(jnp.int32, (gl * c, ns), 1) // p
        return jnp.where(rg == cg, x, 0.0)

    bblk = jnp.concatenate([block_diag(jnp.swapaxes(bbr, 1, 2)), block_diag(jnp.swapaxes(bbi, 1, 2))], axis=-1)
    cblk = jnp.concatenate([block_diag(c_re.astype(F32)), block_diag(c_im.astype(F32))], axis=-1)
    pw = jnp.concatenate([pr.reshape(t + 1, nj, ns), pi.reshape(t + 1, nj, ns)], axis=-1)
    pw = jnp.pad(jnp.swapaxes(pw, 0, 1), ((0, 0), (0, 16 - (t + 1)), (0, 0)))
    return bblk, cblk, pw


def _s5_build_tables(bblk_ref, cblk_ref, pw_ref, w1_scr, w2_scr):
    t = SSM_CHUNK
    ns = bblk_ref.shape[2] // 2
    yw = t * LANES
    bcat = bblk_ref[0]
    bre, bim = bcat[:, :ns], bcat[:, ns:]
    cre, cim = cblk_ref[0, :, :ns], cblk_ref[0, :, ns:]
    for d in range(t + 1):
        pr, pi = pw_ref[0, d:d + 1, :ns], pw_ref[0, d:d + 1, ns:]
        mo = jnp.concatenate([cre * pr - cim * pi, -(cre * pi + cim * pr)], axis=1).T
        if d >= 1:
            w2_scr[:, (d - 1) * LANES:d * LANES] = mo.astype(BF16)
        if d < t:
            kd = jnp.dot(bcat, mo, precision=lax.Precision.HIGHEST, preferred_element_type=F32).astype(BF16)
            for ip in range(t - d):
                w1_scr[ip * LANES:(ip + 1) * LANES, (ip + d) * LANES:(ip + d + 1) * LANES] = kd
            ip = t - 1 - d
            w1_scr[ip * LANES:(ip + 1) * LANES, yw:yw + ns] = (bre * pr - bim * pi).astype(BF16)
            w1_scr[ip * LANES:(ip + 1) * LANES, yw + ns:] = (bre * pi + bim * pr).astype(BF16)
    for ip in range(1, t):
        w1_scr[ip * LANES:(ip + 1) * LANES, 0:ip * LANES] = jnp.zeros((LANES, ip * LANES), BF16)


def _s5_kernel(h_ref, rstd_ref, g_ref, d_ref, bblk_ref, cblk_ref, pw_ref, z_ref,
               w1_scr, w2_scr, u_scr, lhs_scr, s_scr, yc_scr, y_scr):
    t = SSM_CHUNK
    nk = lhs_scr.shape[0]
    ns = s_scr.shape[1] // 2
    yw = t * LANES

    @pl.when(pl.program_id(1) == 0)
    def _():
        _s5_build_tables(bblk_ref, cblk_ref, pw_ref, w1_scr, w2_scr)

    u_scr[...] = h_ref[0] * rstd_ref[0] * g_ref[...]
    for i in range(t):
        lhs_scr[:, i * LANES:(i + 1) * LANES] = u_scr[pl.ds(i, nk, stride=t), :].astype(BF16)

    r1 = jnp.dot(lhs_scr[...], w1_scr[...], preferred_element_type=F32)
    yc_scr[...] = r1[:, :yw]
    s_scr[...] = r1[:, yw:]

    a_re = pw_ref[0, t:t + 1, :ns]
    a_im = pw_ref[0, t:t + 1, ns:]

    def scan_step(k, carry):
        e_re, e_im = carry
        row = s_scr[pl.ds(k, 1), :]
        s_scr[pl.ds(k, 1), :ns] = e_re
        s_scr[pl.ds(k, 1), ns:] = e_im
        n_re = a_re * e_re - a_im * e_im + row[:, :ns]
        n_im = a_re * e_im + a_im * e_re + row[:, ns:]
        return n_re, n_im

    zero = jnp.zeros((1, ns), F32)
    lax.fori_loop(0, nk, scan_step, (zero, zero))

    y = yc_scr[...] + jnp.dot(s_scr[...].astype(BF16), w2_scr[...], preferred_element_type=F32)
    for i in range(t):
        y_scr[pl.ds(i, nk, stride=t), :] = y[:, i * LANES:(i + 1) * LANES]

    yy = y_scr[...] + d_ref[...] * u_scr[...]
    z_ref[0] = jax.nn.gelu(yy).astype(z_ref.dtype)


def _s5_mixer(h3d, rstd3d, g_row, d_row, bblk, cblk, pw):
    b, l, d = h3d.shape
    nj = d // LANES
    nk = l // SSM_CHUNK
    ns2 = bblk.shape[2]
    yw = SSM_CHUNK * LANES
    return pl.pallas_call(
        _s5_kernel,
        grid=(nj, b),
        in_specs=[
            pl.BlockSpec((1, l, LANES), lambda j, i: (i, 0, j)),
            pl.BlockSpec((1, l, 1), lambda j, i: (i, 0, 0)),
            pl.BlockSpec((1, LANES), lambda j, i: (0, j)),
            pl.BlockSpec((1, LANES), lambda j, i: (0, j)),
            pl.BlockSpec((1, LANES, ns2), lambda j, i: (j, 0, 0)),
            pl.BlockSpec((1, LANES, ns2), lambda j, i: (j, 0, 0)),
            pl.BlockSpec((1, 16, ns2), lambda j, i: (j, 0, 0)),
        ],
        out_specs=pl.BlockSpec((1, l, LANES), lambda j, i: (i, 0, j)),
        out_shape=jax.ShapeDtypeStruct((b, l, d), BF16),
        scratch_shapes=[
            pltpu.VMEM((yw, yw + ns2), BF16),
            pltpu.VMEM((ns2, yw), BF16),
            pltpu.VMEM((l, LANES), F32),
            pltpu.VMEM((nk, yw), BF16),
            pltpu.VMEM((nk, ns2), F32),
            pltpu.VMEM((nk, yw), F32),
            pltpu.VMEM((l, LANES), F32),
        ],
        compiler_params=_cparams(("parallel", "arbitrary")),
        name="s5_mixer",
    )(h3d, rstd3d, g_row, d_row, bblk, cblk, pw)


def _glu_kernel(z_ref, w_ref, b_ref, zt_ref, res_ref, o_ref):
    a = jnp.dot(z_ref[...], w_ref[...].astype(BF16), preferred_element_type=F32) + b_ref[...]
    o_ref[...] = res_ref[...] + zt_ref[...].astype(F32) * jax.nn.sigmoid(a)


def _glu_residual(z, w, layer, b_row, res, tm=1024, tn=512):
    m, k = z.shape
    n = w.shape[2]
    return pl.pallas_call(
        _glu_kernel,
        grid=(m // tm, n // tn),
        in_specs=[
            pl.BlockSpec((tm, k), lambda i, j: (i, 0)),
            pl.BlockSpec((None, k, tn), lambda i, j: (layer, 0, j)),
            pl.BlockSpec((1, tn), lambda i, j: (0, j)),
            pl.BlockSpec((tm, tn), lambda i, j: (i, j)),
            pl.BlockSpec((tm, tn), lambda i, j: (i, j)),
        ],
        out_specs=pl.BlockSpec((tm, tn), lambda i, j: (i, j)),
        out_shape=jax.ShapeDtypeStruct((m, n), F32),
        compiler_params=_cparams(("parallel", "parallel")),
        name="glu_residual",
    )(z, w, b_row, z, res)


def _proj_res_kernel(x_ref, w_ref, res_ref, o_ref):
    o_ref[...] = res_ref[...] + jnp.dot(x_ref[...], w_ref[...].astype(BF16), preferred_element_type=F32)


def _proj_residual(x, w, layer, res, tm=1024, tn=512):
    m, k = x.shape
    n = w.shape[2]
    return pl.pallas_call(
        _proj_res_kernel,
        grid=(m // tm, n // tn),
        in_specs=[
            pl.BlockSpec((tm, k), lambda i, j: (i, 0)),
            pl.BlockSpec((None, k, tn), lambda i, j: (layer, 0, j)),
            pl.BlockSpec((tm, tn), lambda i, j: (i, j)),
        ],
        out_specs=pl.BlockSpec((tm, tn), lambda i, j: (i, j)),
        out_shape=jax.ShapeDtypeStruct((m, n), F32),
        compiler_params=_cparams(("parallel", "parallel")),
        name="proj_residual",
    )(x, w, res)


def _qkv_kernel(x_ref, gq_ref, gkv_ref, wq_ref, wkv_ref, o_ref, uq_scr, ukv_scr, *, n_q_tiles, q_scale):
    j = pl.program_id(1)

    @pl.when(j == 0)
    def _():
        x = x_ref[...]
        xn = x * _rms_scale(x)
        uq_scr[...] = (xn * gq_ref[...]).astype(BF16)
        ukv_scr[...] = (xn * gkv_ref[...]).astype(BF16)

    @pl.when(j < n_q_tiles)
    def _():
        q = jnp.dot(uq_scr[...], wq_ref[...].astype(BF16), preferred_element_type=F32)
        o_ref[...] = (q * q_scale).astype(o_ref.dtype)

    @pl.when(j >= n_q_tiles)
    def _():
        o_ref[...] = jnp.dot(ukv_scr[...], wkv_ref[...].astype(BF16), preferred_element_type=F32).astype(o_ref.dtype)


def _qkv_proj(h2d, gq_row, gkv_row, w_q, layer, w_kv, q_scale, tm=1024, tn=512):
    m, k = h2d.shape
    nq = w_q.shape[2] // tn
    nkv = w_kv.shape[1] // tn
    return pl.pallas_call(
        functools.partial(_qkv_kernel, n_q_tiles=nq, q_scale=q_scale),
        grid=(m // tm, nq + nkv),
        in_specs=[
            pl.BlockSpec((tm, k), lambda i, j: (i, 0), pipeline_mode=pl.Buffered(1)),
            pl.BlockSpec((1, k), lambda i, j: (0, 0)),
            pl.BlockSpec((1, k), lambda i, j: (0, 0)),
            pl.BlockSpec((None, k, tn), lambda i, j: (layer, 0, jnp.minimum(j, nq - 1))),
            pl.BlockSpec((k, tn), lambda i, j: (0, jnp.maximum(j - nq, 0))),
        ],
        out_specs=pl.BlockSpec((tm, tn), lambda i, j: (i, j)),
        out_shape=jax.ShapeDtypeStruct((m, (nq + nkv) * tn), BF16),
        scratch_shapes=[pltpu.VMEM((tm, k), BF16), pltpu.VMEM((tm, k), BF16)],
        compiler_params=_cparams(("parallel", "arbitrary")),
        name="qkv_proj",
    )(h2d, gq_row, gkv_row, w_q, w_kv)


def _ffn_kernel(h_ref, g_ref, wg_ref, wu_ref, wd_ref, gfin_ref, o_ref, u_scr, *, final_norm):
    f = pl.program_id(1)

    @pl.when(f == 0)
    def _():
        x = h_ref[...]
        u_scr[...] = (x * _rms_scale(x) * g_ref[...]).astype(BF16)
        o_ref[...] = x

    u = u_scr[...]
    gate = jnp.dot(u, wg_ref[...].astype(BF16), preferred_element_type=F32)
    up = jnp.dot(u, wu_ref[...].astype(BF16), preferred_element_type=F32)
    act = (jax.nn.silu(gate) * up).astype(BF16)
    o_ref[...] += jnp.dot(act, wd_ref[...].astype(BF16), preferred_element_type=F32)

    if final_norm:
        @pl.when(f == pl.num_programs(1) - 1)
        def _():
            y = o_ref[...]
            o_ref[...] = y * _rms_scale(y) * gfin_ref[...]


def _ffn_residual(h2d, g_row, w_gate, w_up, w_down, layer, gfin_row, final_norm, tm=1024, tf=256):
    m, d = h2d.shape
    ff = w_gate.shape[2]
    return pl.pallas_call(
        functools.partial(_ffn_kernel, final_norm=final_norm),
        grid=(m // tm, ff // tf),
        in_specs=[
            pl.BlockSpec((tm, d), lambda i, f: (i, 0), pipeline_mode=pl.Buffered(1)),
            pl.BlockSpec((1, d), lambda i, f: (0, 0)),
            pl.BlockSpec((None, d, tf), lambda i, f: (layer, 0, f)),
            pl.BlockSpec((None, d, tf), lambda i, f: (layer, 0, f)),
            pl.BlockSpec((None, tf, d), lambda i, f: (layer, f, 0)),
            pl.BlockSpec((1, d), lambda i, f: (0, 0)),
        ],
        out_specs=pl.BlockSpec((tm, d), lambda i, f: (i, 0)),
        out_shape=jax.ShapeDtypeStruct((m, d), F32),
        scratch_shapes=[pltpu.VMEM((tm, d), BF16)],
        compiler_params=_cparams(("parallel", "arbitrary")),
        name="ffn_residual",
    )(h2d, g_row, w_gate, w_up, w_down, gfin_row)


def _moba_kernel(slopes_ref, q_ref, k_ref, v_ref, o_ref, kmean_scr):
    kb = KEY_BLOCK
    nb = k_ref.shape[1] // kb
    neg_inf = -jnp.inf
    nt = (((1,), (1,)), ((), ()))
    slope = slopes_ref[pl.program_id(1)]

    kmean_scr[...] = jnp.zeros_like(kmean_scr)
    for n in range(nb):
        kmean_scr[n:n + 1, :] = jnp.mean(k_ref[0, n * kb:(n + 1) * kb, :].astype(F32), axis=0, keepdims=True)
    row = lax.broadcasted_iota(jnp.int32, (kb, kb), 0)
    colk = lax.broadcasted_iota(jnp.int32, (kb, kb), 1)
    alibi = -slope * (row - colk).astype(F32)
    col = lax.broadcasted_iota(jnp.int32, (kb, LANES), 1)
    colf = col.astype(F32)

    for own in range(nb):
        q = q_ref[0, own * kb:(own + 1) * kb, :]
        if own <= TOP_K_BLOCKS:
            rowterms = [-slope * float((own - n) * kb) for n in range(own)]
        else:
            offset = -slope * ((own - col) * kb).astype(F32)
            gate = lax.dot_general(q.astype(F32), kmean_scr[...], nt,
                                   precision=lax.Precision.HIGHEST, preferred_element_type=F32)
            gate = jnp.where(col < own, gate, neg_inf)
            rowterm = jnp.full(gate.shape, neg_inf, F32)
            for _ in range(TOP_K_BLOCKS):
                mx = jnp.max(gate, axis=-1, keepdims=True)
                idx = jnp.min(jnp.where(gate == mx, colf, float(LANES)), axis=-1, keepdims=True)
                hit = colf == idx
                rowterm = jnp.where(hit, offset, rowterm)
                gate = jnp.where(hit, neg_inf, gate)
            rowterms = [rowterm[:, n:n + 1] for n in range(own)]

        s = lax.dot_general(q, k_ref[0, 0:(own + 1) * kb, :], nt, preferred_element_type=F32)
        parts = [s[:, n * kb:(n + 1) * kb] + alibi + rowterms[n] for n in range(own)]
        parts.append(jnp.where(alibi <= 0.0, s[:, own * kb:] + alibi, neg_inf))
        s = jnp.concatenate(parts, axis=1)
        m = jnp.max(s, axis=-1, keepdims=True)
        p = jnp.exp(s - m)
        l = jnp.sum(p, axis=-1, keepdims=True)
        acc = jnp.dot(p.astype(BF16), v_ref[0, 0:(own + 1) * kb, :], preferred_element_type=F32)
        o_ref[0, own * kb:(own + 1) * kb, :] = (acc / l).astype(o_ref.dtype)


def _moba_attention(qkv3d, d_model):
    b, l, _ = qkv3d.shape
    n_heads = d_model // HEAD_DIM
    slopes = jnp.exp2(-8.0 * jnp.arange(1, n_heads + 1, dtype=F32) / n_heads)
    return pl.pallas_call(
        _moba_kernel,
        grid=(b, n_heads),
        in_specs=[
            pl.BlockSpec(memory_space=pltpu.SMEM),
            pl.BlockSpec((1, l, HEAD_DIM), lambda i, h: (i, 0, h)),
            pl.BlockSpec((1, l, HEAD_DIM), lambda i, h: (i, 0, n_heads + h)),
            pl.BlockSpec((1, l, HEAD_DIM), lambda i, h: (i, 0, 2 * n_heads + h)),
        ],
        out_specs=pl.BlockSpec((1, l, HEAD_DIM), lambda i, h: (i, 0, h)),
        out_shape=jax.ShapeDtypeStruct((b, l, d_model), BF16),
        scratch_shapes=[pltpu.VMEM((LANES, HEAD_DIM), F32)],
        compiler_params=_cparams(("parallel", "parallel")),
        name="moba_attention",
    )(slopes, qkv3d, qkv3d, qkv3d)


def kernel(x, g_mix, lambda_re, lambda_im, log_dt, b_re, b_im, c_re, c_im, d_skip, w_glu, b_glu, g_kv, w_kv,
           w_q, w_o, g_ffn, w_gate, w_up, w_down, g_final):
    bsz, l, d = x.shape
    depth = g_mix.shape[0]
    n_a = lambda_re.shape[0]
    m = bsz * l
    assert l % KEY_BLOCK == 0 and d % HEAD_DIM == 0 and l // KEY_BLOCK <= LANES
    row = lambda v: v.reshape(1, -1).astype(F32)

    h = x.astype(F32).reshape(m, d)
    qkv = None
    for i in range(depth):
        if i < n_a:
            tables = _s5_tables(lambda_re[i], lambda_im[i], log_dt[i], b_re[i], b_im[i], c_re[i], c_im[i])
            rstd = _row_rstd(h)
            z = _s5_mixer(h.reshape(bsz, l, d), rstd.reshape(bsz, l, 1), row(g_mix[i]), row(d_skip[i]), *tables)
            h = _glu_residual(z.reshape(m, d), w_glu, i, row(b_glu[i]), h)
        else:
            j = i - n_a
            qkv_i = _qkv_proj(h, row(g_mix[i]), row(g_kv), w_q, j, w_kv, HEAD_DIM ** -0.5)
            if qkv is None:
                qkv = qkv_i
            else:
                qkv = jnp.concatenate([qkv_i[:, :d], qkv[:, d:]], axis=1)
            attn = _moba_attention(qkv.reshape(bsz, l, 3 * d), d)
            h = _proj_residual(attn.reshape(m, d), w_o, j, h)
        h = _ffn_residual(h, row(g_ffn[i]), w_gate, w_up, w_down, i, row(g_final), final_norm=(i == depth - 1))
    return h.reshape(bsz, l, d).astype(x.dtype)
```

```python
import functools
import math

import jax
import jax.numpy as jnp
from jax import lax
from jax.experimental import pallas as pl
from jax.experimental.pallas import tpu as pltpu

F32 = jnp.float32
BF16 = jnp.bfloat16

EPS = 1e-6
LANES = 128
HEAD_DIM = 128
KEY_BLOCK = 256
TOP_K_BLOCKS = 3
MOBA_ROW_CHUNK = 16
LOG2_E = math.log2(math.e)
SSM_CHUNK = 8
VMEM_LIMIT = 56 * 1024 * 1024


def _cparams(sem):
    return pltpu.CompilerParams(dimension_semantics=sem, vmem_limit_bytes=VMEM_LIMIT)


def _rms_scale(x):
    return lax.rsqrt(jnp.mean(x * x, axis=-1, keepdims=True) + EPS)


def _rstd_kernel(x_ref, o_ref):
    o_ref[...] = _rms_scale(x_ref[...])


def _row_rstd(x2d, tm=512):
    m, d = x2d.shape
    return pl.pallas_call(
        _rstd_kernel,
        grid=(m // tm,),
        in_specs=[pl.BlockSpec((tm, d), lambda i: (i, 0))],
        out_specs=pl.BlockSpec((tm, 1), lambda i: (i, 0)),
        out_shape=jax.ShapeDtypeStruct((m, 1), F32),
        compiler_params=_cparams(("parallel",)),
        name="row_rstd",
    )(x2d)


def _s5_tables(lam_re, lam_im, log_dt, b_re, b_im, c_re, c_im):
    g, p, c = b_re.shape
    t = SSM_CHUNK
    gl = LANES // c
    nj = g // gl
    ns = gl * p
    lr, li = lam_re.astype(F32), lam_im.astype(F32)
    dt = jnp.exp(log_dt.astype(F32))[:, None]
    ar, ai = lr * dt, li * dt
    mag = jnp.exp(ar)
    nr, ni = mag * jnp.cos(ai) - 1.0, mag * jnp.sin(ai)
    den = lr * lr + li * li
    fr, fi = (nr * lr + ni * li) / den, (ni * lr - nr * li) / den
    br, bi = b_re.astype(F32), b_im.astype(F32)
    bbr = fr[..., None] * br - fi[..., None] * bi
    bbi = fr[..., None] * bi + fi[..., None] * br
    dd = jnp.arange(t + 1, dtype=F32)[:, None, None]
    pmag = jnp.exp(ar[None] * dd)
    pr, pi = pmag * jnp.cos(ai[None] * dd), pmag * jnp.sin(ai[None] * dd)

    def block_diag(x):
        x = jnp.tile(x.reshape(nj, gl * c, p), (1, 1, gl))
        rg = lax.broadcasted_iota(jnp.int32, (gl * c, ns), 0) // c
        cg = lax.broadcasted_iota(jnp.int32, (gl * c, ns), 1) // p
        return jnp.where(rg == cg, x, 0.0)

    bblk = jnp.concatenate([block_diag(jnp.swapaxes(bbr, 1, 2)), block_diag(jnp.swapaxes(bbi, 1, 2))], axis=-1)
    cblk = jnp.concatenate([block_diag(c_re.astype(F32)), block_diag(c_im.astype(F32))], axis=-1)
    pw = jnp.concatenate([pr.reshape(t + 1, nj, ns), pi.reshape(t + 1, nj, ns)], axis=-1)
    pw = jnp.pad(jnp.swapaxes(pw, 0, 1), ((0, 0), (0, 16 - (t + 1)), (0, 0)))
    return bblk, cblk, pw


def _s5_build_tables(bblk_ref, cblk_ref, pw_ref, w1_scr, w2_scr):
    t = SSM_CHUNK
    ns = bblk_ref.shape[2] // 2
    yw = t * LANES
    bcat = bblk_ref[0]
    bre, bim = bcat[:, :ns], bcat[:, ns:]
    cre, cim = cblk_ref[0, :, :ns], cblk_ref[0, :, ns:]
    for d in range(t + 1):
        pr, pi = pw_ref[0, d:d + 1, :ns], pw_ref[0, d:d + 1, ns:]
        mo = jnp.concatenate([cre * pr - cim * pi, -(cre * pi + cim * pr)], axis=1).T
        if d >= 1:
            w2_scr[:, (d - 1) * LANES:d * LANES] = mo.astype(BF16)
        if d < t:
            kd = jnp.dot(bcat, mo, precision=lax.Precision.HIGHEST, preferred_element_type=F32).astype(BF16)
            for ip in range(t - d):
                w1_scr[ip * LANES:(ip + 1) * LANES, (ip + d) * LANES:(ip + d + 1) * LANES] = kd
            ip = t - 1 - d
            w1_scr[ip * LANES:(ip + 1) * LANES, yw:yw + ns] = (bre * pr - bim * pi).astype(BF16)
            w1_scr[ip * LANES:(ip + 1) * LANES, yw + ns:] = (bre * pi + bim * pr).astype(BF16)
    for ip in range(1, t):
        w1_scr[ip * LANES:(ip + 1) * LANES, 0:ip * LANES] = jnp.zeros((LANES, ip * LANES), BF16)


def _s5_kernel(h_ref, rstd_ref, g_ref, d_ref, bblk_ref, cblk_ref, pw_ref, z_ref,
               w1_scr, w2_scr, u_scr, lhs_scr, s_scr, yc_scr, y_scr):
    t = SSM_CHUNK
    nk = lhs_scr.shape[0]
    ns = s_scr.shape[1] // 2
    yw = t * LANES

    @pl.when(pl.program_id(1) == 0)
    def _():
        _s5_build_tables(bblk_ref, cblk_ref, pw_ref, w1_scr, w2_scr)

    u_scr[...] = h_ref[0] * rstd_ref[0] * g_ref[...]
    for i in range(t):
        lhs_scr[:, i * LANES:(i + 1) * LANES] = u_scr[pl.ds(i, nk, stride=t), :].astype(BF16)

    r1 = jnp.dot(lhs_scr[...], w1_scr[...], preferred_element_type=F32)
    yc_scr[...] = r1[:, :yw]
    s_scr[...] = r1[:, yw:]

    a_re = pw_ref[0, t:t + 1, :ns]
    a_im = pw_ref[0, t:t + 1, ns:]

    def scan_step(k, carry):
        e_re, e_im = carry
        row = s_scr[pl.ds(k, 1), :]
        s_scr[pl.ds(k, 1), :ns] = e_re
        s_scr[pl.ds(k, 1), ns:] = e_im
        n_re = a_re * e_re - a_im * e_im + row[:, :ns]
        n_im = a_re * e_im + a_im * e_re + row[:, ns:]
        return n_re, n_im

    zero = jnp.zeros((1, ns), F32)
    lax.fori_loop(0, nk, scan_step, (zero, zero))

    y = yc_scr[...] + jnp.dot(s_scr[...].astype(BF16), w2_scr[...], preferred_element_type=F32)
    for i in range(t):
        y_scr[pl.ds(i, nk, stride=t), :] = y[:, i * LANES:(i + 1) * LANES]

    yy = y_scr[...] + d_ref[...] * u_scr[...]
    z_ref[0] = jax.nn.gelu(yy).astype(z_ref.dtype)


def _s5_mixer(h3d, rstd3d, g_row, d_row, bblk, cblk, pw):
    b, l, d = h3d.shape
    nj = d // LANES
    nk = l // SSM_CHUNK
    ns2 = bblk.shape[2]
    yw = SSM_CHUNK * LANES
    return pl.pallas_call(
        _s5_kernel,
        grid=(nj, b),
        in_specs=[
            pl.BlockSpec((1, l, LANES), lambda j, i: (i, 0, j)),
            pl.BlockSpec((1, l, 1), lambda j, i: (i, 0, 0)),
            pl.BlockSpec((1, LANES), lambda j, i: (0, j)),
            pl.BlockSpec((1, LANES), lambda j, i: (0, j)),
            pl.BlockSpec((1, LANES, ns2), lambda j, i: (j, 0, 0)),
            pl.BlockSpec((1, LANES, ns2), lambda j, i: (j, 0, 0)),
            pl.BlockSpec((1, 16, ns2), lambda j, i: (j, 0, 0)),
        ],
        out_specs=pl.BlockSpec((1, l, LANES), lambda j, i: (i, 0, j)),
        out_shape=jax.ShapeDtypeStruct((b, l, d), BF16),
        scratch_shapes=[
            pltpu.VMEM((yw, yw + ns2), BF16),
            pltpu.VMEM((ns2, yw), BF16),
            pltpu.VMEM((l, LANES), F32),
            pltpu.VMEM((nk, yw), BF16),
            pltpu.VMEM((nk, ns2), F32),
            pltpu.VMEM((nk, yw), F32),
            pltpu.VMEM((l, LANES), F32),
        ],
        compiler_params=_cparams(("parallel", "arbitrary")),
        name="s5_mixer",
    )(h3d, rstd3d, g_row, d_row, bblk, cblk, pw)


def _glu_kernel(z_ref, w_ref, b_ref, zt_ref, res_ref, o_ref):
    a = jnp.dot(z_ref[...], w_ref[...].astype(BF16), preferred_element_type=F32) + b_ref[...]
    o_ref[...] = res_ref[...] + zt_ref[...].astype(F32) * jax.nn.sigmoid(a)


def _glu_residual(z, w, layer, b_row, res, tm=1024, tn=512):
    m, k = z.shape
    n = w.shape[2]
    return pl.pallas_call(
        _glu_kernel,
        grid=(m // tm, n // tn),
        in_specs=[
            pl.BlockSpec((tm, k), lambda i, j: (i, 0)),
            pl.BlockSpec((None, k, tn), lambda i, j: (layer, 0, j)),
            pl.BlockSpec((1, tn), lambda i, j: (0, j)),
            pl.BlockSpec((tm, tn), lambda i, j: (i, j)),
            pl.BlockSpec((tm, tn), lambda i, j: (i, j)),
        ],
        out_specs=pl.BlockSpec((tm, tn), lambda i, j: (i, j)),
        out_shape=jax.ShapeDtypeStruct((m, n), F32),
        compiler_params=_cparams(("parallel", "parallel")),
        name="glu_residual",
    )(z, w, b_row, z, res)


def _proj_res_kernel(x_ref, w_ref, res_ref, o_ref):
    o_ref[...] = res_ref[...] + jnp.dot(x_ref[...], w_ref[...].astype(BF16), preferred_element_type=F32)


def _proj_residual(x, w, layer, res, tm=1024, tn=512):
    m, k = x.shape
    n = w.shape[2]
    return pl.pallas_call(
        _proj_res_kernel,
        grid=(m // tm, n // tn),
        in_specs=[
            pl.BlockSpec((tm, k), lambda i, j: (i, 0)),
            pl.BlockSpec((None, k, tn), lambda i, j: (layer, 0, j)),
            pl.BlockSpec((tm, tn), lambda i, j: (i, j)),
        ],
        out_specs=pl.BlockSpec((tm, tn), lambda i, j: (i, j)),
        out_shape=jax.ShapeDtypeStruct((m, n), F32),
        compiler_params=_cparams(("parallel", "parallel")),
        name="proj_residual",
    )(x, w, res)


def _qkv_kernel(x_ref, gq_ref, gkv_ref, wq_ref, wkv_ref, o_ref, uq_scr, ukv_scr, *, n_q_tiles, q_scale):
    j = pl.program_id(1)

    @pl.when(j == 0)
    def _():
        x = x_ref[...]
        xn = x * _rms_scale(x)
        uq_scr[...] = (xn * gq_ref[...]).astype(BF16)
        ukv_scr[...] = (xn * gkv_ref[...]).astype(BF16)

    @pl.when(j < n_q_tiles)
    def _():
        q = jnp.dot(uq_scr[...], wq_ref[...].astype(BF16), preferred_element_type=F32)
        o_ref[...] = (q * q_scale).astype(o_ref.dtype)

    @pl.when(j >= n_q_tiles)
    def _():
        o_ref[...] = jnp.dot(ukv_scr[...], wkv_ref[...].astype(BF16), preferred_element_type=F32).astype(o_ref.dtype)


def _qkv_proj(h2d, gq_row, gkv_row, w_q, layer, w_kv, q_scale, tm=1024, tn=512):
    m, k = h2d.shape
    nq = w_q.shape[2] // tn
    nkv = w_kv.shape[1] // tn
    return pl.pallas_call(
        functools.partial(_qkv_kernel, n_q_tiles=nq, q_scale=q_scale),
        grid=(m // tm, nq + nkv),
        in_specs=[
            pl.BlockSpec((tm, k), lambda i, j: (i, 0), pipeline_mode=pl.Buffered(1)),
            pl.BlockSpec((1, k), lambda i, j: (0, 0)),
            pl.BlockSpec((1, k), lambda i, j: (0, 0)),
            pl.BlockSpec((None, k, tn), lambda i, j: (layer, 0, jnp.minimum(j, nq - 1))),
            pl.BlockSpec((k, tn), lambda i, j: (0, jnp.maximum(j - nq, 0))),
        ],
        out_specs=pl.BlockSpec((tm, tn), lambda i, j: (i, j)),
        out_shape=jax.ShapeDtypeStruct((m, (nq + nkv) * tn), BF16),
        scratch_shapes=[pltpu.VMEM((tm, k), BF16), pltpu.VMEM((tm, k), BF16)],
        compiler_params=_cparams(("parallel", "arbitrary")),
        name="qkv_proj",
    )(h2d, gq_row, gkv_row, w_q, w_kv)


def _ffn_kernel(h_ref, g_ref, wg_ref, wu_ref, wd_ref, gfin_ref, o_ref, u_scr, *, final_norm):
    f = pl.program_id(1)

    @pl.when(f == 0)
    def _():
        x = h_ref[...]
        u_scr[...] = (x * _rms_scale(x) * g_ref[...]).astype(BF16)
        o_ref[...] = x

    u = u_scr[...]
    gate = jnp.dot(u, wg_ref[...].astype(BF16), preferred_element_type=F32)
    up = jnp.dot(u, wu_ref[...].astype(BF16), preferred_element_type=F32)
    act = (jax.nn.silu(gate) * up).astype(BF16)
    o_ref[...] += jnp.dot(act, wd_ref[...].astype(BF16), preferred_element_type=F32)

    if final_norm:
        @pl.when(f == pl.num_programs(1) - 1)
        def _():
            y = o_ref[...]
            o_ref[...] = y * _rms_scale(y) * gfin_ref[...]


def _ffn_residual(h2d, g_row, w_gate, w_up, w_down, layer, gfin_row, final_norm, tm=1024, tf=256):
    m, d = h2d.shape
    ff = w_gate.shape[2]
    return pl.pallas_call(
        functools.partial(_ffn_kernel, final_norm=final_norm),
        grid=(m // tm, ff // tf),
        in_specs=[
            pl.BlockSpec((tm, d), lambda i, f: (i, 0), pipeline_mode=pl.Buffered(1)),
            pl.BlockSpec((1, d), lambda i, f: (0, 0)),
            pl.BlockSpec((None, d, tf), lambda i, f: (layer, 0, f)),
            pl.BlockSpec((None, d, tf), lambda i, f: (layer, 0, f)),
            pl.BlockSpec((None, tf, d), lambda i, f: (layer, f, 0)),
            pl.BlockSpec((1, d), lambda i, f: (0, 0)),
        ],
        out_specs=pl.BlockSpec((tm, d), lambda i, f: (i, 0)),
        out_shape=jax.ShapeDtypeStruct((m, d), F32),
        scratch_shapes=[pltpu.VMEM((tm, d), BF16)],
        compiler_params=_cparams(("parallel", "arbitrary")),
        name="ffn_residual",
    )(h2d, g_row, w_gate, w_up, w_down, gfin_row)


def _moba_kernel(slopes_ref, q_ref, k_ref, v_ref, o_ref, kmean_scr, alibi_scr, rt_scr, s_scr, p_scr, linv_scr):
    kb = KEY_BLOCK
    rc_rows = MOBA_ROW_CHUNK
    nb = k_ref.shape[1] // kb
    neg_inf = -jnp.inf
    nt = (((1,), (1,)), ((), ()))
    slope = slopes_ref[pl.program_id(1)]

    kmean_scr[...] = jnp.zeros_like(kmean_scr)
    for n in range(nb):
        kmean_scr[n:n + 1, :] = jnp.mean(k_ref[0, n * kb:(n + 1) * kb, :].astype(F32), axis=0, keepdims=True)
    row = lax.broadcasted_iota(jnp.int32, (kb, kb), 0)
    colk = lax.broadcasted_iota(jnp.int32, (kb, kb), 1)
    alibi_scr[...] = -slope * (row - colk).astype(F32)

    g0 = (TOP_K_BLOCKS + 1) * kb
    if nb * kb > g0:
        ng = nb * kb - g0
        gate = lax.dot_general(q_ref[0, g0:, :].astype(F32), kmean_scr[...], nt,
                               precision=lax.Precision.HIGHEST, preferred_element_type=F32)
        col = lax.broadcasted_iota(jnp.int32, (ng, LANES), 1)
        colf = col.astype(F32)
        own_v = lax.div(lax.broadcasted_iota(jnp.int32, (ng, LANES), 0), jnp.int32(kb)) + (TOP_K_BLOCKS + 1)
        offset = -slope * ((own_v - col) * kb).astype(F32)
        gate = jnp.where(col < own_v, gate, neg_inf)
        rowterm = jnp.full(gate.shape, neg_inf, F32)
        for _ in range(TOP_K_BLOCKS):
            mx = jnp.max(gate, axis=-1, keepdims=True)
            idx = jnp.min(jnp.where(gate == mx, colf, float(LANES)), axis=-1, keepdims=True)
            hit = colf == idx
            rowterm = jnp.where(hit, offset, rowterm)
            gate = jnp.where(hit, neg_inf, gate)
        rt_scr[g0:, :] = rowterm

    for own in range(nb):
        slot = own % 2
        width = (own + 1) * kb
        q0 = own * kb
        q = q_ref[0, q0:q0 + kb, :]
        s_scr[slot, :, 0:width] = lax.dot_general(q, k_ref[0, 0:width, :], nt, preferred_element_type=F32)
        gated = own > TOP_K_BLOCKS

        for rc in range(kb // rc_rows):
            r0, r1 = rc * rc_rows, (rc + 1) * rc_rows
            ab = alibi_scr[r0:r1, :]
            parts = []
            for n in range(own):
                rt = rt_scr[q0 + r0:q0 + r1, n:n + 1] if gated else -slope * float((own - n) * kb)
                parts.append(s_scr[slot, r0:r1, n * kb:(n + 1) * kb] + ab + rt)
            parts.append(jnp.where(ab <= 0.0, s_scr[slot, r0:r1, own * kb:width] + ab, neg_inf))
            m = jnp.max(functools.reduce(jnp.maximum, parts), axis=-1, keepdims=True)
            probs = [jnp.exp2(x - m) for x in parts]
            l = jnp.sum(functools.reduce(jnp.add, probs), axis=-1, keepdims=True)
            for n in range(own + 1):
                p_scr[slot, r0:r1, n * kb:(n + 1) * kb] = probs[n].astype(BF16)
            linv_scr[slot, r0:r1, :] = jnp.broadcast_to(1.0 / l, (rc_rows, LANES))

        acc = jnp.dot(p_scr[slot, :, 0:width], v_ref[0, 0:width, :], preferred_element_type=F32)
        o_ref[0, own * kb:(own + 1) * kb, :] = (acc * linv_scr[slot]).astype(o_ref.dtype)


def _moba_attention(qkv3d, d_model):
    b, l, _ = qkv3d.shape
    n_heads = d_model // HEAD_DIM
    kb = KEY_BLOCK
    slopes = jnp.exp2(-8.0 * jnp.arange(1, n_heads + 1, dtype=F32) / n_heads) * LOG2_E
    return pl.pallas_call(
        _moba_kernel,
        grid=(b, n_heads),
        in_specs=[
            pl.BlockSpec(memory_space=pltpu.SMEM),
            pl.BlockSpec((1, l, HEAD_DIM), lambda i, h: (i, 0, h)),
            pl.BlockSpec((1, l, HEAD_DIM), lambda i, h: (i, 0, n_heads + h)),
            pl.BlockSpec((1, l, HEAD_DIM), lambda i, h: (i, 0, 2 * n_heads + h)),
        ],
        out_specs=pl.BlockSpec((1, l, HEAD_DIM), lambda i, h: (i, 0, h)),
        out_shape=jax.ShapeDtypeStruct((b, l, d_model), BF16),
        scratch_shapes=[
            pltpu.VMEM((LANES, HEAD_DIM), F32),
            pltpu.VMEM((kb, kb), F32),
            pltpu.VMEM((l, LANES), F32),
            pltpu.VMEM((2, kb, l), F32),
            pltpu.VMEM((2, kb, l), BF16),
            pltpu.VMEM((2, kb, LANES), F32),
        ],
        compiler_params=_cparams(("parallel", "parallel")),
        name="moba_attention",
    )(slopes, qkv3d, qkv3d, qkv3d)


def kernel(x, g_mix, lambda_re, lambda_im, log_dt, b_re, b_im, c_re, c_im, d_skip, w_glu, b_glu, g_kv, w_kv,
           w_q, w_o, g_ffn, w_gate, w_up, w_down, g_final):
    bsz, l, d = x.shape
    depth = g_mix.shape[0]
    n_a = lambda_re.shape[0]
    m = bsz * l
    assert l % KEY_BLOCK == 0 and d % HEAD_DIM == 0 and l // KEY_BLOCK <= LANES
    row = lambda v: v.reshape(1, -1).astype(F32)

    h = x.astype(F32).reshape(m, d)
    qkv = None
    for i in range(depth):
        if i < n_a:
            tables = _s5_tables(lambda_re[i], lambda_im[i], log_dt[i], b_re[i], b_im[i], c_re[i], c_im[i])
            rstd = _row_rstd(h)
            z = _s5_mixer(h.reshape(bsz, l, d), rstd.reshape(bsz, l, 1), row(g_mix[i]), row(d_skip[i]), *tables)
            h = _glu_residual(z.reshape(m, d), w_glu, i, row(b_glu[i]), h)
        else:
            j = i - n_a
            qkv_i = _qkv_proj(h, row(g_mix[i]), row(g_kv), w_q, j, w_kv, HEAD_DIM ** -0.5 * LOG2_E)
            if qkv is None:
                qkv = qkv_i
            else:
                qkv = jnp.concatenate([qkv_i[:, :d], qkv[:, d:]], axis=1)
            attn = _moba_attention(qkv.reshape(bsz, l, 3 * d), d)
            h = _proj_residual(attn.reshape(m, d), w_o, j, h)
        h = _ffn_residual(h, row(g_ffn[i]), w_gate, w_up, w_down, i, row(g_final), final_norm=(i == depth - 1))
    return h.reshape(bsz, l, d).astype(x.dtype)
```

```python
import functools
import math

import jax
import jax.numpy as jnp
from jax import lax
from jax.experimental import pallas as pl
from jax.experimental.pallas import tpu as pltpu

F32 = jnp.float32
BF16 = jnp.bfloat16

EPS = 1e-6
LANES = 128
HEAD_DIM = 128
KEY_BLOCK = 256
TOP_K_BLOCKS = 3
MOBA_ROW_CHUNK = 16
LOG2_E = math.log2(math.e)
SSM_CHUNK = 8
S5_GROUP = 16
VMEM_LIMIT = 56 * 1024 * 1024


def _cparams(sem):
    return pltpu.CompilerParams(dimension_semantics=sem, vmem_limit_bytes=VMEM_LIMIT)


def _rms_scale(x):
    return lax.rsqrt(jnp.mean(x * x, axis=-1, keepdims=True) + EPS)


def _rmsnorm_kernel(x_ref, g_ref, o_ref):
    x = x_ref[...]
    o_ref[...] = (x * _rms_scale(x) * g_ref[...]).reshape(o_ref.shape)


def _rmsnorm_interleaved(x2d, g_row, bsz, tm=512):
    m, d = x2d.shape
    tiles = m // bsz // tm
    return pl.pallas_call(
        _rmsnorm_kernel,
        grid=(bsz, tiles),
        in_specs=[pl.BlockSpec((tm, d), lambda b, i: (b * tiles + i, 0)), pl.BlockSpec((1, d), lambda b, i: (0, 0))],
        out_specs=pl.BlockSpec((tm // S5_GROUP, None, S5_GROUP, d), lambda b, i: (i, b, 0, 0)),
        out_shape=jax.ShapeDtypeStruct((m // bsz // S5_GROUP, bsz, S5_GROUP, d), F32),
        compiler_params=_cparams(("parallel", "parallel")),
        name="rmsnorm",
    )(x2d, g_row)


def _s5_tables(lam_re, lam_im, log_dt, b_re, b_im, c_re, c_im):
    g, p, c = b_re.shape
    t = SSM_CHUNK
    gl = LANES // c
    nj = g // gl
    ns = gl * p
    lr, li = lam_re.astype(F32), lam_im.astype(F32)
    dt = jnp.exp(log_dt.astype(F32))[:, None]
    ar, ai = lr * dt, li * dt
    mag = jnp.exp(ar)
    nr, ni = mag * jnp.cos(ai) - 1.0, mag * jnp.sin(ai)
    den = lr * lr + li * li
    fr, fi = (nr * lr + ni * li) / den, (ni * lr - nr * li) / den
    br, bi = b_re.astype(F32), b_im.astype(F32)
    bbr = fr[..., None] * br - fi[..., None] * bi
    bbi = fr[..., None] * bi + fi[..., None] * br
    dd = jnp.arange(t + 1, dtype=F32)[:, None, None]
    pmag = jnp.exp(ar[None] * dd)
    pr, pi = pmag * jnp.cos(ai[None] * dd), pmag * jnp.sin(ai[None] * dd)

    def block_diag(x):
        x = jnp.tile(x.reshape(nj, gl * c, p), (1, 1, gl))
        rg = lax.broadcasted_iota(jnp.int32, (gl * c, ns), 0) // c
        cg = lax.broadcasted_iota(jnp.int32, (gl * c, ns), 1) // p
        return jnp.where(rg == cg, x, 0.0)

    bblk = jnp.concatenate([block_diag(jnp.swapaxes(bbr, 1, 2)), block_diag(jnp.swapaxes(bbi, 1, 2))], axis=-1)
    cblk = jnp.concatenate([block_diag(c_re.astype(F32)), block_diag(c_im.astype(F32))], axis=-1)
    pw = jnp.concatenate([pr.reshape(t + 1, nj, ns), pi.reshape(t + 1, nj, ns)], axis=-1)
    pw = jnp.pad(jnp.swapaxes(pw, 0, 1), ((0, 0), (0, 16 - (t + 1)), (0, 0)))
    return bblk, cblk, pw


def _s5_build_tables(bblk_ref, cblk_ref, pw_ref, w1_scr, w2_scr):
    t = SSM_CHUNK
    ns = bblk_ref.shape[2] // 2
    yw = t * LANES
    bcat = bblk_ref[0]
    bre, bim = bcat[:, :ns], bcat[:, ns:]
    bcat16 = bcat.astype(BF16)
    cre, cim = cblk_ref[0, :, :ns], cblk_ref[0, :, ns:]
    for d in range(t + 1):
        pr, pi = pw_ref[0, d:d + 1, :ns], pw_ref[0, d:d + 1, ns:]
        mo = jnp.concatenate([cre * pr - cim * pi, -(cre * pi + cim * pr)], axis=1).astype(BF16).T
        if d >= 1:
            w2_scr[:, (d - 1) * LANES:d * LANES] = mo
        if d < t:
            kd = jnp.dot(bcat16, mo, preferred_element_type=F32).astype(BF16)
            for ip in range(t - d):
                w1_scr[ip * LANES:(ip + 1) * LANES, (ip + d) * LANES:(ip + d + 1) * LANES] = kd
            ip = t - 1 - d
            w1_scr[ip * LANES:(ip + 1) * LANES, yw:yw + ns] = (bre * pr - bim * pi).astype(BF16)
            w1_scr[ip * LANES:(ip + 1) * LANES, yw + ns:] = (bre * pi + bim * pr).astype(BF16)
    for ip in range(1, t):
        w1_scr[ip * LANES:(ip + 1) * LANES, 0:ip * LANES] = jnp.zeros((LANES, ip * LANES), BF16)


def _s5_kernel(u_ref, d_ref, bblk_ref, cblk_ref, pw_ref, z_ref, w1_scr, w2_scr, lhs_scr, s_scr, yc_scr, y_scr,
               *, n_seq):
    t = SSM_CHUNK
    cpg = S5_GROUP // t
    rows = lhs_scr.shape[0]
    ngroups = rows // (n_seq * cpg)
    nsl = s_scr.shape[0] // 2
    yw = t * LANES

    _s5_build_tables(bblk_ref, cblk_ref, pw_ref, w1_scr, w2_scr)

    for i in range(t):
        lhs_scr[:, i * LANES:(i + 1) * LANES] = u_ref[pl.ds(i, rows, stride=t), :].astype(BF16)
    yc_scr[...] = jnp.dot(lhs_scr[...], w1_scr[:, :yw], preferred_element_type=F32)
    s_in_all = jnp.dot(lhs_scr[...], w1_scr[:, yw:], preferred_element_type=F32)
    for q in range(2 * nsl):
        s_scr[q] = s_in_all[:, q * LANES:(q + 1) * LANES]

    a_re = [pw_ref[0, t:t + 1, q * LANES:(q + 1) * LANES] for q in range(nsl)]
    a_im = [pw_ref[0, t:t + 1, (nsl + q) * LANES:(nsl + q + 1) * LANES] for q in range(nsl)]

    def scan_group(g, carry):
        e = list(carry)
        for c in range(cpg):
            sel = pl.ds(g * (n_seq * cpg) + c, n_seq, stride=cpg)
            for q in range(nsl):
                e_re, e_im = e[q], e[nsl + q]
                in_re, in_im = s_scr[q, sel, :], s_scr[nsl + q, sel, :]
                s_scr[q, sel, :] = e_re
                s_scr[nsl + q, sel, :] = e_im
                e[q] = a_re[q] * e_re - a_im[q] * e_im + in_re
                e[nsl + q] = a_re[q] * e_im + a_im[q] * e_re + in_im
        return tuple(e)

    lax.fori_loop(0, ngroups, scan_group, (jnp.zeros((n_seq, LANES), F32),) * (2 * nsl))

    s_prev = jnp.concatenate([s_scr[q].astype(BF16) for q in range(2 * nsl)], axis=1)
    y = yc_scr[...] + jnp.dot(s_prev, w2_scr[...], preferred_element_type=F32)
    for i in range(t):
        y_scr[pl.ds(i, rows, stride=t), :] = y[:, i * LANES:(i + 1) * LANES]

    yy = y_scr[...] + d_ref[...] * u_ref[...]
    z_ref[...] = jax.nn.gelu(yy).astype(z_ref.dtype)


def _s5_mixer(u4d, d_row, bblk, cblk, pw):
    ngr, bsz, grp, d = u4d.shape
    m = ngr * bsz * grp
    nj = d // LANES
    rows = m // SSM_CHUNK
    ns2 = bblk.shape[2]
    yw = SSM_CHUNK * LANES
    z = pl.pallas_call(
        functools.partial(_s5_kernel, n_seq=bsz),
        grid=(nj,),
        in_specs=[
            pl.BlockSpec((m, LANES), lambda j: (0, j)),
            pl.BlockSpec((1, LANES), lambda j: (0, j)),
            pl.BlockSpec((1, LANES, ns2), lambda j: (j, 0, 0)),
            pl.BlockSpec((1, LANES, ns2), lambda j: (j, 0, 0)),
            pl.BlockSpec((1, 16, ns2), lambda j: (j, 0, 0)),
        ],
        out_specs=pl.BlockSpec((m, LANES), lambda j: (0, j)),
        out_shape=jax.ShapeDtypeStruct((m, d), BF16),
        scratch_shapes=[
            pltpu.VMEM((yw, yw + ns2), BF16),
            pltpu.VMEM((ns2, yw), BF16),
            pltpu.VMEM((rows, yw), BF16),
            pltpu.VMEM((ns2 // LANES, rows, LANES), F32),
            pltpu.VMEM((rows, yw), F32),
            pltpu.VMEM((m, LANES), F32),
        ],
        compiler_params=_cparams(("parallel",)),
        name="s5_mixer",
    )(u4d.reshape(m, d), d_row, bblk, cblk, pw)
    return z.reshape(ngr, bsz, grp, d)


def _glu_kernel(z_ref, w_ref, b_ref, zt_ref, res_ref, o_ref):
    tm, tn = o_ref.shape
    z = z_ref[...].reshape(tm, z_ref.shape[-1])
    a = jnp.dot(z, w_ref[...].astype(BF16), preferred_element_type=F32) + b_ref[...]
    o_ref[...] = res_ref[...] + zt_ref[...].reshape(tm, tn).astype(F32) * jax.nn.sigmoid(a)


def _glu_residual(z4d, w, layer, b_row, res, tm=1024, tn=512):
    ngr, bsz, grp, k = z4d.shape
    m = ngr * bsz * grp
    n = w.shape[2]
    tiles = ngr * grp // tm
    return pl.pallas_call(
        _glu_kernel,
        grid=(bsz, tiles, n // tn),
        in_specs=[
            pl.BlockSpec((tm // grp, None, grp, k), lambda b, i, j: (i, b, 0, 0)),
            pl.BlockSpec((None, k, tn), lambda b, i, j: (layer, 0, j)),
            pl.BlockSpec((1, tn), lambda b, i, j: (0, j)),
            pl.BlockSpec((tm // grp, None, grp, tn), lambda b, i, j: (i, b, 0, j)),
            pl.BlockSpec((tm, tn), lambda b, i, j: (b * tiles + i, j)),
        ],
        out_specs=pl.BlockSpec((tm, tn), lambda b, i, j: (b * tiles + i, j)),
        out_shape=jax.ShapeDtypeStruct((m, n), F32),
        compiler_params=_cparams(("parallel", "parallel", "parallel")),
        name="glu_residual",
    )(z4d, w, b_row, z4d, res)


def _proj_res_kernel(x_ref, w_ref, res_ref, o_ref):
    o_ref[...] = res_ref[...] + jnp.dot(x_ref[...], w_ref[...].astype(BF16), preferred_element_type=F32)


def _proj_residual(x, w, layer, res, tm=1024, tn=512):
    m, k = x.shape
    n = w.shape[2]
    return pl.pallas_call(
        _proj_res_kernel,
        grid=(m // tm, n // tn),
        in_specs=[
            pl.BlockSpec((tm, k), lambda i, j: (i, 0)),
            pl.BlockSpec((None, k, tn), lambda i, j: (layer, 0, j)),
            pl.BlockSpec((tm, tn), lambda i, j: (i, j)),
        ],
        out_specs=pl.BlockSpec((tm, tn), lambda i, j: (i, j)),
        out_shape=jax.ShapeDtypeStruct((m, n), F32),
        compiler_params=_cparams(("parallel", "parallel")),
        name="proj_residual",
    )(x, w, res)


def _qkv_kernel(x_ref, gq_ref, gkv_ref, wq_ref, wkv_ref, o_ref, uq_scr, ukv_scr, *, n_q_tiles, q_scale):
    j = pl.program_id(1)

    @pl.when(j == 0)
    def _():
        x = x_ref[...]
        xn = x * _rms_scale(x)
        uq_scr[...] = (xn * gq_ref[...]).astype(BF16)
        ukv_scr[...] = (xn * gkv_ref[...]).astype(BF16)

    @pl.when(j < n_q_tiles)
    def _():
        q = jnp.dot(uq_scr[...], wq_ref[...].astype(BF16), preferred_element_type=F32)
        o_ref[...] = (q * q_scale).astype(o_ref.dtype)

    @pl.when(j >= n_q_tiles)
    def _():
        o_ref[...] = jnp.dot(ukv_scr[...], wkv_ref[...].astype(BF16), preferred_element_type=F32).astype(o_ref.dtype)


def _qkv_proj(h2d, gq_row, gkv_row, w_q, layer, w_kv, q_scale, tm=1024, tn=512):
    m, k = h2d.shape
    nq = w_q.shape[2] // tn
    nkv = w_kv.shape[1] // tn
    return pl.pallas_call(
        functools.partial(_qkv_kernel, n_q_tiles=nq, q_scale=q_scale),
        grid=(m // tm, nq + nkv),
        in_specs=[
            pl.BlockSpec((tm, k), lambda i, j: (i, 0), pipeline_mode=pl.Buffered(1)),
            pl.BlockSpec((1, k), lambda i, j: (0, 0)),
            pl.BlockSpec((1, k), lambda i, j: (0, 0)),
            pl.BlockSpec((None, k, tn), lambda i, j: (layer, 0, jnp.minimum(j, nq - 1))),
            pl.BlockSpec((k, tn), lambda i, j: (0, jnp.maximum(j - nq, 0))),
        ],
        out_specs=pl.BlockSpec((tm, tn), lambda i, j: (i, j)),
        out_shape=jax.ShapeDtypeStruct((m, (nq + nkv) * tn), BF16),
        scratch_shapes=[pltpu.VMEM((tm, k), BF16), pltpu.VMEM((tm, k), BF16)],
        compiler_params=_cparams(("parallel", "arbitrary")),
        name="qkv_proj",
    )(h2d, gq_row, gkv_row, w_q, w_kv)


def _ffn_kernel(h_ref, g_ref, wg_ref, wu_ref, wd_ref, gfin_ref, o_ref, u_scr, *, final_norm):
    f = pl.program_id(1)

    @pl.when(f == 0)
    def _():
        x = h_ref[...]
        u_scr[...] = (x * _rms_scale(x) * g_ref[...]).astype(BF16)
        o_ref[...] = x

    u = u_scr[...]
    gate = jnp.dot(u, wg_ref[...].astype(BF16), preferred_element_type=F32)
    up = jnp.dot(u, wu_ref[...].astype(BF16), preferred_element_type=F32)
    act = (jax.nn.silu(gate) * up).astype(BF16)
    o_ref[...] += jnp.dot(act, wd_ref[...].astype(BF16), preferred_element_type=F32)

    if final_norm:
        @pl.when(f == pl.num_programs(1) - 1)
        def _():
            y = o_ref[...]
            o_ref[...] = y * _rms_scale(y) * gfin_ref[...]


def _ffn_residual(h2d, g_row, w_gate, w_up, w_down, layer, gfin_row, final_norm, tm=1024, tf=256):
    m, d = h2d.shape
    ff = w_gate.shape[2]
    return pl.pallas_call(
        functools.partial(_ffn_kernel, final_norm=final_norm),
        grid=(m // tm, ff // tf),
        in_specs=[
            pl.BlockSpec((tm, d), lambda i, f: (i, 0), pipeline_mode=pl.Buffered(1)),
            pl.BlockSpec((1, d), lambda i, f: (0, 0)),
            pl.BlockSpec((None, d, tf), lambda i, f: (layer, 0, f)),
            pl.BlockSpec((None, d, tf), lambda i, f: (layer, 0, f)),
            pl.BlockSpec((None, tf, d), lambda i, f: (layer, f, 0)),
            pl.BlockSpec((1, d), lambda i, f: (0, 0)),
        ],
        out_specs=pl.BlockSpec((tm, d), lambda i, f: (i, 0)),
        out_shape=jax.ShapeDtypeStruct((m, d), F32),
        scratch_shapes=[pltpu.VMEM((tm, d), BF16)],
        compiler_params=_cparams(("parallel", "arbitrary")),
        name="ffn_residual",
    )(h2d, g_row, w_gate, w_up, w_down, gfin_row)


def _moba_kernel(slopes_ref, q_ref, k_ref, v_ref, o_ref, kmean_scr, alibi_scr, rt_scr, s_scr, p_scr, linv_scr):
    kb = KEY_BLOCK
    rc_rows = MOBA_ROW_CHUNK
    nb = k_ref.shape[1] // kb
    neg_inf = -jnp.inf
    nt = (((1,), (1,)), ((), ()))
    slope = slopes_ref[pl.program_id(1)]

    kmean_scr[...] = jnp.zeros_like(kmean_scr)
    for n in range(nb):
        kmean_scr[n:n + 1, :] = jnp.mean(k_ref[0, n * kb:(n + 1) * kb, :].astype(F32), axis=0, keepdims=True)
    row = lax.broadcasted_iota(jnp.int32, (kb, kb), 0)
    colk = lax.broadcasted_iota(jnp.int32, (kb, kb), 1)
    alibi_scr[...] = -slope * (row - colk).astype(F32)

    g0 = (TOP_K_BLOCKS + 1) * kb
    if nb * kb > g0:
        ng = nb * kb - g0
        gate = lax.dot_general(q_ref[0, g0:, :].astype(F32), kmean_scr[...], nt,
                               precision=lax.Precision.HIGHEST, preferred_element_type=F32)
        col = lax.broadcasted_iota(jnp.int32, (ng, LANES), 1)
        colf = col.astype(F32)
        own_v = lax.div(lax.broadcasted_iota(jnp.int32, (ng, LANES), 0), jnp.int32(kb)) + (TOP_K_BLOCKS + 1)
        offset = -slope * ((own_v - col) * kb).astype(F32)
        gate = jnp.where(col < own_v, gate, neg_inf)
        rowterm = jnp.full(gate.shape, neg_inf, F32)
        for _ in range(TOP_K_BLOCKS):
            mx = jnp.max(gate, axis=-1, keepdims=True)
            idx = jnp.min(jnp.where(gate == mx, colf, float(LANES)), axis=-1, keepdims=True)
            hit = colf == idx
            rowterm = jnp.where(hit, offset, rowterm)
            gate = jnp.where(hit, neg_inf, gate)
        rt_scr[g0:, :] = rowterm

    for own in range(nb):
        slot = own % 2
        width = (own + 1) * kb
        q0 = own * kb
        q = q_ref[0, q0:q0 + kb, :]
        s_scr[slot, :, 0:width] = lax.dot_general(q, k_ref[0, 0:width, :], nt, preferred_element_type=F32)
        gated = own > TOP_K_BLOCKS

        for rc in range(kb // rc_rows):
            r0, r1 = rc * rc_rows, (rc + 1) * rc_rows
            ab = alibi_scr[r0:r1, :]
            parts = []
            for n in range(own):
                rt = rt_scr[q0 + r0:q0 + r1, n:n + 1] if gated else -slope * float((own - n) * kb)
                parts.append(s_scr[slot, r0:r1, n * kb:(n + 1) * kb] + ab + rt)
            parts.append(jnp.where(ab <= 0.0, s_scr[slot, r0:r1, own * kb:width] + ab, neg_inf))
            m = jnp.max(functools.reduce(jnp.maximum, parts), axis=-1, keepdims=True)
            probs = [jnp.exp2(x - m) for x in parts]
            l = jnp.sum(functools.reduce(jnp.add, probs), axis=-1, keepdims=True)
            for n in range(own + 1):
                p_scr[slot, r0:r1, n * kb:(n + 1) * kb] = probs[n].astype(BF16)
            linv_scr[slot, r0:r1, :] = jnp.broadcast_to(1.0 / l, (rc_rows, LANES))

        acc = jnp.dot(p_scr[slot, :, 0:width], v_ref[0, 0:width, :], preferred_element_type=F32)
        o_ref[0, own * kb:(own + 1) * kb, :] = (acc * linv_scr[slot]).astype(o_ref.dtype)


def _moba_attention(qkv3d, d_model):
    b, l, _ = qkv3d.shape
    n_heads = d_model // HEAD_DIM
    kb = KEY_BLOCK
    slopes = jnp.exp2(-8.0 * jnp.arange(1, n_heads + 1, dtype=F32) / n_heads) * LOG2_E
    return pl.pallas_call(
        _moba_kernel,
        grid=(b, n_heads),
        in_specs=[
            pl.BlockSpec(memory_space=pltpu.SMEM),
            pl.BlockSpec((1, l, HEAD_DIM), lambda i, h: (i, 0, h)),
            pl.BlockSpec((1, l, HEAD_DIM), lambda i, h: (i, 0, n_heads + h)),
            pl.BlockSpec((1, l, HEAD_DIM), lambda i, h: (i, 0, 2 * n_heads + h)),
        ],
        out_specs=pl.BlockSpec((1, l, HEAD_DIM), lambda i, h: (i, 0, h)),
        out_shape=jax.ShapeDtypeStruct((b, l, d_model), BF16),
        scratch_shapes=[
            pltpu.VMEM((LANES, HEAD_DIM), F32),
            pltpu.VMEM((kb, kb), F32),
            pltpu.VMEM((l, LANES), F32),
            pltpu.VMEM((2, kb, l), F32),
            pltpu.VMEM((2, kb, l), BF16),
            pltpu.VMEM((2, kb, LANES), F32),
        ],
        compiler_params=_cparams(("parallel", "parallel")),
        name="moba_attention",
    )(slopes, qkv3d, qkv3d, qkv3d)


def kernel(x, g_mix, lambda_re, lambda_im, log_dt, b_re, b_im, c_re, c_im, d_skip, w_glu, b_glu, g_kv, w_kv,
           w_q, w_o, g_ffn, w_gate, w_up, w_down, g_final):
    bsz, l, d = x.shape
    depth = g_mix.shape[0]
    n_a = lambda_re.shape[0]
    m = bsz * l
    assert l % KEY_BLOCK == 0 and d % HEAD_DIM == 0 and l // KEY_BLOCK <= LANES
    row = lambda v: v.reshape(1, -1).astype(F32)

    h = x.astype(F32).reshape(m, d)
    qkv = None
    for i in range(depth):
        if i < n_a:
            tables = _s5_tables(lambda_re[i], lambda_im[i], log_dt[i], b_re[i], b_im[i], c_re[i], c_im[i])
            u = _rmsnorm_interleaved(h, row(g_mix[i]), bsz)
            z = _s5_mixer(u, row(d_skip[i]), *tables)
            h = _glu_residual(z, w_glu, i, row(b_glu[i]), h)
        else:
            j = i - n_a
            qkv_i = _qkv_proj(h, row(g_mix[i]), row(g_kv), w_q, j, w_kv, HEAD_DIM ** -0.5 * LOG2_E)
            if qkv is None:
                qkv = qkv_i
            else:
                qkv = jnp.concatenate([qkv_i[:, :d], qkv[:, d:]], axis=1)
            attn = _moba_attention(qkv.reshape(bsz, l, 3 * d), d)
            h = _proj_residual(attn.reshape(m, d), w_o, j, h)
        h = _ffn_residual(h, row(g_ffn[i]), w_gate, w_up, w_down, i, row(g_final), final_norm=(i == depth - 1))
    return h.reshape(bsz, l, d).astype(x.dtype)
```

```python
import functools
import math

import jax
import jax.numpy as jnp
from jax import lax
from jax.experimental import pallas as pl
from jax.experimental.pallas import tpu as pltpu

F32 = jnp.float32
BF16 = jnp.bfloat16

EPS = 1e-6
LANES = 128
HEAD_DIM = 128
KEY_BLOCK = 256
TOP_K_BLOCKS = 3
MOBA_ROW_CHUNK = 16
LOG2_E = math.log2(math.e)
SSM_CHUNK = 8
S5_GROUP = 16
VMEM_LIMIT = 56 * 1024 * 1024


def _cparams(sem):
    return pltpu.CompilerParams(dimension_semantics=sem, vmem_limit_bytes=VMEM_LIMIT)


def _rms_scale(x):
    return lax.rsqrt(jnp.mean(x * x, axis=-1, keepdims=True) + EPS)


def _rmsnorm_kernel(x_ref, g_ref, o_ref):
    x = x_ref[...]
    o_ref[...] = (x * _rms_scale(x) * g_ref[...]).reshape(o_ref.shape)


def _rmsnorm_interleaved(x2d, g_row, bsz, tm=512):
    m, d = x2d.shape
    tiles = m // bsz // tm
    return pl.pallas_call(
        _rmsnorm_kernel,
        grid=(bsz, tiles),
        in_specs=[pl.BlockSpec((tm, d), lambda b, i: (b * tiles + i, 0)), pl.BlockSpec((1, d), lambda b, i: (0, 0))],
        out_specs=pl.BlockSpec((tm // S5_GROUP, None, S5_GROUP, d), lambda b, i: (i, b, 0, 0)),
        out_shape=jax.ShapeDtypeStruct((m // bsz // S5_GROUP, bsz, S5_GROUP, d), F32),
        compiler_params=_cparams(("parallel", "parallel")),
        name="rmsnorm",
    )(x2d, g_row)


def _s5_tables(lam_re, lam_im, log_dt, b_re, b_im, c_re, c_im):
    g, p, c = b_re.shape
    t = SSM_CHUNK
    gl = LANES // c
    nj = g // gl
    ns = gl * p
    lr, li = lam_re.astype(F32), lam_im.astype(F32)
    dt = jnp.exp(log_dt.astype(F32))[:, None]
    ar, ai = lr * dt, li * dt
    mag = jnp.exp(ar)
    nr, ni = mag * jnp.cos(ai) - 1.0, mag * jnp.sin(ai)
    den = lr * lr + li * li
    fr, fi = (nr * lr + ni * li) / den, (ni * lr - nr * li) / den
    br, bi = b_re.astype(F32), b_im.astype(F32)
    bbr = fr[..., None] * br - fi[..., None] * bi
    bbi = fr[..., None] * bi + fi[..., None] * br
    dd = jnp.arange(t + 1, dtype=F32)[:, None, None]
    pmag = jnp.exp(ar[None] * dd)
    pr, pi = pmag * jnp.cos(ai[None] * dd), pmag * jnp.sin(ai[None] * dd)

    def block_diag(x):
        x = jnp.tile(x.reshape(nj, gl * c, p), (1, 1, gl))
        rg = lax.broadcasted_iota(jnp.int32, (gl * c, ns), 0) // c
        cg = lax.broadcasted_iota(jnp.int32, (gl * c, ns), 1) // p
        return jnp.where(rg == cg, x, 0.0)

    bblk = jnp.concatenate([block_diag(jnp.swapaxes(bbr, 1, 2)), block_diag(jnp.swapaxes(bbi, 1, 2))], axis=-1)
    cblk = jnp.concatenate([block_diag(c_re.astype(F32)), block_diag(c_im.astype(F32))], axis=-1)
    pw = jnp.concatenate([pr.reshape(t + 1, nj, ns), pi.reshape(t + 1, nj, ns)], axis=-1)
    pw = jnp.pad(jnp.swapaxes(pw, 0, 1), ((0, 0), (0, 16 - (t + 1)), (0, 0)))
    return bblk, cblk, pw


def _s5_build_tables(bblk_ref, cblk_ref, pw_ref, w1_scr, w2_scr):
    t = SSM_CHUNK
    ns = bblk_ref.shape[2] // 2
    yw = t * LANES
    bcat = bblk_ref[0]
    bre, bim = bcat[:, :ns], bcat[:, ns:]
    bcat16 = bcat.astype(BF16)
    cre, cim = cblk_ref[0, :, :ns], cblk_ref[0, :, ns:]
    for d in range(t + 1):
        pr, pi = pw_ref[0, d:d + 1, :ns], pw_ref[0, d:d + 1, ns:]
        mo = jnp.concatenate([cre * pr - cim * pi, -(cre * pi + cim * pr)], axis=1).astype(BF16).T
        if d >= 1:
            w2_scr[:, (d - 1) * LANES:d * LANES] = mo
        if d < t:
            kd = jnp.dot(bcat16, mo, preferred_element_type=F32).astype(BF16)
            for ip in range(t - d):
                w1_scr[ip * LANES:(ip + 1) * LANES, (ip + d) * LANES:(ip + d + 1) * LANES] = kd
            ip = t - 1 - d
            w1_scr[ip * LANES:(ip + 1) * LANES, yw:yw + ns] = (bre * pr - bim * pi).astype(BF16)
            w1_scr[ip * LANES:(ip + 1) * LANES, yw + ns:] = (bre * pi + bim * pr).astype(BF16)
    for ip in range(1, t):
        w1_scr[ip * LANES:(ip + 1) * LANES, 0:ip * LANES] = jnp.zeros((LANES, ip * LANES), BF16)


def _s5_kernel(u_ref, d_ref, bblk_ref, cblk_ref, pw_ref, z_ref, w1_scr, w2_scr, lhs_scr, s_scr, yc_scr, y_scr,
               *, n_seq):
    t = SSM_CHUNK
    cpg = S5_GROUP // t
    rows = lhs_scr.shape[0]
    ngroups = rows // (n_seq * cpg)
    nsl = s_scr.shape[0] // 2
    yw = t * LANES

    _s5_build_tables(bblk_ref, cblk_ref, pw_ref, w1_scr, w2_scr)

    for i in range(t):
        lhs_scr[:, i * LANES:(i + 1) * LANES] = u_ref[pl.ds(i, rows, stride=t), :].astype(BF16)
    yc_scr[...] = jnp.dot(lhs_scr[...], w1_scr[:, :yw], preferred_element_type=F32)
    s_in_all = jnp.dot(lhs_scr[...], w1_scr[:, yw:], preferred_element_type=F32)
    for q in range(2 * nsl):
        s_scr[q] = s_in_all[:, q * LANES:(q + 1) * LANES]

    a_re = [pw_ref[0, t:t + 1, q * LANES:(q + 1) * LANES] for q in range(nsl)]
    a_im = [pw_ref[0, t:t + 1, (nsl + q) * LANES:(nsl + q + 1) * LANES] for q in range(nsl)]

    def scan_group(g, carry):
        e = list(carry)
        for c in range(cpg):
            sel = pl.ds(g * (n_seq * cpg) + c, n_seq, stride=cpg)
            for q in range(nsl):
                e_re, e_im = e[q], e[nsl + q]
                in_re, in_im = s_scr[q, sel, :], s_scr[nsl + q, sel, :]
                s_scr[q, sel, :] = e_re
                s_scr[nsl + q, sel, :] = e_im
                e[q] = a_re[q] * e_re - a_im[q] * e_im + in_re
                e[nsl + q] = a_re[q] * e_im + a_im[q] * e_re + in_im
        return tuple(e)

    lax.fori_loop(0, ngroups, scan_group, (jnp.zeros((n_seq, LANES), F32),) * (2 * nsl))

    s_prev = jnp.concatenate([s_scr[q].astype(BF16) for q in range(2 * nsl)], axis=1)
    y = yc_scr[...] + jnp.dot(s_prev, w2_scr[...], preferred_element_type=F32)
    for i in range(t):
        y_scr[pl.ds(i, rows, stride=t), :] = y[:, i * LANES:(i + 1) * LANES]

    yy = y_scr[...] + d_ref[...] * u_ref[...]
    z_ref[...] = jax.nn.gelu(yy).astype(z_ref.dtype)


def _s5_mixer(u4d, d_row, bblk, cblk, pw):
    ngr, bsz, grp, d = u4d.shape
    m = ngr * bsz * grp
    nj = d // LANES
    rows = m // SSM_CHUNK
    ns2 = bblk.shape[2]
    yw = SSM_CHUNK * LANES
    z = pl.pallas_call(
        functools.partial(_s5_kernel, n_seq=bsz),
        grid=(nj,),
        in_specs=[
            pl.BlockSpec((m, LANES), lambda j: (0, j)),
            pl.BlockSpec((1, LANES), lambda j: (0, j)),
            pl.BlockSpec((1, LANES, ns2), lambda j: (j, 0, 0)),
            pl.BlockSpec((1, LANES, ns2), lambda j: (j, 0, 0)),
            pl.BlockSpec((1, 16, ns2), lambda j: (j, 0, 0)),
        ],
        out_specs=pl.BlockSpec((m, LANES), lambda j: (0, j)),
        out_shape=jax.ShapeDtypeStruct((m, d), BF16),
        scratch_shapes=[
            pltpu.VMEM((yw, yw + ns2), BF16),
            pltpu.VMEM((ns2, yw), BF16),
            pltpu.VMEM((rows, yw), BF16),
            pltpu.VMEM((ns2 // LANES, rows, LANES), F32),
            pltpu.VMEM((rows, yw), F32),
            pltpu.VMEM((m, LANES), F32),
        ],
        compiler_params=_cparams(("parallel",)),
        name="s5_mixer",
    )(u4d.reshape(m, d), d_row, bblk, cblk, pw)
    return z.reshape(ngr, bsz, grp, d)


def _glu_kernel(z_ref, w_ref, b_ref, res_ref, o_ref, w16_scr):
    tm, n = o_ref.shape

    @pl.when((pl.program_id(0) == 0) & (pl.program_id(1) == 0))
    def _():
        w16_scr[...] = w_ref[...].astype(BF16)

    z = z_ref[...].reshape(tm, n)
    a = jnp.dot(z, w16_scr[...], preferred_element_type=F32) + b_ref[...]
    o_ref[...] = res_ref[...] + z.astype(F32) * jax.nn.sigmoid(a)


def _glu_residual(z4d, w, layer, b_row, res, tm=512):
    ngr, bsz, grp, k = z4d.shape
    m = ngr * bsz * grp
    n = w.shape[2]
    assert k == n and (ngr * grp) % tm == 0 and tm % grp == 0
    tiles = ngr * grp // tm
    return pl.pallas_call(
        _glu_kernel,
        grid=(bsz, tiles),
        in_specs=[
            pl.BlockSpec((tm // grp, None, grp, k), lambda b, i: (i, b, 0, 0)),
            pl.BlockSpec((None, k, n), lambda b, i: (layer, 0, 0), pipeline_mode=pl.Buffered(1)),
            pl.BlockSpec((1, n), lambda b, i: (0, 0)),
            pl.BlockSpec((tm, n), lambda b, i: (b * tiles + i, 0)),
        ],
        out_specs=pl.BlockSpec((tm, n), lambda b, i: (b * tiles + i, 0)),
        out_shape=jax.ShapeDtypeStruct((m, n), F32),
        scratch_shapes=[pltpu.VMEM((k, n), BF16)],
        compiler_params=_cparams(("arbitrary", "arbitrary")),
        name="glu_residual",
    )(z4d, w, b_row, res)


def _proj_res_kernel(x_ref, w_ref, res_ref, o_ref, w16_scr):
    @pl.when(pl.program_id(0) == 0)
    def _():
        w16_scr[...] = w_ref[...].astype(BF16)

    o_ref[...] = res_ref[...] + jnp.dot(x_ref[...], w16_scr[...], preferred_element_type=F32)


def _proj_residual(x, w, layer, res, tm=512):
    m, k = x.shape
    n = w.shape[2]
    assert m % tm == 0
    return pl.pallas_call(
        _proj_res_kernel,
        grid=(m // tm,),
        in_specs=[
            pl.BlockSpec((tm, k), lambda i: (i, 0)),
            pl.BlockSpec((None, k, n), lambda i: (layer, 0, 0), pipeline_mode=pl.Buffered(1)),
            pl.BlockSpec((tm, n), lambda i: (i, 0)),
        ],
        out_specs=pl.BlockSpec((tm, n), lambda i: (i, 0)),
        out_shape=jax.ShapeDtypeStruct((m, n), F32),
        scratch_shapes=[pltpu.VMEM((k, n), BF16)],
        compiler_params=_cparams(("arbitrary",)),
        name="proj_residual",
    )(x, w, res)


def _qkv_kernel(x_ref, gq_ref, gkv_ref, wq_ref, wkv_ref, o_ref, uq_scr, ukv_scr, *, n_q_tiles, q_scale):
    j = pl.program_id(1)

    @pl.when(j == 0)
    def _():
        x = x_ref[...]
        xn = x * _rms_scale(x)
        uq_scr[...] = (xn * gq_ref[...]).astype(BF16)
        ukv_scr[...] = (xn * gkv_ref[...]).astype(BF16)

    @pl.when(j < n_q_tiles)
    def _():
        q = jnp.dot(uq_scr[...], wq_ref[...].astype(BF16), preferred_element_type=F32)
        o_ref[...] = (q * q_scale).astype(o_ref.dtype)

    @pl.when(j >= n_q_tiles)
    def _():
        o_ref[...] = jnp.dot(ukv_scr[...], wkv_ref[...].astype(BF16), preferred_element_type=F32).astype(o_ref.dtype)


def _qkv_proj(h2d, gq_row, gkv_row, w_q, layer, w_kv, q_scale, tm=1024, tn=512):
    m, k = h2d.shape
    nq = w_q.shape[2] // tn
    nkv = w_kv.shape[1] // tn
    return pl.pallas_call(
        functools.partial(_qkv_kernel, n_q_tiles=nq, q_scale=q_scale),
        grid=(m // tm, nq + nkv),
        in_specs=[
            pl.BlockSpec((tm, k), lambda i, j: (i, 0), pipeline_mode=pl.Buffered(1)),
            pl.BlockSpec((1, k), lambda i, j: (0, 0)),
            pl.BlockSpec((1, k), lambda i, j: (0, 0)),
            pl.BlockSpec((None, k, tn), lambda i, j: (layer, 0, jnp.minimum(j, nq - 1))),
            pl.BlockSpec((k, tn), lambda i, j: (0, jnp.maximum(j - nq, 0))),
        ],
        out_specs=pl.BlockSpec((tm, tn), lambda i, j: (i, j)),
        out_shape=jax.ShapeDtypeStruct((m, (nq + nkv) * tn), BF16),
        scratch_shapes=[pltpu.VMEM((tm, k), BF16), pltpu.VMEM((tm, k), BF16)],
        compiler_params=_cparams(("parallel", "arbitrary")),
        name="qkv_proj",
    )(h2d, gq_row, gkv_row, w_q, w_kv)


def _ffn_kernel(h_ref, g_ref, wg_ref, wu_ref, wd_ref, gfin_ref, o_ref, u_scr, *, final_norm):
    f = pl.program_id(1)

    @pl.when(f == 0)
    def _():
        x = h_ref[...]
        u_scr[...] = (x * _rms_scale(x) * g_ref[...]).astype(BF16)
        o_ref[...] = x

    u = u_scr[...]
    gate = jnp.dot(u, wg_ref[...].astype(BF16), preferred_element_type=F32)
    up = jnp.dot(u, wu_ref[...].astype(BF16), preferred_element_type=F32)
    act = (jax.nn.silu(gate) * up).astype(BF16)
    o_ref[...] += jnp.dot(act, wd_ref[...].astype(BF16), preferred_element_type=F32)

    if final_norm:
        @pl.when(f == pl.num_programs(1) - 1)
        def _():
            y = o_ref[...]
            o_ref[...] = y * _rms_scale(y) * gfin_ref[...]


def _ffn_residual(h2d, g_row, w_gate, w_up, w_down, layer, gfin_row, final_norm, tm=1024, tf=256):
    m, d = h2d.shape
    ff = w_gate.shape[2]
    return pl.pallas_call(
        functools.partial(_ffn_kernel, final_norm=final_norm),
        grid=(m // tm, ff // tf),
        in_specs=[
            pl.BlockSpec((tm, d), lambda i, f: (i, 0)),
            pl.BlockSpec((1, d), lambda i, f: (0, 0)),
            pl.BlockSpec((None, d, tf), lambda i, f: (layer, 0, f)),
            pl.BlockSpec((None, d, tf), lambda i, f: (layer, 0, f)),
            pl.BlockSpec((None, tf, d), lambda i, f: (layer, f, 0)),
            pl.BlockSpec((1, d), lambda i, f: (0, 0)),
        ],
        out_specs=pl.BlockSpec((tm, d), lambda i, f: (i, 0)),
        out_shape=jax.ShapeDtypeStruct((m, d), F32),
        scratch_shapes=[pltpu.VMEM((tm, d), BF16)],
        compiler_params=_cparams(("parallel", "arbitrary")),
        name="ffn_residual",
    )(h2d, g_row, w_gate, w_up, w_down, gfin_row)


def _moba_kernel(slopes_ref, q_ref, k_ref, v_ref, o_ref, kmean_scr, alibi_scr, rt_scr, s_scr, p_scr, linv_scr):
    kb = KEY_BLOCK
    rc_rows = MOBA_ROW_CHUNK
    nb = k_ref.shape[1] // kb
    neg_inf = -jnp.inf
    nt = (((1,), (1,)), ((), ()))
    slope = slopes_ref[pl.program_id(1)]

    kmean_scr[...] = jnp.zeros_like(kmean_scr)
    for n in range(nb):
        kmean_scr[n:n + 1, :] = jnp.mean(k_ref[0, n * kb:(n + 1) * kb, :].astype(F32), axis=0, keepdims=True)
    row = lax.broadcasted_iota(jnp.int32, (kb, kb), 0)
    colk = lax.broadcasted_iota(jnp.int32, (kb, kb), 1)
    alibi_scr[...] = -slope * (row - colk).astype(F32)

    g0 = (TOP_K_BLOCKS + 1) * kb
    if nb * kb > g0:
        ng = nb * kb - g0
        gate = lax.dot_general(q_ref[0, g0:, :].astype(F32), kmean_scr[...], nt,
                               precision=lax.Precision.HIGHEST, preferred_element_type=F32)
        col = lax.broadcasted_iota(jnp.int32, (ng, LANES), 1)
        colf = col.astype(F32)
        own_v = lax.div(lax.broadcasted_iota(jnp.int32, (ng, LANES), 0), jnp.int32(kb)) + (TOP_K_BLOCKS + 1)
        offset = -slope * ((own_v - col) * kb).astype(F32)
        gate = jnp.where(col < own_v, gate, neg_inf)
        rowterm = jnp.full(gate.shape, neg_inf, F32)
        for _ in range(TOP_K_BLOCKS):
            mx = jnp.max(gate, axis=-1, keepdims=True)
            idx = jnp.min(jnp.where(gate == mx, colf, float(LANES)), axis=-1, keepdims=True)
            hit = colf == idx
            rowterm = jnp.where(hit, offset, rowterm)
            gate = jnp.where(hit, neg_inf, gate)
        rt_scr[g0:, :] = rowterm

    for own in range(nb):
        slot = own % 2
        width = (own + 1) * kb
        q0 = own * kb
        q = q_ref[0, q0:q0 + kb, :]
        s_scr[slot, :, 0:width] = lax.dot_general(q, k_ref[0, 0:width, :], nt, preferred_element_type=F32)
        gated = own > TOP_K_BLOCKS

        for rc in range(kb // rc_rows):
            r0, r1 = rc * rc_rows, (rc + 1) * rc_rows
            ab = alibi_scr[r0:r1, :]
            parts = []
            for n in range(own):
                rt = rt_scr[q0 + r0:q0 + r1, n:n + 1] if gated else -slope * float((own - n) * kb)
                parts.append(s_scr[slot, r0:r1, n * kb:(n + 1) * kb] + ab + rt)
            parts.append(jnp.where(ab <= 0.0, s_scr[slot, r0:r1, own * kb:width] + ab, neg_inf))
            m = jnp.max(functools.reduce(jnp.maximum, parts), axis=-1, keepdims=True)
            probs = [jnp.exp2(x - m) for x in parts]
            l = jnp.sum(functools.reduce(jnp.add, probs), axis=-1, keepdims=True)
            for n in range(own + 1):
                p_scr[slot, r0:r1, n * kb:(n + 1) * kb] = probs[n].astype(BF16)
            linv_scr[slot, r0:r1, :] = jnp.broadcast_to(1.0 / l, (rc_rows, LANES))

        acc = jnp.dot(p_scr[slot, :, 0:width], v_ref[0, 0:width, :], preferred_element_type=F32)
        o_ref[0, own * kb:(own + 1) * kb, :] = (acc * linv_scr[slot]).astype(o_ref.dtype)


def _moba_attention(qkv3d, d_model):
    b, l, _ = qkv3d.shape
    n_heads = d_model // HEAD_DIM
    kb = KEY_BLOCK
    slopes = jnp.exp2(-8.0 * jnp.arange(1, n_heads + 1, dtype=F32) / n_heads) * LOG2_E
    return pl.pallas_call(
        _moba_kernel,
        grid=(b, n_heads),
        in_specs=[
            pl.BlockSpec(memory_space=pltpu.SMEM),
            pl.BlockSpec((1, l, HEAD_DIM), lambda i, h: (i, 0, h)),
            pl.BlockSpec((1, l, HEAD_DIM), lambda i, h: (i, 0, n_heads + h)),
            pl.BlockSpec((1, l, HEAD_DIM), lambda i, h: (i, 0, 2 * n_heads + h)),
        ],
        out_specs=pl.BlockSpec((1, l, HEAD_DIM), lambda i, h: (i, 0, h)),
        out_shape=jax.ShapeDtypeStruct((b, l, d_model), BF16),
        scratch_shapes=[
            pltpu.VMEM((LANES, HEAD_DIM), F32),
            pltpu.VMEM((kb, kb), F32),
            pltpu.VMEM((l, LANES), F32),
            pltpu.VMEM((2, kb, l), F32),
            pltpu.VMEM((2, kb, l), BF16),
            pltpu.VMEM((2, kb, LANES), F32),
        ],
        compiler_params=_cparams(("parallel", "parallel")),
        name="moba_attention",
    )(slopes, qkv3d, qkv3d, qkv3d)


def kernel(x, g_mix, lambda_re, lambda_im, log_dt, b_re, b_im, c_re, c_im, d_skip, w_glu, b_glu, g_kv, w_kv,
           w_q, w_o, g_ffn, w_gate, w_up, w_down, g_final):
    bsz, l, d = x.shape
    depth = g_mix.shape[0]
    n_a = lambda_re.shape[0]
    m = bsz * l
    assert l % KEY_BLOCK == 0 and d % HEAD_DIM == 0 and l // KEY_BLOCK <= LANES
    row = lambda v: v.reshape(1, -1).astype(F32)

    h = x.astype(F32).reshape(m, d)
    qkv = None
    for i in range(depth):
        if i < n_a:
            tables = _s5_tables(lambda_re[i], lambda_im[i], log_dt[i], b_re[i], b_im[i], c_re[i], c_im[i])
            u = _rmsnorm_interleaved(h, row(g_mix[i]), bsz)
            z = _s5_mixer(u, row(d_skip[i]), *tables)
            h = _glu_residual(z, w_glu, i, row(b_glu[i]), h)
        else:
            j = i - n_a
            qkv_i = _qkv_proj(h, row(g_mix[i]), row(g_kv), w_q, j, w_kv, HEAD_DIM ** -0.5 * LOG2_E)
            if qkv is None:
                qkv = qkv_i
            else:
                qkv = jnp.concatenate([qkv_i[:, :d], qkv[:, d:]], axis=1)
            attn = _moba_attention(qkv.reshape(bsz, l, 3 * d), d)
            h = _proj_residual(attn.reshape(m, d), w_o, j, h)
        h = _ffn_residual(h, row(g_ffn[i]), w_gate, w_up, w_down, i, row(g_final), final_norm=(i == depth - 1))
    return h.reshape(bsz, l, d).astype(x.dtype)
```

```python
import functools
import math

import jax
import jax.numpy as jnp
from jax import lax
from jax.experimental import pallas as pl
from jax.experimental.pallas import tpu as pltpu

F32 = jnp.float32
BF16 = jnp.bfloat16

EPS = 1e-6
LANES = 128
HEAD_DIM = 128
KEY_BLOCK = 256
TOP_K_BLOCKS = 3
MOBA_ROW_CHUNK = 16
MOBA_TILE_GROUP = 2
LOG2_E = math.log2(math.e)
MASK_BIG = -2.0 ** 100
SSM_CHUNK = 8
S5_GROUP = 16
VMEM_LIMIT = 56 * 1024 * 1024


def _cparams(sem):
    return pltpu.CompilerParams(dimension_semantics=sem, vmem_limit_bytes=VMEM_LIMIT)


def _rms_scale(x):
    return lax.rsqrt(jnp.mean(x * x, axis=-1, keepdims=True) + EPS)


def _rmsnorm_kernel(x_ref, g_ref, o_ref):
    x = x_ref[...]
    o_ref[...] = (x * _rms_scale(x) * g_ref[...]).reshape(o_ref.shape)


def _rmsnorm_interleaved(x2d, g_row, bsz, tm=512):
    m, d = x2d.shape
    tiles = m // bsz // tm
    return pl.pallas_call(
        _rmsnorm_kernel,
        grid=(bsz, tiles),
        in_specs=[pl.BlockSpec((tm, d), lambda b, i: (b * tiles + i, 0)), pl.BlockSpec((1, d), lambda b, i: (0, 0))],
        out_specs=pl.BlockSpec((tm // S5_GROUP, None, S5_GROUP, d), lambda b, i: (i, b, 0, 0)),
        out_shape=jax.ShapeDtypeStruct((m // bsz // S5_GROUP, bsz, S5_GROUP, d), F32),
        compiler_params=_cparams(("parallel", "parallel")),
        name="rmsnorm",
    )(x2d, g_row)


def _s5_tables(lam_re, lam_im, log_dt, b_re, b_im, c_re, c_im):
    g, p, c = b_re.shape
    t = SSM_CHUNK
    gl = LANES // c
    nj = g // gl
    ns = gl * p
    lr, li = lam_re.astype(F32), lam_im.astype(F32)
    dt = jnp.exp(log_dt.astype(F32))[:, None]
    ar, ai = lr * dt, li * dt
    mag = jnp.exp(ar)
    nr, ni = mag * jnp.cos(ai) - 1.0, mag * jnp.sin(ai)
    den = lr * lr + li * li
    fr, fi = (nr * lr + ni * li) / den, (ni * lr - nr * li) / den
    br, bi = b_re.astype(F32), b_im.astype(F32)
    bbr = fr[..., None] * br - fi[..., None] * bi
    bbi = fr[..., None] * bi + fi[..., None] * br
    dd = jnp.arange(t + 1, dtype=F32)[:, None, None]
    pmag = jnp.exp(ar[None] * dd)
    pr, pi = pmag * jnp.cos(ai[None] * dd), pmag * jnp.sin(ai[None] * dd)

    def block_diag(x):
        x = jnp.tile(x.reshape(nj, gl * c, p), (1, 1, gl))
        rg = lax.broadcasted_iota(jnp.int32, (gl * c, ns), 0) // c
        cg = lax.broadcasted_iota(jnp.int32, (gl * c, ns), 1) // p
        return jnp.where(rg == cg, x, 0.0)

    bblk = jnp.concatenate([block_diag(jnp.swapaxes(bbr, 1, 2)), block_diag(jnp.swapaxes(bbi, 1, 2))], axis=-1)
    cblk = jnp.concatenate([block_diag(c_re.astype(F32)), block_diag(c_im.astype(F32))], axis=-1)
    pw = jnp.concatenate([pr.reshape(t + 1, nj, ns), pi.reshape(t + 1, nj, ns)], axis=-1)
    pw = jnp.pad(jnp.swapaxes(pw, 0, 1), ((0, 0), (0, 16 - (t + 1)), (0, 0)))
    return bblk, cblk, pw


def _s5_build_tables(bblk_ref, cblk_ref, pw_ref, w1_scr, w2_scr):
    t = SSM_CHUNK
    ns = bblk_ref.shape[2] // 2
    yw = t * LANES
    bcat = bblk_ref[0]
    bre, bim = bcat[:, :ns], bcat[:, ns:]
    bcat16 = bcat.astype(BF16)
    cre, cim = cblk_ref[0, :, :ns], cblk_ref[0, :, ns:]
    for d in range(t + 1):
        pr, pi = pw_ref[0, d:d + 1, :ns], pw_ref[0, d:d + 1, ns:]
        mo = jnp.concatenate([cre * pr - cim * pi, -(cre * pi + cim * pr)], axis=1).astype(BF16).T
        if d >= 1:
            w2_scr[:, (d - 1) * LANES:d * LANES] = mo
        if d < t:
            kd = jnp.dot(bcat16, mo, preferred_element_type=F32).astype(BF16)
            for ip in range(t - d):
                w1_scr[ip * LANES:(ip + 1) * LANES, (ip + d) * LANES:(ip + d + 1) * LANES] = kd
            ip = t - 1 - d
            w1_scr[ip * LANES:(ip + 1) * LANES, yw:yw + ns] = (bre * pr - bim * pi).astype(BF16)
            w1_scr[ip * LANES:(ip + 1) * LANES, yw + ns:] = (bre * pi + bim * pr).astype(BF16)
    for ip in range(1, t):
        w1_scr[ip * LANES:(ip + 1) * LANES, 0:ip * LANES] = jnp.zeros((LANES, ip * LANES), BF16)


def _s5_kernel(u_ref, d_ref, bblk_ref, cblk_ref, pw_ref, z_ref, w1_scr, w2_scr, lhs_scr, s_scr, yc_scr, y_scr,
               *, n_seq):
    t = SSM_CHUNK
    cpg = S5_GROUP // t
    rows = lhs_scr.shape[0]
    ngroups = rows // (n_seq * cpg)
    nsl = s_scr.shape[0] // 2
    yw = t * LANES

    _s5_build_tables(bblk_ref, cblk_ref, pw_ref, w1_scr, w2_scr)

    for i in range(t):
        lhs_scr[:, i * LANES:(i + 1) * LANES] = u_ref[pl.ds(i, rows, stride=t), :].astype(BF16)
    yc_scr[...] = jnp.dot(lhs_scr[...], w1_scr[:, :yw], preferred_element_type=F32)
    s_in_all = jnp.dot(lhs_scr[...], w1_scr[:, yw:], preferred_element_type=F32)
    for q in range(2 * nsl):
        s_scr[q] = s_in_all[:, q * LANES:(q + 1) * LANES]

    a_re = [pw_ref[0, t:t + 1, q * LANES:(q + 1) * LANES] for q in range(nsl)]
    a_im = [pw_ref[0, t:t + 1, (nsl + q) * LANES:(nsl + q + 1) * LANES] for q in range(nsl)]

    def scan_group(g, carry):
        e = list(carry)
        for c in range(cpg):
            sel = pl.ds(g * (n_seq * cpg) + c, n_seq, stride=cpg)
            for q in range(nsl):
                e_re, e_im = e[q], e[nsl + q]
                in_re, in_im = s_scr[q, sel, :], s_scr[nsl + q, sel, :]
                s_scr[q, sel, :] = e_re
                s_scr[nsl + q, sel, :] = e_im
                e[q] = a_re[q] * e_re - a_im[q] * e_im + in_re
                e[nsl + q] = a_re[q] * e_im + a_im[q] * e_re + in_im
        return tuple(e)

    lax.fori_loop(0, ngroups, scan_group, (jnp.zeros((n_seq, LANES), F32),) * (2 * nsl))

    s_prev = jnp.concatenate([s_scr[q].astype(BF16) for q in range(2 * nsl)], axis=1)
    y = yc_scr[...] + jnp.dot(s_prev, w2_scr[...], preferred_element_type=F32)
    for i in range(t):
        y_scr[pl.ds(i, rows, stride=t), :] = y[:, i * LANES:(i + 1) * LANES]

    yy = y_scr[...] + d_ref[...] * u_ref[...]
    z_ref[...] = jax.nn.gelu(yy).astype(z_ref.dtype)


def _s5_mixer(u4d, d_row, bblk, cblk, pw):
    ngr, bsz, grp, d = u4d.shape
    m = ngr * bsz * grp
    nj = d // LANES
    rows = m // SSM_CHUNK
    ns2 = bblk.shape[2]
    yw = SSM_CHUNK * LANES
    z = pl.pallas_call(
        functools.partial(_s5_kernel, n_seq=bsz),
        grid=(nj,),
        in_specs=[
            pl.BlockSpec((m, LANES), lambda j: (0, j)),
            pl.BlockSpec((1, LANES), lambda j: (0, j)),
            pl.BlockSpec((1, LANES, ns2), lambda j: (j, 0, 0)),
            pl.BlockSpec((1, LANES, ns2), lambda j: (j, 0, 0)),
            pl.BlockSpec((1, 16, ns2), lambda j: (j, 0, 0)),
        ],
        out_specs=pl.BlockSpec((m, LANES), lambda j: (0, j)),
        out_shape=jax.ShapeDtypeStruct((m, d), BF16),
        scratch_shapes=[
            pltpu.VMEM((yw, yw + ns2), BF16),
            pltpu.VMEM((ns2, yw), BF16),
            pltpu.VMEM((rows, yw), BF16),
            pltpu.VMEM((ns2 // LANES, rows, LANES), F32),
            pltpu.VMEM((rows, yw), F32),
            pltpu.VMEM((m, LANES), F32),
        ],
        compiler_params=_cparams(("parallel",)),
        name="s5_mixer",
    )(u4d.reshape(m, d), d_row, bblk, cblk, pw)
    return z.reshape(ngr, bsz, grp, d)


def _glu_kernel(z_ref, w_ref, b_ref, res_ref, o_ref, w16_scr):
    tm, n = o_ref.shape

    @pl.when((pl.program_id(0) == 0) & (pl.program_id(1) == 0))
    def _():
        w16_scr[...] = w_ref[...].astype(BF16)

    z = z_ref[...].reshape(tm, n)
    a = jnp.dot(z, w16_scr[...], preferred_element_type=F32) + b_ref[...]
    o_ref[...] = res_ref[...] + z.astype(F32) * jax.nn.sigmoid(a)


def _glu_residual(z4d, w, layer, b_row, res, tm=512):
    ngr, bsz, grp, k = z4d.shape
    m = ngr * bsz * grp
    n = w.shape[2]
    assert k == n and (ngr * grp) % tm == 0 and tm % grp == 0
    tiles = ngr * grp // tm
    return pl.pallas_call(
        _glu_kernel,
        grid=(bsz, tiles),
        in_specs=[
            pl.BlockSpec((tm // grp, None, grp, k), lambda b, i: (i, b, 0, 0)),
            pl.BlockSpec((None, k, n), lambda b, i: (layer, 0, 0), pipeline_mode=pl.Buffered(1)),
            pl.BlockSpec((1, n), lambda b, i: (0, 0)),
            pl.BlockSpec((tm, n), lambda b, i: (b * tiles + i, 0)),
        ],
        out_specs=pl.BlockSpec((tm, n), lambda b, i: (b * tiles + i, 0)),
        out_shape=jax.ShapeDtypeStruct((m, n), F32),
        scratch_shapes=[pltpu.VMEM((k, n), BF16)],
        compiler_params=_cparams(("arbitrary", "arbitrary")),
        name="glu_residual",
    )(z4d, w, b_row, res)


def _proj_res_kernel(x_ref, w_ref, res_ref, o_ref, w16_scr):
    @pl.when(pl.program_id(0) == 0)
    def _():
        w16_scr[...] = w_ref[...].astype(BF16)

    o_ref[...] = res_ref[...] + jnp.dot(x_ref[...], w16_scr[...], preferred_element_type=F32)


def _proj_residual(x, w, layer, res, tm=512):
    m, k = x.shape
    n = w.shape[2]
    assert m % tm == 0
    return pl.pallas_call(
        _proj_res_kernel,
        grid=(m // tm,),
        in_specs=[
            pl.BlockSpec((tm, k), lambda i: (i, 0)),
            pl.BlockSpec((None, k, n), lambda i: (layer, 0, 0), pipeline_mode=pl.Buffered(1)),
            pl.BlockSpec((tm, n), lambda i: (i, 0)),
        ],
        out_specs=pl.BlockSpec((tm, n), lambda i: (i, 0)),
        out_shape=jax.ShapeDtypeStruct((m, n), F32),
        scratch_shapes=[pltpu.VMEM((k, n), BF16)],
        compiler_params=_cparams(("arbitrary",)),
        name="proj_residual",
    )(x, w, res)


def _qkv_kernel(x_ref, gq_ref, gkv_ref, wq_ref, wkv_ref, o_ref, uq_scr, ukv_scr, *, n_q_tiles, q_scale):
    j = pl.program_id(1)

    @pl.when(j == 0)
    def _():
        x = x_ref[...]
        xn = x * _rms_scale(x)
        uq_scr[...] = (xn * gq_ref[...]).astype(BF16)
        ukv_scr[...] = (xn * gkv_ref[...]).astype(BF16)

    @pl.when(j < n_q_tiles)
    def _():
        q = jnp.dot(uq_scr[...], wq_ref[...].astype(BF16), preferred_element_type=F32)
        o_ref[...] = (q * q_scale).astype(o_ref.dtype)

    @pl.when(j >= n_q_tiles)
    def _():
        o_ref[...] = jnp.dot(ukv_scr[...], wkv_ref[...].astype(BF16), preferred_element_type=F32).astype(o_ref.dtype)


def _qkv_proj(h2d, gq_row, gkv_row, w_q, layer, w_kv, q_scale, tm=1024, tn=512):
    m, k = h2d.shape
    nq = w_q.shape[2] // tn
    nkv = w_kv.shape[1] // tn
    return pl.pallas_call(
        functools.partial(_qkv_kernel, n_q_tiles=nq, q_scale=q_scale),
        grid=(m // tm, nq + nkv),
        in_specs=[
            pl.BlockSpec((tm, k), lambda i, j: (i, 0), pipeline_mode=pl.Buffered(1)),
            pl.BlockSpec((1, k), lambda i, j: (0, 0)),
            pl.BlockSpec((1, k), lambda i, j: (0, 0)),
            pl.BlockSpec((None, k, tn), lambda i, j: (layer, 0, jnp.minimum(j, nq - 1))),
            pl.BlockSpec((k, tn), lambda i, j: (0, jnp.maximum(j - nq, 0))),
        ],
        out_specs=pl.BlockSpec((tm, tn), lambda i, j: (i, j)),
        out_shape=jax.ShapeDtypeStruct((m, (nq + nkv) * tn), BF16),
        scratch_shapes=[pltpu.VMEM((tm, k), BF16), pltpu.VMEM((tm, k), BF16)],
        compiler_params=_cparams(("parallel", "arbitrary")),
        name="qkv_proj",
    )(h2d, gq_row, gkv_row, w_q, w_kv)


def _ffn_kernel(h_ref, g_ref, wg_ref, wu_ref, wd_ref, gfin_ref, o_ref, u_scr, *, final_norm):
    f = pl.program_id(1)

    @pl.when(f == 0)
    def _():
        x = h_ref[...]
        u_scr[...] = (x * _rms_scale(x) * g_ref[...]).astype(BF16)
        o_ref[...] = x

    u = u_scr[...]
    gate = jnp.dot(u, wg_ref[...].astype(BF16), preferred_element_type=F32)
    up = jnp.dot(u, wu_ref[...].astype(BF16), preferred_element_type=F32)
    act = (jax.nn.silu(gate) * up).astype(BF16)
    o_ref[...] += jnp.dot(act, wd_ref[...].astype(BF16), preferred_element_type=F32)

    if final_norm:
        @pl.when(f == pl.num_programs(1) - 1)
        def _():
            y = o_ref[...]
            o_ref[...] = y * _rms_scale(y) * gfin_ref[...]


def _ffn_residual(h2d, g_row, w_gate, w_up, w_down, layer, gfin_row, final_norm, tm=1024, tf=256):
    m, d = h2d.shape
    ff = w_gate.shape[2]
    return pl.pallas_call(
        functools.partial(_ffn_kernel, final_norm=final_norm),
        grid=(m // tm, ff // tf),
        in_specs=[
            pl.BlockSpec((tm, d), lambda i, f: (i, 0)),
            pl.BlockSpec((1, d), lambda i, f: (0, 0)),
            pl.BlockSpec((None, d, tf), lambda i, f: (layer, 0, f)),
            pl.BlockSpec((None, d, tf), lambda i, f: (layer, 0, f)),
            pl.BlockSpec((None, tf, d), lambda i, f: (layer, f, 0)),
            pl.BlockSpec((1, d), lambda i, f: (0, 0)),
        ],
        out_specs=pl.BlockSpec((tm, d), lambda i, f: (i, 0)),
        out_shape=jax.ShapeDtypeStruct((m, d), F32),
        scratch_shapes=[pltpu.VMEM((tm, d), BF16)],
        compiler_params=_cparams(("parallel", "arbitrary")),
        name="ffn_residual",
    )(h2d, g_row, w_gate, w_up, w_down, gfin_row)


def _moba_tables(n_heads, seq_len):
    nb = seq_len // KEY_BLOCK
    assert nb + 6 <= LANES
    slopes = jnp.exp2(-8.0 * jnp.arange(1, n_heads + 1, dtype=F32) / n_heads) * LOG2_E
    pos = jnp.arange(seq_len, dtype=F32)
    bias = slopes[:, None] * pos[None, :]

    top = 2.0 ** math.ceil(math.log2(LOG2_E * seq_len))

    def split3(x, sign):
        terms, quantum = [], top / 256.0
        for _ in range(2):
            term = jnp.floor(x / quantum) * quantum
            terms.append(sign * term)
            x = x - term
            quantum = quantum / 256.0
        return terms + [sign * x]

    col = jnp.arange(LANES)[None, None, :]
    one_hot_block = (col == (jnp.arange(seq_len) // KEY_BLOCK)[None, :, None]).astype(F32)

    def table(first3, last3, head):
        t = jnp.where(col < nb, head, 0.0)
        for c in range(3):
            t = jnp.where(col == nb + c, first3[c][..., None] if first3 is not None else 1.0, t)
            t = jnp.where(col == nb + 3 + c, last3[c][..., None] if last3 is not None else 1.0, t)
        return t.astype(BF16)

    qx = table(split3(bias, -1.0), None, 0.0)
    kx = table(None, split3(bias, 1.0), one_hot_block)
    return qx, kx


def _moba_kernel(q_ref, k_ref, v_ref, qx_ref, kx_ref, o_ref, kmean_scr, causal_scr, qa_scr, ka_scr, va_scr,
                 s_scr, p_scr):
    kb = KEY_BLOCK
    hd = HEAD_DIM
    rc_rows = MOBA_ROW_CHUNK
    seq = k_ref.shape[1]
    nb = seq // kb
    neg_inf = -jnp.inf
    nt = (((1,), (1,)), ((), ()))

    qa_scr[:, :hd] = q_ref[0]
    ka_scr[:, :hd] = k_ref[0]
    ka_scr[:, hd:] = kx_ref[0]
    va_scr[:, :hd] = v_ref[0]
    va_scr[:, hd:] = jnp.ones((seq, hd), BF16)
    kmean_scr[...] = jnp.zeros_like(kmean_scr)
    for n in range(nb):
        kmean_scr[n:n + 1, :] = jnp.mean(k_ref[0, n * kb:(n + 1) * kb, :].astype(F32), axis=0, keepdims=True)
    row = lax.broadcasted_iota(jnp.int32, (kb, kb), 0)
    colk = lax.broadcasted_iota(jnp.int32, (kb, kb), 1)
    causal_scr[...] = jnp.where(row >= colk, 0.0, neg_inf)

    g0 = min((TOP_K_BLOCKS + 1) * kb, seq)
    qa_scr[:g0, hd:] = qx_ref[0, :g0, :]
    if seq > g0:
        ng = seq - g0
        gate = lax.dot_general(q_ref[0, g0:, :].astype(F32), kmean_scr[...], nt,
                               precision=lax.Precision.HIGHEST, preferred_element_type=F32)
        col = lax.broadcasted_iota(jnp.int32, (ng, LANES), 1)
        colf = col.astype(F32)
        own_v = lax.div(lax.broadcasted_iota(jnp.int32, (ng, LANES), 0), jnp.int32(kb)) + (TOP_K_BLOCKS + 1)
        past = col < own_v
        gate = jnp.where(past, gate, neg_inf)
        mask = jnp.where(past, MASK_BIG, 0.0)
        for _ in range(TOP_K_BLOCKS):
            mx = jnp.max(gate, axis=-1, keepdims=True)
            idx = jnp.min(jnp.where(gate == mx, colf, float(LANES)), axis=-1, keepdims=True)
            hit = colf == idx
            mask = jnp.where(hit, 0.0, mask)
            gate = jnp.where(hit, neg_inf, gate)
        qa_scr[g0:, hd:] = (qx_ref[0, g0:, :].astype(F32) + mask).astype(BF16)

    for slot, t0 in enumerate(range(0, nb, MOBA_TILE_GROUP)):
        slot = slot % 2
        tiles = range(t0, min(t0 + MOBA_TILE_GROUP, nb))
        q0 = t0 * kb
        rows = len(tiles) * kb
        width = (tiles[-1] + 1) * kb
        s_scr[slot, 0:rows, 0:width] = lax.dot_general(qa_scr[q0:q0 + rows, :], ka_scr[0:width, :], nt,
                                                       preferred_element_type=F32)
        for own in tiles:
            base = (own - t0) * kb
            for rc in range(kb // rc_rows):
                r0, r1 = base + rc * rc_rows, base + (rc + 1) * rc_rows
                parts = [s_scr[slot, r0:r1, n * kb:(n + 1) * kb] for n in range(own)]
                parts.append(s_scr[slot, r0:r1, own * kb:(own + 1) * kb] + causal_scr[r0 - base:r1 - base, :])
                m = jnp.max(functools.reduce(jnp.maximum, parts), axis=-1, keepdims=True)
                for n in range(own + 1):
                    p_scr[slot, r0:r1, n * kb:(n + 1) * kb] = jnp.exp2(parts[n] - m).astype(BF16)
            if (own + 1) * kb < width:
                p_scr[slot, base:base + kb, (own + 1) * kb:width] = jnp.zeros((kb, width - (own + 1) * kb), BF16)

        res = jnp.dot(p_scr[slot, 0:rows, 0:width], va_scr[0:width, :], preferred_element_type=F32)
        o_ref[0, q0:q0 + rows, :] = (res[:, :hd] / res[:, hd:]).astype(o_ref.dtype)


def _moba_attention(qkv3d, d_model):
    b, l, _ = qkv3d.shape
    n_heads = d_model // HEAD_DIM
    kb = KEY_BLOCK
    qx, kx = _moba_tables(n_heads, l)
    return pl.pallas_call(
        _moba_kernel,
        grid=(b, n_heads),
        in_specs=[
            pl.BlockSpec((1, l, HEAD_DIM), lambda i, h: (i, 0, h)),
            pl.BlockSpec((1, l, HEAD_DIM), lambda i, h: (i, 0, n_heads + h)),
            pl.BlockSpec((1, l, HEAD_DIM), lambda i, h: (i, 0, 2 * n_heads + h)),
            pl.BlockSpec((1, l, LANES), lambda i, h: (h, 0, 0)),
            pl.BlockSpec((1, l, LANES), lambda i, h: (h, 0, 0)),
        ],
        out_specs=pl.BlockSpec((1, l, HEAD_DIM), lambda i, h: (i, 0, h)),
        out_shape=jax.ShapeDtypeStruct((b, l, d_model), BF16),
        scratch_shapes=[
            pltpu.VMEM((LANES, HEAD_DIM), F32),
            pltpu.VMEM((kb, kb), F32),
            pltpu.VMEM((l, HEAD_DIM + LANES), BF16),
            pltpu.VMEM((l, HEAD_DIM + LANES), BF16),
            pltpu.VMEM((l, 2 * HEAD_DIM), BF16),
            pltpu.VMEM((2, MOBA_TILE_GROUP * kb, l), F32),
            pltpu.VMEM((2, MOBA_TILE_GROUP * kb, l), BF16),
        ],
        compiler_params=_cparams(("parallel", "parallel")),
        name="moba_attention",
    )(qkv3d, qkv3d, qkv3d, qx, kx)


def kernel(x, g_mix, lambda_re, lambda_im, log_dt, b_re, b_im, c_re, c_im, d_skip, w_glu, b_glu, g_kv, w_kv,
           w_q, w_o, g_ffn, w_gate, w_up, w_down, g_final):
    bsz, l, d = x.shape
    depth = g_mix.shape[0]
    n_a = lambda_re.shape[0]
    m = bsz * l
    assert l % KEY_BLOCK == 0 and d % HEAD_DIM == 0 and l // KEY_BLOCK <= LANES
    row = lambda v: v.reshape(1, -1).astype(F32)

    h = x.astype(F32).reshape(m, d)
    qkv = None
    for i in range(depth):
        if i < n_a:
            tables = _s5_tables(lambda_re[i], lambda_im[i], log_dt[i], b_re[i], b_im[i], c_re[i], c_im[i])
            u = _rmsnorm_interleaved(h, row(g_mix[i]), bsz)
            z = _s5_mixer(u, row(d_skip[i]), *tables)
            h = _glu_residual(z, w_glu, i, row(b_glu[i]), h)
        else:
            j = i - n_a
            qkv_i = _qkv_proj(h, row(g_mix[i]), row(g_kv), w_q, j, w_kv, HEAD_DIM ** -0.5 * LOG2_E)
            if qkv is None:
                qkv = qkv_i
            else:
                qkv = jnp.concatenate([qkv_i[:, :d], qkv[:, d:]], axis=1)
            attn = _moba_attention(qkv.reshape(bsz, l, 3 * d), d)
            h = _proj_residual(attn.reshape(m, d), w_o, j, h)
        h = _ffn_residual(h, row(g_ffn[i]), w_gate, w_up, w_down, i, row(g_final), final_norm=(i == depth - 1))
    return h.reshape(bsz, l, d).astype(x.dtype)
```

```python
import functools
import math

import jax
import jax.numpy as jnp
import numpy as np
from jax import lax
from jax.experimental import pallas as pl
from jax.experimental.pallas import tpu as pltpu

F32 = jnp.float32
BF16 = jnp.bfloat16

EPS = 1e-6
LANES = 128
HEAD_DIM = 128
KEY_BLOCK = 256
TOP_K_BLOCKS = 3
MOBA_ROW_CHUNK = 16
MOBA_TILE_GROUP = 2
MOBA_BIAS_TERMS = 4
LOG2_E = math.log2(math.e)
MASK_BIG = -2.0 ** 100
SSM_CHUNK = 8
S5_GROUP = 16
VMEM_LIMIT = 56 * 1024 * 1024


def _cparams(sem):
    return pltpu.CompilerParams(dimension_semantics=sem, vmem_limit_bytes=VMEM_LIMIT)


def _rms_scale(x):
    return lax.rsqrt(jnp.mean(x * x, axis=-1, keepdims=True) + EPS)


def _rmsnorm_kernel(x_ref, g_ref, o_ref):
    x = x_ref[...]
    o_ref[...] = (x * _rms_scale(x) * g_ref[...]).reshape(o_ref.shape)


def _rmsnorm_interleaved(x2d, g_row, bsz, tm=512):
    m, d = x2d.shape
    tiles = m // bsz // tm
    return pl.pallas_call(
        _rmsnorm_kernel,
        grid=(bsz, tiles),
        in_specs=[pl.BlockSpec((tm, d), lambda b, i: (b * tiles + i, 0)), pl.BlockSpec((1, d), lambda b, i: (0, 0))],
        out_specs=pl.BlockSpec((tm // S5_GROUP, None, S5_GROUP, d), lambda b, i: (i, b, 0, 0)),
        out_shape=jax.ShapeDtypeStruct((m // bsz // S5_GROUP, bsz, S5_GROUP, d), F32),
        compiler_params=_cparams(("parallel", "parallel")),
        name="rmsnorm",
    )(x2d, g_row)


def _s5_tables(lam_re, lam_im, log_dt, b_re, b_im, c_re, c_im):
    g, p, c = b_re.shape
    t = SSM_CHUNK
    gl = LANES // c
    nj = g // gl
    ns = gl * p
    lr, li = lam_re.astype(F32), lam_im.astype(F32)
    dt = jnp.exp(log_dt.astype(F32))[:, None]
    ar, ai = lr * dt, li * dt
    mag = jnp.exp(ar)
    nr, ni = mag * jnp.cos(ai) - 1.0, mag * jnp.sin(ai)
    den = lr * lr + li * li
    fr, fi = (nr * lr + ni * li) / den, (ni * lr - nr * li) / den
    br, bi = b_re.astype(F32), b_im.astype(F32)
    bbr = fr[..., None] * br - fi[..., None] * bi
    bbi = fr[..., None] * bi + fi[..., None] * br
    dd = jnp.arange(t + 1, dtype=F32)[:, None, None]
    pmag = jnp.exp(ar[None] * dd)
    pr, pi = pmag * jnp.cos(ai[None] * dd), pmag * jnp.sin(ai[None] * dd)

    def block_diag(x):
        x = jnp.tile(x.reshape(nj, gl * c, p), (1, 1, gl))
        rg = lax.broadcasted_iota(jnp.int32, (gl * c, ns), 0) // c
        cg = lax.broadcasted_iota(jnp.int32, (gl * c, ns), 1) // p
        return jnp.where(rg == cg, x, 0.0)

    bblk = jnp.concatenate([block_diag(jnp.swapaxes(bbr, 1, 2)), block_diag(jnp.swapaxes(bbi, 1, 2))], axis=-1)
    cblk = jnp.concatenate([block_diag(c_re.astype(F32)), block_diag(c_im.astype(F32))], axis=-1)
    pw = jnp.concatenate([pr.reshape(t + 1, nj, ns), pi.reshape(t + 1, nj, ns)], axis=-1)
    pw = jnp.pad(jnp.swapaxes(pw, 0, 1), ((0, 0), (0, 16 - (t + 1)), (0, 0)))
    return bblk, cblk, pw


def _s5_build_tables(bblk_ref, cblk_ref, pw_ref, w1_scr, w2_scr):
    t = SSM_CHUNK
    ns = bblk_ref.shape[2] // 2
    yw = t * LANES
    bcat = bblk_ref[0]
    bre, bim = bcat[:, :ns], bcat[:, ns:]
    bcat16 = bcat.astype(BF16)
    cre, cim = cblk_ref[0, :, :ns], cblk_ref[0, :, ns:]
    nt = (((1,), (1,)), ((), ()))
    for d in range(t + 1):
        pr, pi = pw_ref[0, d:d + 1, :ns], pw_ref[0, d:d + 1, ns:]
        mo = jnp.concatenate([cre * pr - cim * pi, -(cre * pi + cim * pr)], axis=1).astype(BF16)
        if d >= 1:
            w2_scr[(d - 1) * LANES:d * LANES, :] = mo
        if d < t:
            kd = lax.dot_general(bcat16, mo, nt, preferred_element_type=F32).astype(BF16)
            for ip in range(t - d):
                w1_scr[ip * LANES:(ip + 1) * LANES, (ip + d) * LANES:(ip + d + 1) * LANES] = kd
            ip = t - 1 - d
            w1_scr[ip * LANES:(ip + 1) * LANES, yw:yw + ns] = (bre * pr - bim * pi).astype(BF16)
            w1_scr[ip * LANES:(ip + 1) * LANES, yw + ns:] = (bre * pi + bim * pr).astype(BF16)
    for ip in range(1, t):
        w1_scr[ip * LANES:(ip + 1) * LANES, 0:ip * LANES] = jnp.zeros((LANES, ip * LANES), BF16)


def _s5_kernel(u_ref, d_ref, bblk_ref, cblk_ref, pw_ref, z_ref, w1_scr, w2_scr, lhs_scr, s_scr, yc_scr, y_scr,
               *, n_seq):
    t = SSM_CHUNK
    cpg = S5_GROUP // t
    rows = lhs_scr.shape[0]
    ngroups = rows // (n_seq * cpg)
    nsl = s_scr.shape[0] // 2
    yw = t * LANES

    _s5_build_tables(bblk_ref, cblk_ref, pw_ref, w1_scr, w2_scr)

    for i in range(t):
        lhs_scr[:, i * LANES:(i + 1) * LANES] = u_ref[pl.ds(i, rows, stride=t), :].astype(BF16)
    strip = 2 * LANES
    for c0 in range(0, yw, strip):
        c1 = c0 + strip
        yc_scr[:, c0:c1] = jnp.dot(lhs_scr[:, :c1], w1_scr[:c1, c0:c1], preferred_element_type=F32)
    s_in_all = jnp.dot(lhs_scr[...], w1_scr[:, yw:], preferred_element_type=F32)
    for q in range(2 * nsl):
        s_scr[q] = s_in_all[:, q * LANES:(q + 1) * LANES]

    a_re = [pw_ref[0, t:t + 1, q * LANES:(q + 1) * LANES] for q in range(nsl)]
    a_im = [pw_ref[0, t:t + 1, (nsl + q) * LANES:(nsl + q + 1) * LANES] for q in range(nsl)]

    def scan_group(g, carry):
        e = list(carry)
        for c in range(cpg):
            sel = pl.ds(g * (n_seq * cpg) + c, n_seq, stride=cpg)
            for q in range(nsl):
                e_re, e_im = e[q], e[nsl + q]
                in_re, in_im = s_scr[q, sel, :], s_scr[nsl + q, sel, :]
                s_scr[q, sel, :] = e_re
                s_scr[nsl + q, sel, :] = e_im
                e[q] = a_re[q] * e_re - a_im[q] * e_im + in_re
                e[nsl + q] = a_re[q] * e_im + a_im[q] * e_re + in_im
        return tuple(e)

    lax.fori_loop(0, ngroups, scan_group, (jnp.zeros((n_seq, LANES), F32),) * (2 * nsl))

    s_prev = jnp.concatenate([s_scr[q].astype(BF16) for q in range(2 * nsl)], axis=1)
    y = yc_scr[...] + lax.dot_general(s_prev, w2_scr[...], (((1,), (1,)), ((), ())), preferred_element_type=F32)
    for i in range(t):
        y_scr[pl.ds(i, rows, stride=t), :] = y[:, i * LANES:(i + 1) * LANES]

    yy = y_scr[...] + d_ref[...] * u_ref[...]
    z_ref[...] = jax.nn.gelu(yy).astype(z_ref.dtype)


def _s5_mixer(u4d, d_row, bblk, cblk, pw):
    ngr, bsz, grp, d = u4d.shape
    m = ngr * bsz * grp
    nj = d // LANES
    rows = m // SSM_CHUNK
    ns2 = bblk.shape[2]
    yw = SSM_CHUNK * LANES
    z = pl.pallas_call(
        functools.partial(_s5_kernel, n_seq=bsz),
        grid=(nj,),
        in_specs=[
            pl.BlockSpec((m, LANES), lambda j: (0, j)),
            pl.BlockSpec((1, LANES), lambda j: (0, j)),
            pl.BlockSpec((1, LANES, ns2), lambda j: (j, 0, 0)),
            pl.BlockSpec((1, LANES, ns2), lambda j: (j, 0, 0)),
            pl.BlockSpec((1, 16, ns2), lambda j: (j, 0, 0)),
        ],
        out_specs=pl.BlockSpec((m, LANES), lambda j: (0, j)),
        out_shape=jax.ShapeDtypeStruct((m, d), BF16),
        scratch_shapes=[
            pltpu.VMEM((yw, yw + ns2), BF16),
            pltpu.VMEM((yw, ns2), BF16),
            pltpu.VMEM((rows, yw), BF16),
            pltpu.VMEM((ns2 // LANES, rows, LANES), F32),
            pltpu.VMEM((rows, yw), F32),
            pltpu.VMEM((m, LANES), F32),
        ],
        compiler_params=_cparams(("parallel",)),
        name="s5_mixer",
    )(u4d.reshape(m, d), d_row, bblk, cblk, pw)
    return z.reshape(ngr, bsz, grp, d)


def _glu_kernel(z_ref, w_ref, b_ref, res_ref, o_ref, w16_scr):
    tm, n = o_ref.shape

    @pl.when((pl.program_id(0) == 0) & (pl.program_id(1) == 0))
    def _():
        w16_scr[...] = w_ref[...].astype(BF16)

    z = z_ref[...].reshape(tm, n)
    a = jnp.dot(z, w16_scr[...], preferred_element_type=F32) + b_ref[...]
    o_ref[...] = res_ref[...] + z.astype(F32) * jax.nn.sigmoid(a)


def _glu_residual(z4d, w, layer, b_row, res, tm=512):
    ngr, bsz, grp, k = z4d.shape
    m = ngr * bsz * grp
    n = w.shape[2]
    assert k == n and (ngr * grp) % tm == 0 and tm % grp == 0
    tiles = ngr * grp // tm
    return pl.pallas_call(
        _glu_kernel,
        grid=(bsz, tiles),
        in_specs=[
            pl.BlockSpec((tm // grp, None, grp, k), lambda b, i: (i, b, 0, 0)),
            pl.BlockSpec((None, k, n), lambda b, i: (layer, 0, 0), pipeline_mode=pl.Buffered(1)),
            pl.BlockSpec((1, n), lambda b, i: (0, 0)),
            pl.BlockSpec((tm, n), lambda b, i: (b * tiles + i, 0)),
        ],
        out_specs=pl.BlockSpec((tm, n), lambda b, i: (b * tiles + i, 0)),
        out_shape=jax.ShapeDtypeStruct((m, n), F32),
        scratch_shapes=[pltpu.VMEM((k, n), BF16)],
        compiler_params=_cparams(("arbitrary", "arbitrary")),
        name="glu_residual",
    )(z4d, w, b_row, res)


def _proj_res_kernel(x_ref, w_ref, res_ref, o_ref, w16_scr):
    @pl.when(pl.program_id(0) == 0)
    def _():
        w16_scr[...] = w_ref[...].astype(BF16)

    o_ref[...] = res_ref[...] + jnp.dot(x_ref[...], w16_scr[...], preferred_element_type=F32)


def _proj_residual(x, w, layer, res, tm=512):
    m, k = x.shape
    n = w.shape[2]
    assert m % tm == 0
    return pl.pallas_call(
        _proj_res_kernel,
        grid=(m // tm,),
        in_specs=[
            pl.BlockSpec((tm, k), lambda i: (i, 0)),
            pl.BlockSpec((None, k, n), lambda i: (layer, 0, 0), pipeline_mode=pl.Buffered(1)),
            pl.BlockSpec((tm, n), lambda i: (i, 0)),
        ],
        out_specs=pl.BlockSpec((tm, n), lambda i: (i, 0)),
        out_shape=jax.ShapeDtypeStruct((m, n), F32),
        scratch_shapes=[pltpu.VMEM((k, n), BF16)],
        compiler_params=_cparams(("arbitrary",)),
        name="proj_residual",
    )(x, w, res)


def _qkv_kernel(x_ref, gq_ref, gkv_ref, wq_ref, wkv_ref, o_ref, uq_scr, ukv_scr, *, n_q_tiles, q_scale):
    j = pl.program_id(1)

    @pl.when(j == 0)
    def _():
        x = x_ref[...]
        xn = x * _rms_scale(x)
        uq_scr[...] = (xn * gq_ref[...]).astype(BF16)
        ukv_scr[...] = (xn * gkv_ref[...]).astype(BF16)

    @pl.when(j < n_q_tiles)
    def _():
        q = jnp.dot(uq_scr[...], wq_ref[...].astype(BF16), preferred_element_type=F32)
        o_ref[...] = (q * q_scale).astype(o_ref.dtype)

    @pl.when(j >= n_q_tiles)
    def _():
        o_ref[...] = jnp.dot(ukv_scr[...], wkv_ref[...].astype(BF16), preferred_element_type=F32).astype(o_ref.dtype)


def _qkv_proj(h2d, gq_row, gkv_row, w_q, layer, w_kv, q_scale, tm=1024, tn=512):
    m, k = h2d.shape
    nq = w_q.shape[2] // tn
    nkv = w_kv.shape[1] // tn
    return pl.pallas_call(
        functools.partial(_qkv_kernel, n_q_tiles=nq, q_scale=q_scale),
        grid=(m // tm, nq + nkv),
        in_specs=[
            pl.BlockSpec((tm, k), lambda i, j: (i, 0), pipeline_mode=pl.Buffered(1)),
            pl.BlockSpec((1, k), lambda i, j: (0, 0)),
            pl.BlockSpec((1, k), lambda i, j: (0, 0)),
            pl.BlockSpec((None, k, tn), lambda i, j: (layer, 0, jnp.minimum(j, nq - 1))),
            pl.BlockSpec((k, tn), lambda i, j: (0, jnp.maximum(j - nq, 0))),
        ],
        out_specs=pl.BlockSpec((tm, tn), lambda i, j: (i, j)),
        out_shape=jax.ShapeDtypeStruct((m, (nq + nkv) * tn), BF16),
        scratch_shapes=[pltpu.VMEM((tm, k), BF16), pltpu.VMEM((tm, k), BF16)],
        compiler_params=_cparams(("parallel", "arbitrary")),
        name="qkv_proj",
    )(h2d, gq_row, gkv_row, w_q, w_kv)


def _ffn_kernel(h_ref, g_ref, wg_ref, wu_ref, wd_ref, gfin_ref, o_ref, u_scr, *, final_norm):
    f = pl.program_id(1)

    @pl.when(f == 0)
    def _():
        x = h_ref[...]
        u_scr[...] = (x * _rms_scale(x) * g_ref[...]).astype(BF16)
        o_ref[...] = x

    u = u_scr[...]
    gate = jnp.dot(u, wg_ref[...].astype(BF16), preferred_element_type=F32)
    up = jnp.dot(u, wu_ref[...].astype(BF16), preferred_element_type=F32)
    act = (jax.nn.silu(gate) * up).astype(BF16)
    o_ref[...] += jnp.dot(act, wd_ref[...].astype(BF16), preferred_element_type=F32)

    if final_norm:
        @pl.when(f == pl.num_programs(1) - 1)
        def _():
            y = o_ref[...]
            o_ref[...] = y * _rms_scale(y) * gfin_ref[...]


def _ffn_residual(h2d, g_row, w_gate, w_up, w_down, layer, gfin_row, final_norm, tm=1024, tf=256):
    m, d = h2d.shape
    ff = w_gate.shape[2]
    return pl.pallas_call(
        functools.partial(_ffn_kernel, final_norm=final_norm),
        grid=(m // tm, ff // tf),
        in_specs=[
            pl.BlockSpec((tm, d), lambda i, f: (i, 0)),
            pl.BlockSpec((1, d), lambda i, f: (0, 0)),
            pl.BlockSpec((None, d, tf), lambda i, f: (layer, 0, f)),
            pl.BlockSpec((None, d, tf), lambda i, f: (layer, 0, f)),
            pl.BlockSpec((None, tf, d), lambda i, f: (layer, f, 0)),
            pl.BlockSpec((1, d), lambda i, f: (0, 0)),
        ],
        out_specs=pl.BlockSpec((tm, d), lambda i, f: (i, 0)),
        out_shape=jax.ShapeDtypeStruct((m, d), F32),
        scratch_shapes=[pltpu.VMEM((tm, d), BF16)],
        compiler_params=_cparams(("parallel", "arbitrary")),
        name="ffn_residual",
    )(h2d, g_row, w_gate, w_up, w_down, gfin_row)


def _moba_tables(n_heads, seq_len):
    nb = seq_len // KEY_BLOCK
    nt = MOBA_BIAS_TERMS
    assert nb + 2 * nt <= LANES
    slopes = np.exp2(-8.0 * np.arange(1, n_heads + 1, dtype=np.float64) / n_heads) * LOG2_E
    bias = slopes[:, None] * np.arange(seq_len, dtype=np.float64)[None, :]
    top = 2.0 ** math.ceil(math.log2(LOG2_E * seq_len))

    terms, rest, quantum = [], bias, top / 256.0
    for _ in range(nt - 1):
        term = np.floor(rest / quantum) * quantum
        terms.append(term)
        rest = rest - term
        quantum = quantum / 256.0
    terms.append(rest)

    qx = np.zeros((n_heads, seq_len, LANES), np.float64)
    kx = np.zeros((n_heads, seq_len, LANES), np.float64)
    kx[:, np.arange(seq_len), np.arange(seq_len) // KEY_BLOCK] = 1.0
    for c, term in enumerate(terms):
        qx[:, :, nb + c] = -term
        qx[:, :, nb + nt + c] = 1.0
        kx[:, :, nb + c] = 1.0
        kx[:, :, nb + nt + c] = term
    return jnp.asarray(qx.astype(jnp.bfloat16)), jnp.asarray(kx.astype(jnp.bfloat16))


def _moba_kernel(q_ref, k_ref, v_ref, qx_ref, kx_ref, o_ref, kmean_scr, causal_scr, qa_scr, ka_scr, va_scr,
                 s_scr, p_scr):
    kb = KEY_BLOCK
    hd = HEAD_DIM
    rc_rows = MOBA_ROW_CHUNK
    seq = k_ref.shape[1]
    nb = seq // kb
    neg_inf = -jnp.inf
    nt = (((1,), (1,)), ((), ()))

    qa_scr[:, :hd] = q_ref[0]
    ka_scr[:, :hd] = k_ref[0]
    ka_scr[:, hd:] = kx_ref[0]
    va_scr[:, :hd] = v_ref[0]
    va_scr[:, hd:] = jnp.ones((seq, hd), BF16)
    kmean_scr[...] = jnp.zeros_like(kmean_scr)
    for n in range(nb):
        kmean_scr[n:n + 1, :] = jnp.mean(k_ref[0, n * kb:(n + 1) * kb, :].astype(F32), axis=0, keepdims=True)
    row = lax.broadcasted_iota(jnp.int32, (kb, kb), 0)
    colk = lax.broadcasted_iota(jnp.int32, (kb, kb), 1)
    causal_scr[...] = jnp.where(row >= colk, 0.0, neg_inf)

    g0 = min((TOP_K_BLOCKS + 1) * kb, seq)
    qa_scr[:g0, hd:] = qx_ref[0, :g0, :]
    if seq > g0:
        ng = seq - g0
        gate = lax.dot_general(q_ref[0, g0:, :].astype(F32), kmean_scr[...], nt,
                               precision=lax.Precision.HIGHEST, preferred_element_type=F32)
        col = lax.broadcasted_iota(jnp.int32, (ng, LANES), 1)
        colf = col.astype(F32)
        own_v = lax.div(lax.broadcasted_iota(jnp.int32, (ng, LANES), 0), jnp.int32(kb)) + (TOP_K_BLOCKS + 1)
        past = col < own_v
        gate = jnp.where(past, gate, neg_inf)
        mask = jnp.where(past, MASK_BIG, 0.0)
        for _ in range(TOP_K_BLOCKS):
            mx = jnp.max(gate, axis=-1, keepdims=True)
            idx = jnp.min(jnp.where(gate == mx, colf, float(LANES)), axis=-1, keepdims=True)
            hit = colf == idx
            mask = jnp.where(hit, 0.0, mask)
            gate = jnp.where(hit, neg_inf, gate)
        qa_scr[g0:, hd:] = (qx_ref[0, g0:, :].astype(F32) + mask).astype(BF16)

    for slot, t0 in enumerate(range(0, nb, MOBA_TILE_GROUP)):
        slot = slot % 2
        tiles = range(t0, min(t0 + MOBA_TILE_GROUP, nb))
        q0 = t0 * kb
        rows = len(tiles) * kb
        width = (tiles[-1] + 1) * kb
        s_scr[slot, 0:rows, 0:width] = lax.dot_general(qa_scr[q0:q0 + rows, :], ka_scr[0:width, :], nt,
                                                       preferred_element_type=F32)
        for own in tiles:
            base = (own - t0) * kb
            for rc in range(kb // rc_rows):
                r0, r1 = base + rc * rc_rows, base + (rc + 1) * rc_rows
                parts = [s_scr[slot, r0:r1, n * kb:(n + 1) * kb] for n in range(own)]
                parts.append(s_scr[slot, r0:r1, own * kb:(own + 1) * kb] + causal_scr[r0 - base:r1 - base, :])
                m = jnp.max(functools.reduce(jnp.maximum, parts), axis=-1, keepdims=True)
                for n in range(own + 1):
                    p_scr[slot, r0:r1, n * kb:(n + 1) * kb] = jnp.exp2(parts[n] - m).astype(BF16)
            if (own + 1) * kb < width:
                p_scr[slot, base:base + kb, (own + 1) * kb:width] = jnp.zeros((kb, width - (own + 1) * kb), BF16)

        res = jnp.dot(p_scr[slot, 0:rows, 0:width], va_scr[0:width, :], preferred_element_type=F32)
        o_ref[0, q0:q0 + rows, :] = (res[:, :hd] / res[:, hd:]).astype(o_ref.dtype)


def _moba_attention(qkv3d, d_model):
    b, l, _ = qkv3d.shape
    n_heads = d_model // HEAD_DIM
    kb = KEY_BLOCK
    qx, kx = _moba_tables(n_heads, l)
    return pl.pallas_call(
        _moba_kernel,
        grid=(b, n_heads),
        in_specs=[
            pl.BlockSpec((1, l, HEAD_DIM), lambda i, h: (i, 0, h)),
            pl.BlockSpec((1, l, HEAD_DIM), lambda i, h: (i, 0, n_heads + h)),
            pl.BlockSpec((1, l, HEAD_DIM), lambda i, h: (i, 0, 2 * n_heads + h)),
            pl.BlockSpec((1, l, LANES), lambda i, h: (h, 0, 0)),
            pl.BlockSpec((1, l, LANES), lambda i, h: (h, 0, 0)),
        ],
        out_specs=pl.BlockSpec((1, l, HEAD_DIM), lambda i, h: (i, 0, h)),
        out_shape=jax.ShapeDtypeStruct((b, l, d_model), BF16),
        scratch_shapes=[
            pltpu.VMEM((LANES, HEAD_DIM), F32),
            pltpu.VMEM((kb, kb), F32),
            pltpu.VMEM((l, HEAD_DIM + LANES), BF16),
            pltpu.VMEM((l, HEAD_DIM + LANES), BF16),
            pltpu.VMEM((l, 2 * HEAD_DIM), BF16),
            pltpu.VMEM((2, MOBA_TILE_GROUP * kb, l), F32),
            pltpu.VMEM((2, MOBA_TILE_GROUP * kb, l), BF16),
        ],
        compiler_params=_cparams(("parallel", "parallel")),
        name="moba_attention",
    )(qkv3d, qkv3d, qkv3d, qx, kx)


def kernel(x, g_mix, lambda_re, lambda_im, log_dt, b_re, b_im, c_re, c_im, d_skip, w_glu, b_glu, g_kv, w_kv,
           w_q, w_o, g_ffn, w_gate, w_up, w_down, g_final):
    bsz, l, d = x.shape
    depth = g_mix.shape[0]
    n_a = lambda_re.shape[0]
    m = bsz * l
    assert l % KEY_BLOCK == 0 and d % HEAD_DIM == 0 and l // KEY_BLOCK <= LANES
    row = lambda v: v.reshape(1, -1).astype(F32)

    h = x.astype(F32).reshape(m, d)
    qkv = None
    for i in range(depth):
        if i < n_a:
            tables = _s5_tables(lambda_re[i], lambda_im[i], log_dt[i], b_re[i], b_im[i], c_re[i], c_im[i])
            u = _rmsnorm_interleaved(h, row(g_mix[i]), bsz)
            z = _s5_mixer(u, row(d_skip[i]), *tables)
            h = _glu_residual(z, w_glu, i, row(b_glu[i]), h)
        else:
            j = i - n_a
            qkv_i = _qkv_proj(h, row(g_mix[i]), row(g_kv), w_q, j, w_kv, HEAD_DIM ** -0.5 * LOG2_E)
            if qkv is None:
                qkv = qkv_i
            else:
                qkv = jnp.concatenate([qkv_i[:, :d], qkv[:, d:]], axis=1)
            attn = _moba_attention(qkv.reshape(bsz, l, 3 * d), d)
            h = _proj_residual(attn.reshape(m, d), w_o, j, h)
        h = _ffn_residual(h, row(g_ffn[i]), w_gate, w_up, w_down, i, row(g_final), final_norm=(i == depth - 1))
    return h.reshape(bsz, l, d).astype(x.dtype)
```

```python
import functools
import math

import jax
import jax.numpy as jnp
import numpy as np
from jax import lax
from jax.experimental import pallas as pl
from jax.experimental.pallas import tpu as pltpu

F32 = jnp.float32
BF16 = jnp.bfloat16

EPS = 1e-6
LANES = 128
HEAD_DIM = 128
KEY_BLOCK = 256
TOP_K_BLOCKS = 3
MOBA_ROW_CHUNK = 16
MOBA_TILE_GROUP = 2
MOBA_BIAS_TERMS = 4
LOG2_E = math.log2(math.e)
MASK_BIG = -2.0 ** 100
SSM_CHUNK = 8
S5_GROUP = 16
VMEM_LIMIT = 56 * 1024 * 1024


def _cparams(sem):
    return pltpu.CompilerParams(dimension_semantics=sem, vmem_limit_bytes=VMEM_LIMIT)


def _rms_scale(x):
    return lax.rsqrt(jnp.mean(x * x, axis=-1, keepdims=True) + EPS)


def _rmsnorm_kernel(x_ref, g_ref, o_ref):
    x = x_ref[...]
    o_ref[...] = (x * _rms_scale(x) * g_ref[...]).reshape(o_ref.shape)


def _rmsnorm_interleaved(x2d, g_row, bsz, tm=512):
    m, d = x2d.shape
    tiles = m // bsz // tm
    return pl.pallas_call(
        _rmsnorm_kernel,
        grid=(bsz, tiles),
        in_specs=[pl.BlockSpec((tm, d), lambda b, i: (b * tiles + i, 0)), pl.BlockSpec((1, d), lambda b, i: (0, 0))],
        out_specs=pl.BlockSpec((tm // S5_GROUP, None, S5_GROUP, d), lambda b, i: (i, b, 0, 0)),
        out_shape=jax.ShapeDtypeStruct((m // bsz // S5_GROUP, bsz, S5_GROUP, d), F32),
        compiler_params=_cparams(("parallel", "parallel")),
        name="rmsnorm",
    )(x2d, g_row)


def _s5_tables(lam_re, lam_im, log_dt, b_re, b_im, c_re, c_im):
    g, p, c = b_re.shape
    t = SSM_CHUNK
    gl = LANES // c
    nj = g // gl
    ns = gl * p
    lr, li = lam_re.astype(F32), lam_im.astype(F32)
    dt = jnp.exp(log_dt.astype(F32))[:, None]
    ar, ai = lr * dt, li * dt
    mag = jnp.exp(ar)
    nr, ni = mag * jnp.cos(ai) - 1.0, mag * jnp.sin(ai)
    den = lr * lr + li * li
    fr, fi = (nr * lr + ni * li) / den, (ni * lr - nr * li) / den
    br, bi = b_re.astype(F32), b_im.astype(F32)
    bbr = fr[..., None] * br - fi[..., None] * bi
    bbi = fr[..., None] * bi + fi[..., None] * br
    dd = jnp.arange(t + 1, dtype=F32)[:, None, None]
    pmag = jnp.exp(ar[None] * dd)
    pr, pi = pmag * jnp.cos(ai[None] * dd), pmag * jnp.sin(ai[None] * dd)

    def block_diag(x):
        x = jnp.tile(x.reshape(nj, gl * c, p), (1, 1, gl))
        rg = lax.broadcasted_iota(jnp.int32, (gl * c, ns), 0) // c
        cg = lax.broadcasted_iota(jnp.int32, (gl * c, ns), 1) // p
        return jnp.where(rg == cg, x, 0.0)

    bblk = jnp.concatenate([block_diag(jnp.swapaxes(bbr, 1, 2)), block_diag(jnp.swapaxes(bbi, 1, 2))], axis=-1)
    cblk = jnp.concatenate([block_diag(c_re.astype(F32)), block_diag(c_im.astype(F32))], axis=-1)
    pw = jnp.concatenate([pr.reshape(t + 1, nj, ns), pi.reshape(t + 1, nj, ns)], axis=-1)
    pw = jnp.pad(jnp.swapaxes(pw, 0, 1), ((0, 0), (0, 16 - (t + 1)), (0, 0)))
    return bblk, cblk, pw


def _s5_build_tables(bblk_ref, cblk_ref, pw_ref, w1_scr, w2_scr):
    t = SSM_CHUNK
    ns = bblk_ref.shape[2] // 2
    yw = t * LANES
    bcat = bblk_ref[0]
    bre, bim = bcat[:, :ns], bcat[:, ns:]
    bcat16 = bcat.astype(BF16)
    cre, cim = cblk_ref[0, :, :ns], cblk_ref[0, :, ns:]
    nt = (((1,), (1,)), ((), ()))
    for d in range(t + 1):
        pr, pi = pw_ref[0, d:d + 1, :ns], pw_ref[0, d:d + 1, ns:]
        mo = jnp.concatenate([cre * pr - cim * pi, -(cre * pi + cim * pr)], axis=1).astype(BF16)
        if d >= 1:
            w2_scr[(d - 1) * LANES:d * LANES, :] = mo
        if d < t:
            kd = lax.dot_general(bcat16, mo, nt, preferred_element_type=F32).astype(BF16)
            for ip in range(t - d):
                w1_scr[ip * LANES:(ip + 1) * LANES, (ip + d) * LANES:(ip + d + 1) * LANES] = kd
            ip = t - 1 - d
            w1_scr[ip * LANES:(ip + 1) * LANES, yw:yw + ns] = (bre * pr - bim * pi).astype(BF16)
            w1_scr[ip * LANES:(ip + 1) * LANES, yw + ns:] = (bre * pi + bim * pr).astype(BF16)
    for ip in range(1, t):
        w1_scr[ip * LANES:(ip + 1) * LANES, 0:ip * LANES] = jnp.zeros((LANES, ip * LANES), BF16)


def _s5_kernel(u_ref, d_ref, bblk_ref, cblk_ref, pw_ref, z_ref, w1_scr, w2_scr, lhs_scr, s_scr, yc_scr, y_scr,
               *, n_seq):
    t = SSM_CHUNK
    cpg = S5_GROUP // t
    rows = lhs_scr.shape[0]
    ngroups = rows // (n_seq * cpg)
    nsl = s_scr.shape[0] // 2
    yw = t * LANES

    _s5_build_tables(bblk_ref, cblk_ref, pw_ref, w1_scr, w2_scr)

    for i in range(t):
        lhs_scr[:, i * LANES:(i + 1) * LANES] = u_ref[pl.ds(i, rows, stride=t), :].astype(BF16)
    strip = 2 * LANES
    for c0 in range(0, yw, strip):
        c1 = c0 + strip
        yc_scr[:, c0:c1] = jnp.dot(lhs_scr[:, :c1], w1_scr[:c1, c0:c1], preferred_element_type=F32)
    s_in_all = jnp.dot(lhs_scr[...], w1_scr[:, yw:], preferred_element_type=F32)
    for q in range(2 * nsl):
        s_scr[q] = s_in_all[:, q * LANES:(q + 1) * LANES]

    a_re = [pw_ref[0, t:t + 1, q * LANES:(q + 1) * LANES] for q in range(nsl)]
    a_im = [pw_ref[0, t:t + 1, (nsl + q) * LANES:(nsl + q + 1) * LANES] for q in range(nsl)]

    def scan_group(g, carry):
        e = list(carry)
        for c in range(cpg):
            sel = pl.ds(g * (n_seq * cpg) + c, n_seq, stride=cpg)
            for q in range(nsl):
                e_re, e_im = e[q], e[nsl + q]
                in_re, in_im = s_scr[q, sel, :], s_scr[nsl + q, sel, :]
                s_scr[q, sel, :] = e_re
                s_scr[nsl + q, sel, :] = e_im
                e[q] = a_re[q] * e_re - a_im[q] * e_im + in_re
                e[nsl + q] = a_re[q] * e_im + a_im[q] * e_re + in_im
        return tuple(e)

    lax.fori_loop(0, ngroups, scan_group, (jnp.zeros((n_seq, LANES), F32),) * (2 * nsl))

    s_prev = jnp.concatenate([s_scr[q].astype(BF16) for q in range(2 * nsl)], axis=1)
    y = yc_scr[...] + lax.dot_general(s_prev, w2_scr[...], (((1,), (1,)), ((), ())), preferred_element_type=F32)
    for i in range(t):
        y_scr[pl.ds(i, rows, stride=t), :] = y[:, i * LANES:(i + 1) * LANES]

    yy = y_scr[...] + d_ref[...] * u_ref[...]
    z_ref[...] = jax.nn.gelu(yy).astype(z_ref.dtype)


def _s5_mixer(u4d, d_row, bblk, cblk, pw):
    ngr, bsz, grp, d = u4d.shape
    m = ngr * bsz * grp
    nj = d // LANES
    rows = m // SSM_CHUNK
    ns2 = bblk.shape[2]
    yw = SSM_CHUNK * LANES
    z = pl.pallas_call(
        functools.partial(_s5_kernel, n_seq=bsz),
        grid=(nj,),
        in_specs=[
            pl.BlockSpec((m, LANES), lambda j: (0, j)),
            pl.BlockSpec((1, LANES), lambda j: (0, j)),
            pl.BlockSpec((1, LANES, ns2), lambda j: (j, 0, 0)),
            pl.BlockSpec((1, LANES, ns2), lambda j: (j, 0, 0)),
            pl.BlockSpec((1, 16, ns2), lambda j: (j, 0, 0)),
        ],
        out_specs=pl.BlockSpec((m, LANES), lambda j: (0, j)),
        out_shape=jax.ShapeDtypeStruct((m, d), BF16),
        scratch_shapes=[
            pltpu.VMEM((yw, yw + ns2), BF16),
            pltpu.VMEM((yw, ns2), BF16),
            pltpu.VMEM((rows, yw), BF16),
            pltpu.VMEM((ns2 // LANES, rows, LANES), F32),
            pltpu.VMEM((rows, yw), F32),
            pltpu.VMEM((m, LANES), F32),
        ],
        compiler_params=_cparams(("parallel",)),
        name="s5_mixer",
    )(u4d.reshape(m, d), d_row, bblk, cblk, pw)
    return z.reshape(ngr, bsz, grp, d)


def _glu_kernel(z_ref, w_ref, b_ref, res_ref, o_ref, w16_scr):
    tm, n = o_ref.shape

    @pl.when((pl.program_id(0) == 0) & (pl.program_id(1) == 0))
    def _():
        w16_scr[...] = w_ref[...].astype(BF16)

    z = z_ref[...].reshape(tm, n)
    a = jnp.dot(z, w16_scr[...], preferred_element_type=F32) + b_ref[...]
    o_ref[...] = res_ref[...] + z.astype(F32) * jax.nn.sigmoid(a)


def _glu_residual(z4d, w, layer, b_row, res, tm=512):
    ngr, bsz, grp, k = z4d.shape
    m = ngr * bsz * grp
    n = w.shape[2]
    assert k == n and (ngr * grp) % tm == 0 and tm % grp == 0
    tiles = ngr * grp // tm
    return pl.pallas_call(
        _glu_kernel,
        grid=(bsz, tiles),
        in_specs=[
            pl.BlockSpec((tm // grp, None, grp, k), lambda b, i: (i, b, 0, 0)),
            pl.BlockSpec((None, k, n), lambda b, i: (layer, 0, 0), pipeline_mode=pl.Buffered(1)),
            pl.BlockSpec((1, n), lambda b, i: (0, 0)),
            pl.BlockSpec((tm, n), lambda b, i: (b * tiles + i, 0)),
        ],
        out_specs=pl.BlockSpec((tm, n), lambda b, i: (b * tiles + i, 0)),
        out_shape=jax.ShapeDtypeStruct((m, n), F32),
        scratch_shapes=[pltpu.VMEM((k, n), BF16)],
        compiler_params=_cparams(("arbitrary", "arbitrary")),
        name="glu_residual",
    )(z4d, w, b_row, res)


def _proj_res_kernel(x_ref, w_ref, res_ref, o_ref, w16_scr):
    @pl.when(pl.program_id(0) == 0)
    def _():
        w16_scr[...] = w_ref[...].astype(BF16)

    o_ref[...] = res_ref[...] + jnp.dot(x_ref[...], w16_scr[...], preferred_element_type=F32)


def _proj_residual(x, w, layer, res, tm=512):
    m, k = x.shape
    n = w.shape[2]
    assert m % tm == 0
    return pl.pallas_call(
        _proj_res_kernel,
        grid=(m // tm,),
        in_specs=[
            pl.BlockSpec((tm, k), lambda i: (i, 0)),
            pl.BlockSpec((None, k, n), lambda i: (layer, 0, 0), pipeline_mode=pl.Buffered(1)),
            pl.BlockSpec((tm, n), lambda i: (i, 0)),
        ],
        out_specs=pl.BlockSpec((tm, n), lambda i: (i, 0)),
        out_shape=jax.ShapeDtypeStruct((m, n), F32),
        scratch_shapes=[pltpu.VMEM((k, n), BF16)],
        compiler_params=_cparams(("arbitrary",)),
        name="proj_residual",
    )(x, w, res)


def _qkv_kernel(x_ref, gq_ref, gkv_ref, wq_ref, wkv_ref, o_ref, xn_scr, *, n_q_tiles, q_scale):
    j = pl.program_id(1)

    @pl.when(j == 0)
    def _():
        x = x_ref[...]
        xn_scr[...] = (x * _rms_scale(x)).astype(BF16)

    @pl.when(j < n_q_tiles)
    def _():
        q = jnp.dot(xn_scr[...], (wq_ref[...] * gq_ref[...]).astype(BF16), preferred_element_type=F32)
        o_ref[...] = (q * q_scale).astype(o_ref.dtype)

    @pl.when(j >= n_q_tiles)
    def _():
        kv = jnp.dot(xn_scr[...], (wkv_ref[...] * gkv_ref[...]).astype(BF16), preferred_element_type=F32)
        o_ref[...] = kv.astype(o_ref.dtype)


def _qkv_proj(h2d, gq_col, gkv_col, w_q, layer, w_kv, q_scale, tm=1024, tn=512):
    m, k = h2d.shape
    nq = w_q.shape[2] // tn
    nkv = w_kv.shape[1] // tn
    return pl.pallas_call(
        functools.partial(_qkv_kernel, n_q_tiles=nq, q_scale=q_scale),
        grid=(m // tm, nq + nkv),
        in_specs=[
            pl.BlockSpec((tm, k), lambda i, j: (i, 0)),
            pl.BlockSpec((k, 1), lambda i, j: (0, 0)),
            pl.BlockSpec((k, 1), lambda i, j: (0, 0)),
            pl.BlockSpec((None, k, tn), lambda i, j: (layer, 0, jnp.minimum(j, nq - 1))),
            pl.BlockSpec((k, tn), lambda i, j: (0, jnp.maximum(j - nq, 0))),
        ],
        out_specs=pl.BlockSpec((tm, tn), lambda i, j: (i, j)),
        out_shape=jax.ShapeDtypeStruct((m, (nq + nkv) * tn), BF16),
        scratch_shapes=[pltpu.VMEM((tm, k), BF16)],
        compiler_params=_cparams(("parallel", "arbitrary")),
        name="qkv_proj",
    )(h2d, gq_col, gkv_col, w_q, w_kv)


def _ffn_kernel(h_ref, g_ref, wg_ref, wu_ref, wd_ref, gfin_ref, o_ref, u_scr, *, final_norm):
    f = pl.program_id(1)

    @pl.when(f == 0)
    def _():
        x = h_ref[...]
        u_scr[...] = (x * _rms_scale(x) * g_ref[...]).astype(BF16)
        o_ref[...] = x

    u = u_scr[...]
    gate = jnp.dot(u, wg_ref[...].astype(BF16), preferred_element_type=F32)
    up = jnp.dot(u, wu_ref[...].astype(BF16), preferred_element_type=F32)
    act = (jax.nn.silu(gate) * up).astype(BF16)
    o_ref[...] += jnp.dot(act, wd_ref[...].astype(BF16), preferred_element_type=F32)

    if final_norm:
        @pl.when(f == pl.num_programs(1) - 1)
        def _():
            y = o_ref[...]
            o_ref[...] = y * _rms_scale(y) * gfin_ref[...]


def _ffn_residual(h2d, g_row, w_gate, w_up, w_down, layer, gfin_row, final_norm, tm=1024, tf=256):
    m, d = h2d.shape
    ff = w_gate.shape[2]
    return pl.pallas_call(
        functools.partial(_ffn_kernel, final_norm=final_norm),
        grid=(m // tm, ff // tf),
        in_specs=[
            pl.BlockSpec((tm, d), lambda i, f: (i, 0)),
            pl.BlockSpec((1, d), lambda i, f: (0, 0)),
            pl.BlockSpec((None, d, tf), lambda i, f: (layer, 0, f)),
            pl.BlockSpec((None, d, tf), lambda i, f: (layer, 0, f)),
            pl.BlockSpec((None, tf, d), lambda i, f: (layer, f, 0)),
            pl.BlockSpec((1, d), lambda i, f: (0, 0)),
        ],
        out_specs=pl.BlockSpec((tm, d), lambda i, f: (i, 0)),
        out_shape=jax.ShapeDtypeStruct((m, d), F32),
        scratch_shapes=[pltpu.VMEM((tm, d), BF16)],
        compiler_params=_cparams(("parallel", "arbitrary")),
        name="ffn_residual",
    )(h2d, g_row, w_gate, w_up, w_down, gfin_row)


def _moba_tables(n_heads, seq_len):
    nb = seq_len // KEY_BLOCK
    nt = MOBA_BIAS_TERMS
    assert nb + 2 * nt <= LANES
    slopes = np.exp2(-8.0 * np.arange(1, n_heads + 1, dtype=np.float64) / n_heads) * LOG2_E
    bias = slopes[:, None] * np.arange(seq_len, dtype=np.float64)[None, :]
    top = 2.0 ** math.ceil(math.log2(LOG2_E * seq_len))

    terms, rest, quantum = [], bias, top / 256.0
    for _ in range(nt - 1):
        term = np.floor(rest / quantum) * quantum
        terms.append(term)
        rest = rest - term
        quantum = quantum / 256.0
    terms.append(rest)

    qx = np.zeros((n_heads, seq_len, LANES), np.float64)
    kx = np.zeros((n_heads, seq_len, LANES), np.float64)
    kx[:, np.arange(seq_len), np.arange(seq_len) // KEY_BLOCK] = 1.0
    for c, term in enumerate(terms):
        qx[:, :, nb + c] = -term
        qx[:, :, nb + nt + c] = 1.0
        kx[:, :, nb + c] = 1.0
        kx[:, :, nb + nt + c] = term
    return jnp.asarray(qx.astype(jnp.bfloat16)), jnp.asarray(kx.astype(jnp.bfloat16))


def _moba_kernel(q_ref, k_ref, v_ref, qx_ref, kx_ref, o_ref, kmean_scr, causal_scr, qa_scr, ka_scr, va_scr,
                 s_scr, p_scr):
    kb = KEY_BLOCK
    hd = HEAD_DIM
    rc_rows = MOBA_ROW_CHUNK
    seq = k_ref.shape[1]
    nb = seq // kb
    neg_inf = -jnp.inf
    nt = (((1,), (1,)), ((), ()))

    qa_scr[:, :hd] = q_ref[0]
    ka_scr[:, :hd] = k_ref[0]
    ka_scr[:, hd:] = kx_ref[0]
    va_scr[:, :hd] = v_ref[0]
    va_scr[:, hd:] = jnp.ones((seq, hd), BF16)
    kmean_scr[...] = jnp.zeros_like(kmean_scr)
    for n in range(nb):
        kmean_scr[n:n + 1, :] = jnp.mean(k_ref[0, n * kb:(n + 1) * kb, :].astype(F32), axis=0, keepdims=True)
    row = lax.broadcasted_iota(jnp.int32, (kb, kb), 0)
    colk = lax.broadcasted_iota(jnp.int32, (kb, kb), 1)
    causal_scr[...] = jnp.where(row >= colk, 0.0, neg_inf)

    g0 = min((TOP_K_BLOCKS + 1) * kb, seq)
    qa_scr[:g0, hd:] = qx_ref[0, :g0, :]
    if seq > g0:
        ng = seq - g0
        gate = lax.dot_general(q_ref[0, g0:, :].astype(F32), kmean_scr[...], nt,
                               precision=lax.Precision.HIGHEST, preferred_element_type=F32)
        col = lax.broadcasted_iota(jnp.int32, (ng, LANES), 1)
        colf = col.astype(F32)
        own_v = lax.div(lax.broadcasted_iota(jnp.int32, (ng, LANES), 0), jnp.int32(kb)) + (TOP_K_BLOCKS + 1)
        past = col < own_v
        gate = jnp.where(past, gate, neg_inf)
        mask = jnp.where(past, MASK_BIG, 0.0)
        for _ in range(TOP_K_BLOCKS):
            mx = jnp.max(gate, axis=-1, keepdims=True)
            idx = jnp.min(jnp.where(gate == mx, colf, float(LANES)), axis=-1, keepdims=True)
            hit = colf == idx
            mask = jnp.where(hit, 0.0, mask)
            gate = jnp.where(hit, neg_inf, gate)
        qa_scr[g0:, hd:] = (qx_ref[0, g0:, :].astype(F32) + mask).astype(BF16)

    for slot, t0 in enumerate(range(0, nb, MOBA_TILE_GROUP)):
        slot = slot % 2
        tiles = range(t0, min(t0 + MOBA_TILE_GROUP, nb))
        q0 = t0 * kb
        rows = len(tiles) * kb
        width = (tiles[-1] + 1) * kb
        s_scr[slot, 0:rows, 0:width] = lax.dot_general(qa_scr[q0:q0 + rows, :], ka_scr[0:width, :], nt,
                                                       preferred_element_type=F32)
        for own in tiles:
            base = (own - t0) * kb
            for rc in range(kb // rc_rows):
                r0, r1 = base + rc * rc_rows, base + (rc + 1) * rc_rows
                parts = [s_scr[slot, r0:r1, n * kb:(n + 1) * kb] for n in range(own)]
                parts.append(s_scr[slot, r0:r1, own * kb:(own + 1) * kb] + causal_scr[r0 - base:r1 - base, :])
                m = jnp.max(functools.reduce(jnp.maximum, parts), axis=-1, keepdims=True)
                for n in range(own + 1):
                    p_scr[slot, r0:r1, n * kb:(n + 1) * kb] = jnp.exp2(parts[n] - m).astype(BF16)
            if (own + 1) * kb < width:
                p_scr[slot, base:base + kb, (own + 1) * kb:width] = jnp.zeros((kb, width - (own + 1) * kb), BF16)

        res = jnp.dot(p_scr[slot, 0:rows, 0:width], va_scr[0:width, :], preferred_element_type=F32)
        o_ref[0, q0:q0 + rows, :] = (res[:, :hd] / res[:, hd:]).astype(o_ref.dtype)


def _moba_attention(qkv3d, d_model):
    b, l, _ = qkv3d.shape
    n_heads = d_model // HEAD_DIM
    kb = KEY_BLOCK
    qx, kx = _moba_tables(n_heads, l)
    return pl.pallas_call(
        _moba_kernel,
        grid=(b, n_heads),
        in_specs=[
            pl.BlockSpec((1, l, HEAD_DIM), lambda i, h: (i, 0, h)),
            pl.BlockSpec((1, l, HEAD_DIM), lambda i, h: (i, 0, n_heads + h)),
            pl.BlockSpec((1, l, HEAD_DIM), lambda i, h: (i, 0, 2 * n_heads + h)),
            pl.BlockSpec((1, l, LANES), lambda i, h: (h, 0, 0)),
            pl.BlockSpec((1, l, LANES), lambda i, h: (h, 0, 0)),
        ],
        out_specs=pl.BlockSpec((1, l, HEAD_DIM), lambda i, h: (i, 0, h)),
        out_shape=jax.ShapeDtypeStruct((b, l, d_model), BF16),
        scratch_shapes=[
            pltpu.VMEM((LANES, HEAD_DIM), F32),
            pltpu.VMEM((kb, kb), F32),
            pltpu.VMEM((l, HEAD_DIM + LANES), BF16),
            pltpu.VMEM((l, HEAD_DIM + LANES), BF16),
            pltpu.VMEM((l, 2 * HEAD_DIM), BF16),
            pltpu.VMEM((2, MOBA_TILE_GROUP * kb, l), F32),
            pltpu.VMEM((2, MOBA_TILE_GROUP * kb, l), BF16),
        ],
        compiler_params=_cparams(("parallel", "parallel")),
        name="moba_attention",
    )(qkv3d, qkv3d, qkv3d, qx, kx)


def kernel(x, g_mix, lambda_re, lambda_im, log_dt, b_re, b_im, c_re, c_im, d_skip, w_glu, b_glu, g_kv, w_kv,
           w_q, w_o, g_ffn, w_gate, w_up, w_down, g_final):
    bsz, l, d = x.shape
    depth = g_mix.shape[0]
    n_a = lambda_re.shape[0]
    m = bsz * l
    assert l % KEY_BLOCK == 0 and d % HEAD_DIM == 0 and l // KEY_BLOCK <= LANES
    row = lambda v: v.reshape(1, -1).astype(F32)

    h = x.astype(F32).reshape(m, d)
    qkv = None
    for i in range(depth):
        if i < n_a:
            tables = _s5_tables(lambda_re[i], lambda_im[i], log_dt[i], b_re[i], b_im[i], c_re[i], c_im[i])
            u = _rmsnorm_interleaved(h, row(g_mix[i]), bsz)
            z = _s5_mixer(u, row(d_skip[i]), *tables)
            h = _glu_residual(z, w_glu, i, row(b_glu[i]), h)
        else:
            j = i - n_a
            col = lambda v: v.reshape(-1, 1).astype(F32)
            qkv_i = _qkv_proj(h, col(g_mix[i]), col(g_kv), w_q, j, w_kv, HEAD_DIM ** -0.5 * LOG2_E)
            if qkv is None:
                qkv = qkv_i
            else:
                qkv = jnp.concatenate([qkv_i[:, :d], qkv[:, d:]], axis=1)
            attn = _moba_attention(qkv.reshape(bsz, l, 3 * d), d)
            h = _proj_residual(attn.reshape(m, d), w_o, j, h)
        h = _ffn_residual(h, row(g_ffn[i]), w_gate, w_up, w_down, i, row(g_final), final_norm=(i == depth - 1))
    return h.reshape(bsz, l, d).astype(x.dtype)
```

```python
import functools
import math

import jax
import jax.numpy as jnp
import numpy as np
from jax import lax
from jax.experimental import pallas as pl
from jax.experimental.pallas import tpu as pltpu

F32 = jnp.float32
BF16 = jnp.bfloat16

EPS = 1e-6
LANES = 128
HEAD_DIM = 128
KEY_BLOCK = 256
TOP_K_BLOCKS = 3
MOBA_ROW_CHUNK = 16
MOBA_TILE_GROUP = 2
MOBA_BIAS_TERMS = 4
LOG2_E = math.log2(math.e)
MASK_BIG = -2.0 ** 100
SSM_CHUNK = 8
S5_GROUP = 16
VMEM_LIMIT = 56 * 1024 * 1024


def _cparams(sem):
    return pltpu.CompilerParams(dimension_semantics=sem, vmem_limit_bytes=VMEM_LIMIT)


def _rms_scale(x):
    return lax.rsqrt(jnp.mean(x * x, axis=-1, keepdims=True) + EPS)


def _rmsnorm_kernel(x_ref, g_ref, o_ref):
    x = x_ref[...]
    o_ref[...] = (x * _rms_scale(x) * g_ref[...]).reshape(o_ref.shape)


def _rmsnorm_interleaved(x2d, g_row, bsz, tm=512):
    m, d = x2d.shape
    tiles = m // bsz // tm
    return pl.pallas_call(
        _rmsnorm_kernel,
        grid=(bsz, tiles),
        in_specs=[pl.BlockSpec((tm, d), lambda b, i: (b * tiles + i, 0)), pl.BlockSpec((1, d), lambda b, i: (0, 0))],
        out_specs=pl.BlockSpec((tm // S5_GROUP, None, S5_GROUP, d), lambda b, i: (i, b, 0, 0)),
        out_shape=jax.ShapeDtypeStruct((m // bsz // S5_GROUP, bsz, S5_GROUP, d), F32),
        compiler_params=_cparams(("parallel", "parallel")),
        name="rmsnorm",
    )(x2d, g_row)


def _s5_tables(lam_re, lam_im, log_dt, b_re, b_im, c_re, c_im):
    g, p, c = b_re.shape
    t = SSM_CHUNK
    gl = LANES // c
    nj = g // gl
    ns = gl * p
    lr, li = lam_re.astype(F32), lam_im.astype(F32)
    dt = jnp.exp(log_dt.astype(F32))[:, None]
    ar, ai = lr * dt, li * dt
    mag = jnp.exp(ar)
    nr, ni = mag * jnp.cos(ai) - 1.0, mag * jnp.sin(ai)
    den = lr * lr + li * li
    fr, fi = (nr * lr + ni * li) / den, (ni * lr - nr * li) / den
    br, bi = b_re.astype(F32), b_im.astype(F32)
    bbr = fr[..., None] * br - fi[..., None] * bi
    bbi = fr[..., None] * bi + fi[..., None] * br
    dd = jnp.arange(t + 1, dtype=F32)[:, None, None]
    pmag = jnp.exp(ar[None] * dd)
    pr, pi = pmag * jnp.cos(ai[None] * dd), pmag * jnp.sin(ai[None] * dd)

    def per_tile(x):
        return jnp.transpose(x.reshape(nj, gl, c, p), (0, 2, 1, 3)).reshape(nj, c, ns)

    bcmp = jnp.concatenate([per_tile(jnp.swapaxes(bbr, 1, 2)), per_tile(jnp.swapaxes(bbi, 1, 2))], axis=-1)
    ccmp = jnp.concatenate([per_tile(c_re.astype(F32)), per_tile(c_im.astype(F32))], axis=-1)
    pw = jnp.concatenate([pr.reshape(t + 1, nj, ns), pi.reshape(t + 1, nj, ns)], axis=-1)
    pw = jnp.pad(jnp.swapaxes(pw, 0, 1), ((0, 0), (0, 16 - (t + 1)), (0, 0)))
    return bcmp, ccmp, pw


def _s5_build_tables(bcmp_ref, ccmp_ref, pw_ref, w1_scr, w2_scr):
    t = SSM_CHUNK
    cg = bcmp_ref.shape[1]
    ns = bcmp_ref.shape[2] // 2
    sg = ns * cg // LANES
    yw = t * LANES
    same_group = (lax.broadcasted_iota(jnp.int32, (LANES, ns), 0) // cg
                  == lax.broadcasted_iota(jnp.int32, (LANES, ns), 1) // sg)

    def block_diag(x):
        return jnp.where(same_group, jnp.tile(x, (LANES // cg, 1)), 0.0)

    bre, bim = bcmp_ref[0, :, :ns], bcmp_ref[0, :, ns:]
    cre, cim = ccmp_ref[0, :, :ns], ccmp_ref[0, :, ns:]
    bcat16 = jnp.concatenate([block_diag(bre), block_diag(bim)], axis=1).astype(BF16)
    nt = (((1,), (1,)), ((), ()))
    for d in range(t + 1):
        pr, pi = pw_ref[0, d:d + 1, :ns], pw_ref[0, d:d + 1, ns:]
        mo = jnp.concatenate([block_diag(cre * pr - cim * pi), block_diag(-(cre * pi + cim * pr))],
                             axis=1).astype(BF16)
        if d >= 1:
            w2_scr[(d - 1) * LANES:d * LANES, :] = mo
        if d < t:
            kd = lax.dot_general(bcat16, mo, nt, preferred_element_type=F32).astype(BF16)
            for ip in range(t - d):
                w1_scr[ip * LANES:(ip + 1) * LANES, (ip + d) * LANES:(ip + d + 1) * LANES] = kd
            ip = t - 1 - d
            w1_scr[ip * LANES:(ip + 1) * LANES, yw:yw + ns] = block_diag(bre * pr - bim * pi).astype(BF16)
            w1_scr[ip * LANES:(ip + 1) * LANES, yw + ns:] = block_diag(bre * pi + bim * pr).astype(BF16)
    for ip in range(1, t):
        w1_scr[ip * LANES:(ip + 1) * LANES, 0:ip * LANES] = jnp.zeros((LANES, ip * LANES), BF16)


def _s5_kernel(u_ref, d_ref, bcmp_ref, ccmp_ref, pw_ref, z_ref, w1_scr, w2_scr, lhs_scr, s_scr, yc_scr, y_scr,
               *, n_seq):
    t = SSM_CHUNK
    cpg = S5_GROUP // t
    rows = lhs_scr.shape[0]
    ngroups = rows // (n_seq * cpg)
    nsl = s_scr.shape[0] // 2
    yw = t * LANES

    _s5_build_tables(bcmp_ref, ccmp_ref, pw_ref, w1_scr, w2_scr)

    for i in range(t):
        lhs_scr[:, i * LANES:(i + 1) * LANES] = u_ref[pl.ds(i, rows, stride=t), :].astype(BF16)
    strip = 2 * LANES
    for c0 in range(0, yw, strip):
        c1 = c0 + strip
        yc_scr[:, c0:c1] = jnp.dot(lhs_scr[:, :c1], w1_scr[:c1, c0:c1], preferred_element_type=F32)
    s_in_all = jnp.dot(lhs_scr[...], w1_scr[:, yw:], preferred_element_type=F32)
    for q in range(2 * nsl):
        s_scr[q] = s_in_all[:, q * LANES:(q + 1) * LANES]

    a_re = [pw_ref[0, t:t + 1, q * LANES:(q + 1) * LANES] for q in range(nsl)]
    a_im = [pw_ref[0, t:t + 1, (nsl + q) * LANES:(nsl + q + 1) * LANES] for q in range(nsl)]

    def scan_group(g, carry):
        e = list(carry)
        for c in range(cpg):
            sel = pl.ds(g * (n_seq * cpg) + c, n_seq, stride=cpg)
            for q in range(nsl):
                e_re, e_im = e[q], e[nsl + q]
                in_re, in_im = s_scr[q, sel, :], s_scr[nsl + q, sel, :]
                s_scr[q, sel, :] = e_re
                s_scr[nsl + q, sel, :] = e_im
                e[q] = a_re[q] * e_re - a_im[q] * e_im + in_re
                e[nsl + q] = a_re[q] * e_im + a_im[q] * e_re + in_im
        return tuple(e)

    lax.fori_loop(0, ngroups, scan_group, (jnp.zeros((n_seq, LANES), F32),) * (2 * nsl))

    s_prev = jnp.concatenate([s_scr[q].astype(BF16) for q in range(2 * nsl)], axis=1)
    y = yc_scr[...] + lax.dot_general(s_prev, w2_scr[...], (((1,), (1,)), ((), ())), preferred_element_type=F32)
    for i in range(t):
        y_scr[pl.ds(i, rows, stride=t), :] = y[:, i * LANES:(i + 1) * LANES]

    yy = y_scr[...] + d_ref[...] * u_ref[...]
    z_ref[...] = jax.nn.gelu(yy).astype(z_ref.dtype)


def _s5_mixer(u4d, d_row, bcmp, ccmp, pw):
    ngr, bsz, grp, d = u4d.shape
    m = ngr * bsz * grp
    nj = d // LANES
    rows = m // SSM_CHUNK
    cg, ns2 = bcmp.shape[1:]
    yw = SSM_CHUNK * LANES
    z = pl.pallas_call(
        functools.partial(_s5_kernel, n_seq=bsz),
        grid=(nj,),
        in_specs=[
            pl.BlockSpec((m, LANES), lambda j: (0, j)),
            pl.BlockSpec((1, LANES), lambda j: (0, j)),
            pl.BlockSpec((1, cg, ns2), lambda j: (j, 0, 0)),
            pl.BlockSpec((1, cg, ns2), lambda j: (j, 0, 0)),
            pl.BlockSpec((1, 16, ns2), lambda j: (j, 0, 0)),
        ],
        out_specs=pl.BlockSpec((m, LANES), lambda j: (0, j)),
        out_shape=jax.ShapeDtypeStruct((m, d), BF16),
        scratch_shapes=[
            pltpu.VMEM((yw, yw + ns2), BF16),
            pltpu.VMEM((yw, ns2), BF16),
            pltpu.VMEM((rows, yw), BF16),
            pltpu.VMEM((ns2 // LANES, rows, LANES), F32),
            pltpu.VMEM((rows, yw), F32),
            pltpu.VMEM((m, LANES), F32),
        ],
        compiler_params=_cparams(("parallel",)),
        name="s5_mixer",
    )(u4d.reshape(m, d), d_row, bcmp, ccmp, pw)
    return z.reshape(ngr, bsz, grp, d)


def _glu_kernel(z_ref, w_ref, b_ref, res_ref, o_ref, w16_scr):
    tm, n = o_ref.shape

    @pl.when((pl.program_id(0) == 0) & (pl.program_id(1) == 0))
    def _():
        w16_scr[...] = w_ref[...].astype(BF16)

    z = z_ref[...].reshape(tm, n)
    a = jnp.dot(z, w16_scr[...], preferred_element_type=F32) + b_ref[...]
    o_ref[...] = res_ref[...] + z.astype(F32) * jax.nn.sigmoid(a)


def _glu_residual(z4d, w, layer, b_row, res, tm=512):
    ngr, bsz, grp, k = z4d.shape
    m = ngr * bsz * grp
    n = w.shape[2]
    assert k == n and (ngr * grp) % tm == 0 and tm % grp == 0
    tiles = ngr * grp // tm
    return pl.pallas_call(
        _glu_kernel,
        grid=(bsz, tiles),
        in_specs=[
            pl.BlockSpec((tm // grp, None, grp, k), lambda b, i: (i, b, 0, 0)),
            pl.BlockSpec((None, k, n), lambda b, i: (layer, 0, 0), pipeline_mode=pl.Buffered(1)),
            pl.BlockSpec((1, n), lambda b, i: (0, 0)),
            pl.BlockSpec((tm, n), lambda b, i: (b * tiles + i, 0)),
        ],
        out_specs=pl.BlockSpec((tm, n), lambda b, i: (b * tiles + i, 0)),
        out_shape=jax.ShapeDtypeStruct((m, n), F32),
        scratch_shapes=[pltpu.VMEM((k, n), BF16)],
        compiler_params=_cparams(("arbitrary", "arbitrary")),
        name="glu_residual",
    )(z4d, w, b_row, res)


def _proj_res_kernel(x_ref, w_ref, res_ref, o_ref, w16_scr):
    @pl.when(pl.program_id(0) == 0)
    def _():
        w16_scr[...] = w_ref[...].astype(BF16)

    o_ref[...] = res_ref[...] + jnp.dot(x_ref[...], w16_scr[...], preferred_element_type=F32)


def _proj_residual(x, w, layer, res, tm=512):
    m, k = x.shape
    n = w.shape[2]
    assert m % tm == 0
    return pl.pallas_call(
        _proj_res_kernel,
        grid=(m // tm,),
        in_specs=[
            pl.BlockSpec((tm, k), lambda i: (i, 0)),
            pl.BlockSpec((None, k, n), lambda i: (layer, 0, 0), pipeline_mode=pl.Buffered(1)),
            pl.BlockSpec((tm, n), lambda i: (i, 0)),
        ],
        out_specs=pl.BlockSpec((tm, n), lambda i: (i, 0)),
        out_shape=jax.ShapeDtypeStruct((m, n), F32),
        scratch_shapes=[pltpu.VMEM((k, n), BF16)],
        compiler_params=_cparams(("arbitrary",)),
        name="proj_residual",
    )(x, w, res)


def _qkv_kernel(x_ref, gq_ref, gkv_ref, wq_ref, wkv_ref, o_ref, xn_scr, *, n_q_tiles, q_scale):
    j = pl.program_id(1)

    @pl.when(j == 0)
    def _():
        x = x_ref[...]
        xn_scr[...] = (x * _rms_scale(x)).astype(BF16)

    @pl.when(j < n_q_tiles)
    def _():
        q = jnp.dot(xn_scr[...], (wq_ref[...] * gq_ref[...]).astype(BF16), preferred_element_type=F32)
        o_ref[...] = (q * q_scale).astype(o_ref.dtype)

    @pl.when(j >= n_q_tiles)
    def _():
        kv = jnp.dot(xn_scr[...], (wkv_ref[...] * gkv_ref[...]).astype(BF16), preferred_element_type=F32)
        o_ref[...] = kv.astype(o_ref.dtype)


def _qkv_proj(h2d, gq_col, gkv_col, w_q, layer, w_kv, q_scale, tm=1024, tn=512):
    m, k = h2d.shape
    nq = w_q.shape[2] // tn
    nkv = w_kv.shape[1] // tn
    return pl.pallas_call(
        functools.partial(_qkv_kernel, n_q_tiles=nq, q_scale=q_scale),
        grid=(m // tm, nq + nkv),
        in_specs=[
            pl.BlockSpec((tm, k), lambda i, j: (i, 0)),
            pl.BlockSpec((k, 1), lambda i, j: (0, 0)),
            pl.BlockSpec((k, 1), lambda i, j: (0, 0)),
            pl.BlockSpec((None, k, tn), lambda i, j: (layer, 0, jnp.minimum(j, nq - 1))),
            pl.BlockSpec((k, tn), lambda i, j: (0, jnp.maximum(j - nq, 0))),
        ],
        out_specs=pl.BlockSpec((tm, tn), lambda i, j: (i, j)),
        out_shape=jax.ShapeDtypeStruct((m, (nq + nkv) * tn), BF16),
        scratch_shapes=[pltpu.VMEM((tm, k), BF16)],
        compiler_params=_cparams(("parallel", "arbitrary")),
        name="qkv_proj",
    )(h2d, gq_col, gkv_col, w_q, w_kv)


def _ffn_kernel(h_ref, g_ref, wg_ref, wu_ref, wd_ref, gfin_ref, o_ref, u_scr, *, final_norm):
    f = pl.program_id(1)

    @pl.when(f == 0)
    def _():
        x = h_ref[...]
        u_scr[...] = (x * _rms_scale(x) * g_ref[...]).astype(BF16)
        o_ref[...] = x

    u = u_scr[...]
    gate = jnp.dot(u, wg_ref[...].astype(BF16), preferred_element_type=F32)
    up = jnp.dot(u, wu_ref[...].astype(BF16), preferred_element_type=F32)
    act = (jax.nn.silu(gate) * up).astype(BF16)
    o_ref[...] += jnp.dot(act, wd_ref[...].astype(BF16), preferred_element_type=F32)

    if final_norm:
        @pl.when(f == pl.num_programs(1) - 1)
        def _():
            y = o_ref[...]
            o_ref[...] = y * _rms_scale(y) * gfin_ref[...]


def _ffn_residual(h2d, g_row, w_gate, w_up, w_down, layer, gfin_row, final_norm, tm=1024, tf=256):
    m, d = h2d.shape
    ff = w_gate.shape[2]
    return pl.pallas_call(
        functools.partial(_ffn_kernel, final_norm=final_norm),
        grid=(m // tm, ff // tf),
        in_specs=[
            pl.BlockSpec((tm, d), lambda i, f: (i, 0)),
            pl.BlockSpec((1, d), lambda i, f: (0, 0)),
            pl.BlockSpec((None, d, tf), lambda i, f: (layer, 0, f)),
            pl.BlockSpec((None, d, tf), lambda i, f: (layer, 0, f)),
            pl.BlockSpec((None, tf, d), lambda i, f: (layer, f, 0)),
            pl.BlockSpec((1, d), lambda i, f: (0, 0)),
        ],
        out_specs=pl.BlockSpec((tm, d), lambda i, f: (i, 0)),
        out_shape=jax.ShapeDtypeStruct((m, d), F32),
        scratch_shapes=[pltpu.VMEM((tm, d), BF16)],
        compiler_params=_cparams(("parallel", "arbitrary")),
        name="ffn_residual",
    )(h2d, g_row, w_gate, w_up, w_down, gfin_row)


def _moba_tables(n_heads, seq_len):
    nb = seq_len // KEY_BLOCK
    nt = MOBA_BIAS_TERMS
    assert nb + 2 * nt <= LANES
    slopes = np.exp2(-8.0 * np.arange(1, n_heads + 1, dtype=np.float64) / n_heads) * LOG2_E
    bias = slopes[:, None] * np.arange(seq_len, dtype=np.float64)[None, :]
    top = 2.0 ** math.ceil(math.log2(LOG2_E * seq_len))

    terms, rest, quantum = [], bias, top / 256.0
    for _ in range(nt - 1):
        term = np.floor(rest / quantum) * quantum
        terms.append(term)
        rest = rest - term
        quantum = quantum / 256.0
    terms.append(rest)

    qx = np.zeros((n_heads, seq_len, LANES), np.float64)
    kx = np.zeros((n_heads, seq_len, LANES), np.float64)
    kx[:, np.arange(seq_len), np.arange(seq_len) // KEY_BLOCK] = 1.0
    for c, term in enumerate(terms):
        qx[:, :, nb + c] = -term
        qx[:, :, nb + nt + c] = 1.0
        kx[:, :, nb + c] = 1.0
        kx[:, :, nb + nt + c] = term
    return jnp.asarray(qx.astype(jnp.bfloat16)), jnp.asarray(kx.astype(jnp.bfloat16))


def _moba_kernel(q_ref, k_ref, v_ref, qx_ref, kx_ref, o_ref, kmean_scr, causal_scr, qa_scr, ka_scr, va_scr,
                 s_scr, p_scr):
    kb = KEY_BLOCK
    hd = HEAD_DIM
    rc_rows = MOBA_ROW_CHUNK
    seq = k_ref.shape[1]
    nb = seq // kb
    neg_inf = -jnp.inf
    nt = (((1,), (1,)), ((), ()))

    qa_scr[:, :hd] = q_ref[0]
    ka_scr[:, :hd] = k_ref[0]
    ka_scr[:, hd:] = kx_ref[0]
    va_scr[:, :hd] = v_ref[0]
    va_scr[:, hd:] = jnp.ones((seq, hd), BF16)
    kmean_scr[...] = jnp.zeros_like(kmean_scr)
    for n in range(nb):
        kmean_scr[n:n + 1, :] = jnp.mean(k_ref[0, n * kb:(n + 1) * kb, :].astype(F32), axis=0, keepdims=True)
    row = lax.broadcasted_iota(jnp.int32, (kb, kb), 0)
    colk = lax.broadcasted_iota(jnp.int32, (kb, kb), 1)
    causal_scr[...] = jnp.where(row >= colk, 0.0, neg_inf)

    g0 = min((TOP_K_BLOCKS + 1) * kb, seq)
    qa_scr[:g0, hd:] = qx_ref[0, :g0, :]
    if seq > g0:
        ng = seq - g0
        gate = lax.dot_general(q_ref[0, g0:, :].astype(F32), kmean_scr[...], nt,
                               precision=lax.Precision.HIGHEST, preferred_element_type=F32)
        col = lax.broadcasted_iota(jnp.int32, (ng, LANES), 1)
        colf = col.astype(F32)
        own_v = lax.div(lax.broadcasted_iota(jnp.int32, (ng, LANES), 0), jnp.int32(kb)) + (TOP_K_BLOCKS + 1)
        past = col < own_v
        gate = jnp.where(past, gate, neg_inf)
        mask = jnp.where(past, MASK_BIG, 0.0)
        for _ in range(TOP_K_BLOCKS):
            mx = jnp.max(gate, axis=-1, keepdims=True)
            idx = jnp.min(jnp.where(gate == mx, colf, float(LANES)), axis=-1, keepdims=True)
            hit = colf == idx
            mask = jnp.where(hit, 0.0, mask)
            gate = jnp.where(hit, neg_inf, gate)
        qa_scr[g0:, hd:] = (qx_ref[0, g0:, :].astype(F32) + mask).astype(BF16)

    for slot, t0 in enumerate(range(0, nb, MOBA_TILE_GROUP)):
        slot = slot % 2
        tiles = range(t0, min(t0 + MOBA_TILE_GROUP, nb))
        q0 = t0 * kb
        rows = len(tiles) * kb
        width = (tiles[-1] + 1) * kb
        s_scr[slot, 0:rows, 0:width] = lax.dot_general(qa_scr[q0:q0 + rows, :], ka_scr[0:width, :], nt,
                                                       preferred_element_type=F32)
        for own in tiles:
            base = (own - t0) * kb
            for rc in range(kb // rc_rows):
                r0, r1 = base + rc * rc_rows, base + (rc + 1) * rc_rows
                parts = [s_scr[slot, r0:r1, n * kb:(n + 1) * kb] for n in range(own)]
                parts.append(s_scr[slot, r0:r1, own * kb:(own + 1) * kb] + causal_scr[r0 - base:r1 - base, :])
                m = jnp.max(functools.reduce(jnp.maximum, parts), axis=-1, keepdims=True)
                for n in range(own + 1):
                    p_scr[slot, r0:r1, n * kb:(n + 1) * kb] = jnp.exp2(parts[n] - m).astype(BF16)
            if (own + 1) * kb < width:
                p_scr[slot, base:base + kb, (own + 1) * kb:width] = jnp.zeros((kb, width - (own + 1) * kb), BF16)

        res = jnp.dot(p_scr[slot, 0:rows, 0:width], va_scr[0:width, :], preferred_element_type=F32)
        o_ref[0, q0:q0 + rows, :] = (res[:, :hd] / res[:, hd:]).astype(o_ref.dtype)


def _moba_attention(qkv3d, d_model):
    b, l, _ = qkv3d.shape
    n_heads = d_model // HEAD_DIM
    kb = KEY_BLOCK
    qx, kx = _moba_tables(n_heads, l)
    return pl.pallas_call(
        _moba_kernel,
        grid=(b, n_heads),
        in_specs=[
            pl.BlockSpec((1, l, HEAD_DIM), lambda i, h: (i, 0, h)),
            pl.BlockSpec((1, l, HEAD_DIM), lambda i, h: (i, 0, n_heads + h)),
            pl.BlockSpec((1, l, HEAD_DIM), lambda i, h: (i, 0, 2 * n_heads + h)),
            pl.BlockSpec((1, l, LANES), lambda i, h: (h, 0, 0)),
            pl.BlockSpec((1, l, LANES), lambda i, h: (h, 0, 0)),
        ],
        out_specs=pl.BlockSpec((1, l, HEAD_DIM), lambda i, h: (i, 0, h)),
        out_shape=jax.ShapeDtypeStruct((b, l, d_model), BF16),
        scratch_shapes=[
            pltpu.VMEM((LANES, HEAD_DIM), F32),
            pltpu.VMEM((kb, kb), F32),
            pltpu.VMEM((l, HEAD_DIM + LANES), BF16),
            pltpu.VMEM((l, HEAD_DIM + LANES), BF16),
            pltpu.VMEM((l, 2 * HEAD_DIM), BF16),
            pltpu.VMEM((2, MOBA_TILE_GROUP * kb, l), F32),
            pltpu.VMEM((2, MOBA_TILE_GROUP * kb, l), BF16),
        ],
        compiler_params=_cparams(("parallel", "parallel")),
        name="moba_attention",
    )(qkv3d, qkv3d, qkv3d, qx, kx)


def kernel(x, g_mix, lambda_re, lambda_im, log_dt, b_re, b_im, c_re, c_im, d_skip, w_glu, b_glu, g_kv, w_kv,
           w_q, w_o, g_ffn, w_gate, w_up, w_down, g_final):
    bsz, l, d = x.shape
    depth = g_mix.shape[0]
    n_a = lambda_re.shape[0]
    m = bsz * l
    assert l % KEY_BLOCK == 0 and d % HEAD_DIM == 0 and l // KEY_BLOCK <= LANES
    row = lambda v: v.reshape(1, -1).astype(F32)

    h = x.astype(F32).reshape(m, d)
    qkv = None
    for i in range(depth):
        if i < n_a:
            tables = _s5_tables(lambda_re[i], lambda_im[i], log_dt[i], b_re[i], b_im[i], c_re[i], c_im[i])
            u = _rmsnorm_interleaved(h, row(g_mix[i]), bsz)
            z = _s5_mixer(u, row(d_skip[i]), *tables)
            h = _glu_residual(z, w_glu, i, row(b_glu[i]), h)
        else:
            j = i - n_a
            col = lambda v: v.reshape(-1, 1).astype(F32)
            qkv_i = _qkv_proj(h, col(g_mix[i]), col(g_kv), w_q, j, w_kv, HEAD_DIM ** -0.5 * LOG2_E)
            if qkv is None:
                qkv = qkv_i
            else:
                qkv = jnp.concatenate([qkv_i[:, :d], qkv[:, d:]], axis=1)
            attn = _moba_attention(qkv.reshape(bsz, l, 3 * d), d)
            h = _proj_residual(attn.reshape(m, d), w_o, j, h)
        h = _ffn_residual(h, row(g_ffn[i]), w_gate, w_up, w_down, i, row(g_final), final_norm=(i == depth - 1))
    return h.reshape(bsz, l, d).astype(x.dtype)
```

```python
import functools
import math

import jax
import jax.numpy as jnp
import numpy as np
from jax import lax
from jax.experimental import pallas as pl
from jax.experimental.pallas import tpu as pltpu

F32 = jnp.float32
BF16 = jnp.bfloat16

EPS = 1e-6
LANES = 128
HEAD_DIM = 128
KEY_BLOCK = 256
TOP_K_BLOCKS = 3
MOBA_ROW_CHUNK = 16
MOBA_TILE_GROUP = 2
MOBA_BIAS_TERMS = 4
LOG2_E = math.log2(math.e)
MASK_BIG = -2.0 ** 100
SSM_CHUNK = 8
S5_GROUP = 16
VMEM_LIMIT = 56 * 1024 * 1024


def _cparams(sem):
    return pltpu.CompilerParams(dimension_semantics=sem, vmem_limit_bytes=VMEM_LIMIT)


def _rms_scale(x):
    return lax.rsqrt(jnp.mean(x * x, axis=-1, keepdims=True) + EPS)


def _rmsnorm_kernel(x_ref, g_ref, o_ref):
    x = x_ref[...]
    o_ref[...] = (x * _rms_scale(x) * g_ref[...]).reshape(o_ref.shape)


def _rmsnorm_interleaved(x2d, g_row, bsz, tm=512):
    m, d = x2d.shape
    tiles = m // bsz // tm
    return pl.pallas_call(
        _rmsnorm_kernel,
        grid=(bsz, tiles),
        in_specs=[pl.BlockSpec((tm, d), lambda b, i: (b * tiles + i, 0)), pl.BlockSpec((1, d), lambda b, i: (0, 0))],
        out_specs=pl.BlockSpec((tm // S5_GROUP, None, S5_GROUP, d), lambda b, i: (i, b, 0, 0)),
        out_shape=jax.ShapeDtypeStruct((m // bsz // S5_GROUP, bsz, S5_GROUP, d), F32),
        compiler_params=_cparams(("parallel", "parallel")),
        name="rmsnorm",
    )(x2d, g_row)


def _s5_tables(lam_re, lam_im, log_dt, b_re, b_im, c_re, c_im):
    g, p, c = b_re.shape
    t = SSM_CHUNK
    gl = LANES // c
    nj = g // gl
    ns = gl * p
    lr, li = lam_re.astype(F32), lam_im.astype(F32)
    dt = jnp.exp(log_dt.astype(F32))[:, None]
    ar, ai = lr * dt, li * dt
    mag = jnp.exp(ar)
    nr, ni = mag * jnp.cos(ai) - 1.0, mag * jnp.sin(ai)
    den = lr * lr + li * li
    fr, fi = (nr * lr + ni * li) / den, (ni * lr - nr * li) / den
    br, bi = b_re.astype(F32), b_im.astype(F32)
    bbr = fr[..., None] * br - fi[..., None] * bi
    bbi = fr[..., None] * bi + fi[..., None] * br
    dd = jnp.arange(t + 1, dtype=F32)[:, None, None]
    pmag = jnp.exp(ar[None] * dd)
    pr, pi = pmag * jnp.cos(ai[None] * dd), pmag * jnp.sin(ai[None] * dd)

    def per_tile(x):
        return jnp.transpose(x.reshape(nj, gl, c, p), (0, 2, 1, 3)).reshape(nj, c, ns)

    bcmp = jnp.concatenate([per_tile(jnp.swapaxes(bbr, 1, 2)), per_tile(jnp.swapaxes(bbi, 1, 2))], axis=-1)
    ccmp = jnp.concatenate([per_tile(c_re.astype(F32)), per_tile(c_im.astype(F32))], axis=-1)
    pw = jnp.concatenate([pr.reshape(t + 1, nj, ns), pi.reshape(t + 1, nj, ns)], axis=-1)
    pw = jnp.pad(jnp.swapaxes(pw, 0, 1), ((0, 0), (0, 16 - (t + 1)), (0, 0)))
    return bcmp, ccmp, pw


def _s5_build_tables(bcmp_ref, ccmp_ref, pw_ref, w1_scr, w2_scr):
    t = SSM_CHUNK
    cg = bcmp_ref.shape[1]
    ns = bcmp_ref.shape[2] // 2
    sg = ns * cg // LANES
    yw = t * LANES
    same_group = (lax.broadcasted_iota(jnp.int32, (LANES, ns), 0) // cg
                  == lax.broadcasted_iota(jnp.int32, (LANES, ns), 1) // sg)

    def block_diag(x):
        return jnp.where(same_group, jnp.tile(x, (LANES // cg, 1)), 0.0)

    bre, bim = bcmp_ref[0, :, :ns], bcmp_ref[0, :, ns:]
    cre, cim = ccmp_ref[0, :, :ns], ccmp_ref[0, :, ns:]
    bcat16 = jnp.concatenate([block_diag(bre), block_diag(bim)], axis=1).astype(BF16)
    nt = (((1,), (1,)), ((), ()))
    for d in range(t + 1):
        pr, pi = pw_ref[0, d:d + 1, :ns], pw_ref[0, d:d + 1, ns:]
        mo = jnp.concatenate([block_diag(cre * pr - cim * pi), block_diag(-(cre * pi + cim * pr))],
                             axis=1).astype(BF16)
        if d >= 1:
            w2_scr[(d - 1) * LANES:d * LANES, :] = mo
        if d < t:
            kd = lax.dot_general(bcat16, mo, nt, preferred_element_type=F32).astype(BF16)
            for ip in range(t - d):
                w1_scr[ip * LANES:(ip + 1) * LANES, (ip + d) * LANES:(ip + d + 1) * LANES] = kd
            ip = t - 1 - d
            w1_scr[ip * LANES:(ip + 1) * LANES, yw:yw + ns] = block_diag(bre * pr - bim * pi).astype(BF16)
            w1_scr[ip * LANES:(ip + 1) * LANES, yw + ns:] = block_diag(bre * pi + bim * pr).astype(BF16)
    for ip in range(1, t):
        w1_scr[ip * LANES:(ip + 1) * LANES, 0:ip * LANES] = jnp.zeros((LANES, ip * LANES), BF16)


def _s5_kernel(u_ref, d_ref, bcmp_ref, ccmp_ref, pw_ref, z_ref, w1_scr, w2_scr, lhs_scr, s_scr, yc_scr, y_scr,
               *, n_seq):
    t = SSM_CHUNK
    cpg = S5_GROUP // t
    rows = lhs_scr.shape[0]
    ngroups = rows // (n_seq * cpg)
    nsl = s_scr.shape[0] // 2
    yw = t * LANES

    _s5_build_tables(bcmp_ref, ccmp_ref, pw_ref, w1_scr, w2_scr)

    for i in range(t):
        lhs_scr[:, i * LANES:(i + 1) * LANES] = u_ref[pl.ds(i, rows, stride=t), :].astype(BF16)
    strip = 2 * LANES
    for c0 in range(0, yw, strip):
        c1 = c0 + strip
        yc_scr[:, c0:c1] = jnp.dot(lhs_scr[:, :c1], w1_scr[:c1, c0:c1], preferred_element_type=F32)
    s_in_all = jnp.dot(lhs_scr[...], w1_scr[:, yw:], preferred_element_type=F32)
    for q in range(2 * nsl):
        s_scr[q] = s_in_all[:, q * LANES:(q + 1) * LANES]

    a_re = [pw_ref[0, t:t + 1, q * LANES:(q + 1) * LANES] for q in range(nsl)]
    a_im = [pw_ref[0, t:t + 1, (nsl + q) * LANES:(nsl + q + 1) * LANES] for q in range(nsl)]

    def scan_group(g, carry):
        e = list(carry)
        for c in range(cpg):
            sel = pl.ds(g * (n_seq * cpg) + c, n_seq, stride=cpg)
            for q in range(nsl):
                e_re, e_im = e[q], e[nsl + q]
                in_re, in_im = s_scr[q, sel, :], s_scr[nsl + q, sel, :]
                s_scr[q, sel, :] = e_re
                s_scr[nsl + q, sel, :] = e_im
                e[q] = a_re[q] * e_re - a_im[q] * e_im + in_re
                e[nsl + q] = a_re[q] * e_im + a_im[q] * e_re + in_im
        return tuple(e)

    lax.fori_loop(0, ngroups, scan_group, (jnp.zeros((n_seq, LANES), F32),) * (2 * nsl))

    s_prev = jnp.concatenate([s_scr[q].astype(BF16) for q in range(2 * nsl)], axis=1)
    y = yc_scr[...] + lax.dot_general(s_prev, w2_scr[...], (((1,), (1,)), ((), ())), preferred_element_type=F32)
    for i in range(t):
        y_scr[pl.ds(i, rows, stride=t), :] = y[:, i * LANES:(i + 1) * LANES]

    yy = y_scr[...] + d_ref[...] * u_ref[...]
    z_ref[...] = jax.nn.gelu(yy).astype(z_ref.dtype)


def _s5_mixer(u4d, d_row, bcmp, ccmp, pw):
    ngr, bsz, grp, d = u4d.shape
    m = ngr * bsz * grp
    nj = d // LANES
    rows = m // SSM_CHUNK
    cg, ns2 = bcmp.shape[1:]
    yw = SSM_CHUNK * LANES
    z = pl.pallas_call(
        functools.partial(_s5_kernel, n_seq=bsz),
        grid=(nj,),
        in_specs=[
            pl.BlockSpec((m, LANES), lambda j: (0, j)),
            pl.BlockSpec((1, LANES), lambda j: (0, j)),
            pl.BlockSpec((1, cg, ns2), lambda j: (j, 0, 0)),
            pl.BlockSpec((1, cg, ns2), lambda j: (j, 0, 0)),
            pl.BlockSpec((1, 16, ns2), lambda j: (j, 0, 0)),
        ],
        out_specs=pl.BlockSpec((m, LANES), lambda j: (0, j)),
        out_shape=jax.ShapeDtypeStruct((m, d), BF16),
        scratch_shapes=[
            pltpu.VMEM((yw, yw + ns2), BF16),
            pltpu.VMEM((yw, ns2), BF16),
            pltpu.VMEM((rows, yw), BF16),
            pltpu.VMEM((ns2 // LANES, rows, LANES), F32),
            pltpu.VMEM((rows, yw), F32),
            pltpu.VMEM((m, LANES), F32),
        ],
        compiler_params=_cparams(("parallel",)),
        name="s5_mixer",
    )(u4d.reshape(m, d), d_row, bcmp, ccmp, pw)
    return z.reshape(ngr, bsz, grp, d)


def _glu_kernel(z_ref, w_ref, b_ref, res_ref, o_ref, w16_scr):
    tm, n = o_ref.shape

    @pl.when((pl.program_id(0) == 0) & (pl.program_id(1) == 0))
    def _():
        w16_scr[...] = w_ref[...].astype(BF16)

    z = z_ref[...].reshape(tm, n)
    a = jnp.dot(z, w16_scr[...], preferred_element_type=F32) + b_ref[...]
    o_ref[...] = res_ref[...] + z.astype(F32) * jax.nn.sigmoid(a)


def _glu_residual(z4d, w, layer, b_row, res, tm=512):
    ngr, bsz, grp, k = z4d.shape
    m = ngr * bsz * grp
    n = w.shape[2]
    assert k == n and (ngr * grp) % tm == 0 and tm % grp == 0
    tiles = ngr * grp // tm
    return pl.pallas_call(
        _glu_kernel,
        grid=(bsz, tiles),
        in_specs=[
            pl.BlockSpec((tm // grp, None, grp, k), lambda b, i: (i, b, 0, 0)),
            pl.BlockSpec((None, k, n), lambda b, i: (layer, 0, 0), pipeline_mode=pl.Buffered(1)),
            pl.BlockSpec((1, n), lambda b, i: (0, 0)),
            pl.BlockSpec((tm, n), lambda b, i: (b * tiles + i, 0)),
        ],
        out_specs=pl.BlockSpec((tm, n), lambda b, i: (b * tiles + i, 0)),
        out_shape=jax.ShapeDtypeStruct((m, n), F32),
        scratch_shapes=[pltpu.VMEM((k, n), BF16)],
        compiler_params=_cparams(("arbitrary", "arbitrary")),
        name="glu_residual",
    )(z4d, w, b_row, res)


def _proj_res_kernel(x_ref, w_ref, res_ref, o_ref, w16_scr):
    @pl.when(pl.program_id(0) == 0)
    def _():
        w16_scr[...] = w_ref[...].astype(BF16)

    o_ref[...] = res_ref[...] + jnp.dot(x_ref[...], w16_scr[...], preferred_element_type=F32)


def _proj_residual(x, w, layer, res, tm=512):
    m, k = x.shape
    n = w.shape[2]
    assert m % tm == 0
    return pl.pallas_call(
        _proj_res_kernel,
        grid=(m // tm,),
        in_specs=[
            pl.BlockSpec((tm, k), lambda i: (i, 0)),
            pl.BlockSpec((None, k, n), lambda i: (layer, 0, 0), pipeline_mode=pl.Buffered(1)),
            pl.BlockSpec((tm, n), lambda i: (i, 0)),
        ],
        out_specs=pl.BlockSpec((tm, n), lambda i: (i, 0)),
        out_shape=jax.ShapeDtypeStruct((m, n), F32),
        scratch_shapes=[pltpu.VMEM((k, n), BF16)],
        compiler_params=_cparams(("arbitrary",)),
        name="proj_residual",
    )(x, w, res)


def _qkv_kernel(x_ref, gq_ref, gkv_ref, wq_ref, wkv_ref, o_ref, xn_scr, *, n_q_tiles, q_scale):
    j = pl.program_id(1)

    @pl.when(j == 0)
    def _():
        x = x_ref[...]
        xn_scr[...] = (x * _rms_scale(x)).astype(BF16)

    @pl.when(j < n_q_tiles)
    def _():
        q = jnp.dot(xn_scr[...], (wq_ref[...] * gq_ref[...]).astype(BF16), preferred_element_type=F32)
        o_ref[...] = (q * q_scale).astype(o_ref.dtype)

    @pl.when(j >= n_q_tiles)
    def _():
        kv = jnp.dot(xn_scr[...], (wkv_ref[...] * gkv_ref[...]).astype(BF16), preferred_element_type=F32)
        o_ref[...] = kv.astype(o_ref.dtype)


def _qkv_proj(h2d, gq_col, gkv_col, w_q, layer, w_kv, q_scale, tm=1024, tn=512):
    m, k = h2d.shape
    nq = w_q.shape[2] // tn
    nkv = w_kv.shape[1] // tn
    return pl.pallas_call(
        functools.partial(_qkv_kernel, n_q_tiles=nq, q_scale=q_scale),
        grid=(m // tm, nq + nkv),
        in_specs=[
            pl.BlockSpec((tm, k), lambda i, j: (i, 0)),
            pl.BlockSpec((k, 1), lambda i, j: (0, 0)),
            pl.BlockSpec((k, 1), lambda i, j: (0, 0)),
            pl.BlockSpec((None, k, tn), lambda i, j: (layer, 0, jnp.minimum(j, nq - 1))),
            pl.BlockSpec((k, tn), lambda i, j: (0, jnp.maximum(j - nq, 0))),
        ],
        out_specs=pl.BlockSpec((tm, tn), lambda i, j: (i, j)),
        out_shape=jax.ShapeDtypeStruct((m, (nq + nkv) * tn), BF16),
        scratch_shapes=[pltpu.VMEM((tm, k), BF16)],
        compiler_params=_cparams(("parallel", "arbitrary")),
        name="qkv_proj",
    )(h2d, gq_col, gkv_col, w_q, w_kv)


def _ffn_kernel(h_ref, g_ref, wg_ref, wu_ref, wd_ref, gfin_ref, o_ref, u_scr, *, final_norm):
    f = pl.program_id(1)

    @pl.when(f == 0)
    def _():
        x = h_ref[...]
        u_scr[...] = (x * _rms_scale(x) * g_ref[...]).astype(BF16)
        o_ref[...] = x

    u = u_scr[...]
    gate = jnp.dot(u, wg_ref[...].astype(BF16), preferred_element_type=F32)
    up = jnp.dot(u, wu_ref[...].astype(BF16), preferred_element_type=F32)
    act = (jax.nn.silu(gate) * up).astype(BF16)
    o_ref[...] += jnp.dot(act, wd_ref[...].astype(BF16), preferred_element_type=F32)

    if final_norm:
        @pl.when(f == pl.num_programs(1) - 1)
        def _():
            y = o_ref[...]
            o_ref[...] = y * _rms_scale(y) * gfin_ref[...]


def _ffn_residual(h2d, g_row, w_gate, w_up, w_down, layer, gfin_row, final_norm, tm=1024, tf=256):
    m, d = h2d.shape
    ff = w_gate.shape[2]
    return pl.pallas_call(
        functools.partial(_ffn_kernel, final_norm=final_norm),
        grid=(m // tm, ff // tf),
        in_specs=[
            pl.BlockSpec((tm, d), lambda i, f: (i, 0)),
            pl.BlockSpec((1, d), lambda i, f: (0, 0)),
            pl.BlockSpec((None, d, tf), lambda i, f: (layer, 0, f)),
            pl.BlockSpec((None, d, tf), lambda i, f: (layer, 0, f)),
            pl.BlockSpec((None, tf, d), lambda i, f: (layer, f, 0)),
            pl.BlockSpec((1, d), lambda i, f: (0, 0)),
        ],
        out_specs=pl.BlockSpec((tm, d), lambda i, f: (i, 0)),
        out_shape=jax.ShapeDtypeStruct((m, d), F32),
        scratch_shapes=[pltpu.VMEM((tm, d), BF16)],
        compiler_params=_cparams(("parallel", "arbitrary")),
        name="ffn_residual",
    )(h2d, g_row, w_gate, w_up, w_down, gfin_row)


def _moba_tables(n_heads, seq_len):
    nb = seq_len // KEY_BLOCK
    nt = MOBA_BIAS_TERMS
    assert nb + 2 * nt <= LANES
    slopes = np.exp2(-8.0 * np.arange(1, n_heads + 1, dtype=np.float64) / n_heads) * LOG2_E
    bias = slopes[:, None] * np.arange(seq_len, dtype=np.float64)[None, :]
    top = 2.0 ** math.ceil(math.log2(LOG2_E * seq_len))

    terms, rest, quantum = [], bias, top / 256.0
    for _ in range(nt - 1):
        term = np.floor(rest / quantum) * quantum
        terms.append(term)
        rest = rest - term
        quantum = quantum / 256.0
    terms.append(rest)

    qx = np.zeros((n_heads, seq_len, LANES), np.float64)
    kx = np.zeros((n_heads, seq_len, LANES), np.float64)
    kx[:, np.arange(seq_len), np.arange(seq_len) // KEY_BLOCK] = 1.0
    for c, term in enumerate(terms):
        qx[:, :, nb + c] = -term
        qx[:, :, nb + nt + c] = 1.0
        kx[:, :, nb + c] = 1.0
        kx[:, :, nb + nt + c] = term
    return jnp.asarray(qx.astype(jnp.bfloat16)), jnp.asarray(kx.astype(jnp.bfloat16))


def _moba_kernel(q_ref, k_ref, v_ref, qx_ref, kx_ref, o_ref, kmean_scr, causal_scr, qa_scr, ka_scr, va_scr,
                 s_scr, p_scr):
    kb = KEY_BLOCK
    hd = HEAD_DIM
    rc_rows = MOBA_ROW_CHUNK
    seq = k_ref.shape[1]
    nb = seq // kb
    neg_inf = -jnp.inf
    nt = (((1,), (1,)), ((), ()))

    qa_scr[:, :hd] = q_ref[0]
    ka_scr[:, :hd] = k_ref[0]
    ka_scr[:, hd:] = kx_ref[0]
    va_scr[:, :hd] = v_ref[0]
    va_scr[:, hd:] = jnp.ones((seq, hd), BF16)
    kmean_scr[...] = jnp.zeros_like(kmean_scr)
    for n in range(nb):
        kmean_scr[n:n + 1, :] = jnp.mean(k_ref[0, n * kb:(n + 1) * kb, :].astype(F32), axis=0, keepdims=True)
    row = lax.broadcasted_iota(jnp.int32, (kb, kb), 0)
    colk = lax.broadcasted_iota(jnp.int32, (kb, kb), 1)
    causal_scr[...] = jnp.where(row >= colk, 0.0, neg_inf)

    g0 = min((TOP_K_BLOCKS + 1) * kb, seq)
    qa_scr[:g0, hd:] = qx_ref[0, :g0, :]
    if seq > g0:
        ng = seq - g0
        gate = lax.dot_general(q_ref[0, g0:, :].astype(F32), kmean_scr[...], nt,
                               precision=lax.Precision.HIGHEST, preferred_element_type=F32)
        col = lax.broadcasted_iota(jnp.int32, (ng, LANES), 1)
        colf = col.astype(F32)
        own_v = lax.div(lax.broadcasted_iota(jnp.int32, (ng, LANES), 0), jnp.int32(kb)) + (TOP_K_BLOCKS + 1)
        past = col < own_v
        gate = jnp.where(past, gate, neg_inf)
        mask = jnp.where(past, MASK_BIG, 0.0)
        for _ in range(TOP_K_BLOCKS):
            mx = jnp.max(gate, axis=-1, keepdims=True)
            idx = jnp.min(jnp.where(gate == mx, colf, float(LANES)), axis=-1, keepdims=True)
            hit = colf == idx
            mask = jnp.where(hit, 0.0, mask)
            gate = jnp.where(hit, neg_inf, gate)
        qa_scr[g0:, hd:] = (qx_ref[0, g0:, :].astype(F32) + mask).astype(BF16)

    for slot, t0 in enumerate(range(0, nb, MOBA_TILE_GROUP)):
        slot = slot % 2
        tiles = range(t0, min(t0 + MOBA_TILE_GROUP, nb))
        q0 = t0 * kb
        rows = len(tiles) * kb
        width = (tiles[-1] + 1) * kb
        s_scr[slot, 0:rows, 0:width] = lax.dot_general(qa_scr[q0:q0 + rows, :], ka_scr[0:width, :], nt,
                                                       preferred_element_type=F32)
        for own in tiles:
            base = (own - t0) * kb
            for rc in range(kb // rc_rows):
                r0, r1 = base + rc * rc_rows, base + (rc + 1) * rc_rows
                parts = [s_scr[slot, r0:r1, n * kb:(n + 1) * kb] for n in range(own)]
                parts.append(s_scr[slot, r0:r1, own * kb:(own + 1) * kb] + causal_scr[r0 - base:r1 - base, :])
                m = jnp.max(functools.reduce(jnp.maximum, parts), axis=-1, keepdims=True)
                for n in range(own + 1):
                    p_scr[slot, r0:r1, n * kb:(n + 1) * kb] = jnp.exp2(parts[n] - m).astype(BF16)
            if (own + 1) * kb < width:
                p_scr[slot, base:base + kb, (own + 1) * kb:width] = jnp.zeros((kb, width - (own + 1) * kb), BF16)

        for own in tiles:
            base = (own - t0) * kb
            res = jnp.dot(p_scr[slot, base:base + kb, 0:width], va_scr[0:width, :],
                          preferred_element_type=F32)
            o_ref[0, own * kb:(own + 1) * kb, :] = (res[:, :hd] / res[:, hd:]).astype(o_ref.dtype)


def _moba_attention(qkv3d, d_model):
    b, l, _ = qkv3d.shape
    n_heads = d_model // HEAD_DIM
    kb = KEY_BLOCK
    qx, kx = _moba_tables(n_heads, l)
    return pl.pallas_call(
        _moba_kernel,
        grid=(b, n_heads),
        in_specs=[
            pl.BlockSpec((1, l, HEAD_DIM), lambda i, h: (i, 0, h)),
            pl.BlockSpec((1, l, HEAD_DIM), lambda i, h: (i, 0, n_heads + h)),
            pl.BlockSpec((1, l, HEAD_DIM), lambda i, h: (i, 0, 2 * n_heads + h)),
            pl.BlockSpec((1, l, LANES), lambda i, h: (h, 0, 0)),
            pl.BlockSpec((1, l, LANES), lambda i, h: (h, 0, 0)),
        ],
        out_specs=pl.BlockSpec((1, l, HEAD_DIM), lambda i, h: (i, 0, h)),
        out_shape=jax.ShapeDtypeStruct((b, l, d_model), BF16),
        scratch_shapes=[
            pltpu.VMEM((LANES, HEAD_DIM), F32),
            pltpu.VMEM((kb, kb), F32),
            pltpu.VMEM((l, HEAD_DIM + LANES), BF16),
            pltpu.VMEM((l, HEAD_DIM + LANES), BF16),
            pltpu.VMEM((l, 2 * HEAD_DIM), BF16),
            pltpu.VMEM((2, MOBA_TILE_GROUP * kb, l), F32),
            pltpu.VMEM((2, MOBA_TILE_GROUP * kb, l), BF16),
        ],
        compiler_params=_cparams(("parallel", "parallel")),
        name="moba_attention",
    )(qkv3d, qkv3d, qkv3d, qx, kx)


def kernel(x, g_mix, lambda_re, lambda_im, log_dt, b_re, b_im, c_re, c_im, d_skip, w_glu, b_glu, g_kv, w_kv,
           w_q, w_o, g_ffn, w_gate, w_up, w_down, g_final):
    bsz, l, d = x.shape
    depth = g_mix.shape[0]
    n_a = lambda_re.shape[0]
    m = bsz * l
    assert l % KEY_BLOCK == 0 and d % HEAD_DIM == 0 and l // KEY_BLOCK <= LANES
    row = lambda v: v.reshape(1, -1).astype(F32)

    h = x.astype(F32).reshape(m, d)
    qkv = None
    for i in range(depth):
        if i < n_a:
            tables = _s5_tables(lambda_re[i], lambda_im[i], log_dt[i], b_re[i], b_im[i], c_re[i], c_im[i])
            u = _rmsnorm_interleaved(h, row(g_mix[i]), bsz)
            z = _s5_mixer(u, row(d_skip[i]), *tables)
            h = _glu_residual(z, w_glu, i, row(b_glu[i]), h)
        else:
            j = i - n_a
            col = lambda v: v.reshape(-1, 1).astype(F32)
            qkv_i = _qkv_proj(h, col(g_mix[i]), col(g_kv), w_q, j, w_kv, HEAD_DIM ** -0.5 * LOG2_E)
            if qkv is None:
                qkv = qkv_i
            else:
                qkv = jnp.concatenate([qkv_i[:, :d], qkv[:, d:]], axis=1)
            attn = _moba_attention(qkv.reshape(bsz, l, 3 * d), d)
            h = _proj_residual(attn.reshape(m, d), w_o, j, h)
        h = _ffn_residual(h, row(g_ffn[i]), w_gate, w_up, w_down, i, row(g_final), final_norm=(i == depth - 1))
    return h.reshape(bsz, l, d).astype(x.dtype)
```

```python
import functools
import math

import jax
import jax.numpy as jnp
import numpy as np
from jax import lax
from jax.experimental import pallas as pl
from jax.experimental.pallas import tpu as pltpu

F32 = jnp.float32
BF16 = jnp.bfloat16

EPS = 1e-6
LANES = 128
HEAD_DIM = 128
KEY_BLOCK = 256
TOP_K_BLOCKS = 3
MOBA_ROW_CHUNK = 16
MOBA_TILE_GROUP = 2
MOBA_BIAS_TERMS = 4
LOG2_E = math.log2(math.e)
GELU_A = -2.0 * math.sqrt(2.0 / math.pi) * LOG2_E
GELU_B = GELU_A * 0.044715
MASK_BIG = -2.0 ** 100
SSM_CHUNK = 8
S5_GROUP = 16
VMEM_LIMIT = 56 * 1024 * 1024


def _cparams(sem):
    return pltpu.CompilerParams(dimension_semantics=sem, vmem_limit_bytes=VMEM_LIMIT)


def _rms_scale(x):
    return lax.rsqrt(jnp.mean(x * x, axis=-1, keepdims=True) + EPS)


def _rmsnorm_kernel(x_ref, g_ref, o_ref):
    x = x_ref[...]
    o_ref[...] = (x * _rms_scale(x) * g_ref[...]).reshape(o_ref.shape)


def _rmsnorm_interleaved(x2d, g_row, bsz, tm=512):
    m, d = x2d.shape
    tiles = m // bsz // tm
    return pl.pallas_call(
        _rmsnorm_kernel,
        grid=(bsz, tiles),
        in_specs=[pl.BlockSpec((tm, d), lambda b, i: (b * tiles + i, 0)), pl.BlockSpec((1, d), lambda b, i: (0, 0))],
        out_specs=pl.BlockSpec((tm // S5_GROUP, None, S5_GROUP, d), lambda b, i: (i, b, 0, 0)),
        out_shape=jax.ShapeDtypeStruct((m // bsz // S5_GROUP, bsz, S5_GROUP, d), F32),
        compiler_params=_cparams(("parallel", "parallel")),
        name="rmsnorm",
    )(x2d, g_row)


def _s5_tables(lam_re, lam_im, log_dt, b_re, b_im, c_re, c_im):
    g, p, c = b_re.shape
    t = SSM_CHUNK
    gl = LANES // c
    nj = g // gl
    ns = gl * p
    lr, li = lam_re.astype(F32), lam_im.astype(F32)
    dt = jnp.exp(log_dt.astype(F32))[:, None]
    ar, ai = lr * dt, li * dt
    mag = jnp.exp(ar)
    nr, ni = mag * jnp.cos(ai) - 1.0, mag * jnp.sin(ai)
    den = lr * lr + li * li
    fr, fi = (nr * lr + ni * li) / den, (ni * lr - nr * li) / den
    br, bi = b_re.astype(F32), b_im.astype(F32)
    bbr = fr[..., None] * br - fi[..., None] * bi
    bbi = fr[..., None] * bi + fi[..., None] * br
    dd = jnp.arange(t + 1, dtype=F32)[:, None, None]
    pmag = jnp.exp(ar[None] * dd)
    pr, pi = pmag * jnp.cos(ai[None] * dd), pmag * jnp.sin(ai[None] * dd)

    def per_tile(x):
        return jnp.transpose(x.reshape(nj, gl, c, p), (0, 2, 1, 3)).reshape(nj, c, ns)

    bcmp = jnp.concatenate([per_tile(jnp.swapaxes(bbr, 1, 2)), per_tile(jnp.swapaxes(bbi, 1, 2))], axis=-1)
    ccmp = jnp.concatenate([per_tile(c_re.astype(F32)), per_tile(c_im.astype(F32))], axis=-1)
    pw = jnp.concatenate([pr.reshape(t + 1, nj, ns), pi.reshape(t + 1, nj, ns)], axis=-1)
    pw = jnp.pad(jnp.swapaxes(pw, 0, 1), ((0, 0), (0, 16 - (t + 1)), (0, 0)))
    return bcmp, ccmp, pw


def _s5_build_tables(bcmp_ref, ccmp_ref, pw_ref, w1_scr, w2_scr):
    t = SSM_CHUNK
    cg = bcmp_ref.shape[1]
    ns = bcmp_ref.shape[2] // 2
    sg = ns * cg // LANES
    yw = t * LANES
    same_group = (lax.broadcasted_iota(jnp.int32, (LANES, ns), 0) // cg
                  == lax.broadcasted_iota(jnp.int32, (LANES, ns), 1) // sg)

    def block_diag(x):
        return jnp.where(same_group, jnp.tile(x, (LANES // cg, 1)), 0.0)

    bre, bim = bcmp_ref[0, :, :ns], bcmp_ref[0, :, ns:]
    cre, cim = ccmp_ref[0, :, :ns], ccmp_ref[0, :, ns:]
    bcat16 = jnp.concatenate([block_diag(bre), block_diag(bim)], axis=1).astype(BF16)
    nt = (((1,), (1,)), ((), ()))
    for d in range(t + 1):
        pr, pi = pw_ref[0, d:d + 1, :ns], pw_ref[0, d:d + 1, ns:]
        mo = jnp.concatenate([block_diag(cre * pr - cim * pi), block_diag(-(cre * pi + cim * pr))],
                             axis=1).astype(BF16)
        if d >= 1:
            w2_scr[(d - 1) * LANES:d * LANES, :] = mo
        if d < t:
            kd = lax.dot_general(bcat16, mo, nt, preferred_element_type=F32).astype(BF16)
            for ip in range(t - d):
                w1_scr[ip * LANES:(ip + 1) * LANES, (ip + d) * LANES:(ip + d + 1) * LANES] = kd
            ip = t - 1 - d
            w1_scr[ip * LANES:(ip + 1) * LANES, yw:yw + ns] = block_diag(bre * pr - bim * pi).astype(BF16)
            w1_scr[ip * LANES:(ip + 1) * LANES, yw + ns:] = block_diag(bre * pi + bim * pr).astype(BF16)
    for ip in range(1, t):
        w1_scr[ip * LANES:(ip + 1) * LANES, 0:ip * LANES] = jnp.zeros((LANES, ip * LANES), BF16)


def _s5_kernel(u_ref, d_ref, bcmp_ref, ccmp_ref, pw_ref, z_ref, w1_scr, w2_scr, lhs_scr, s_scr, yc_scr, y_scr,
               *, n_seq):
    t = SSM_CHUNK
    cpg = S5_GROUP // t
    rows = lhs_scr.shape[0]
    ngroups = rows // (n_seq * cpg)
    nsl = s_scr.shape[0] // 2
    yw = t * LANES

    _s5_build_tables(bcmp_ref, ccmp_ref, pw_ref, w1_scr, w2_scr)

    for i in range(t):
        lhs_scr[:, i * LANES:(i + 1) * LANES] = u_ref[pl.ds(i, rows, stride=t), :].astype(BF16)
    strip = 2 * LANES
    for c0 in range(0, yw, strip):
        c1 = c0 + strip
        yc_scr[:, c0:c1] = jnp.dot(lhs_scr[:, :c1], w1_scr[:c1, c0:c1], preferred_element_type=F32)
    s_in_all = jnp.dot(lhs_scr[...], w1_scr[:, yw:], preferred_element_type=F32)
    for q in range(2 * nsl):
        s_scr[q] = s_in_all[:, q * LANES:(q + 1) * LANES]

    a_re = [pw_ref[0, t:t + 1, q * LANES:(q + 1) * LANES] for q in range(nsl)]
    a_im = [pw_ref[0, t:t + 1, (nsl + q) * LANES:(nsl + q + 1) * LANES] for q in range(nsl)]

    def scan_group(g, carry):
        e = list(carry)
        for c in range(cpg):
            sel = pl.ds(g * (n_seq * cpg) + c, n_seq, stride=cpg)
            for q in range(nsl):
                e_re, e_im = e[q], e[nsl + q]
                in_re, in_im = s_scr[q, sel, :], s_scr[nsl + q, sel, :]
                s_scr[q, sel, :] = e_re
                s_scr[nsl + q, sel, :] = e_im
                e[q] = a_re[q] * e_re - a_im[q] * e_im + in_re
                e[nsl + q] = a_re[q] * e_im + a_im[q] * e_re + in_im
        return tuple(e)

    lax.fori_loop(0, ngroups, scan_group, (jnp.zeros((n_seq, LANES), F32),) * (2 * nsl))

    s_prev = jnp.concatenate([s_scr[q].astype(BF16) for q in range(2 * nsl)], axis=1)
    y = yc_scr[...] + lax.dot_general(s_prev, w2_scr[...], (((1,), (1,)), ((), ())), preferred_element_type=F32)
    for i in range(t):
        y_scr[pl.ds(i, rows, stride=t), :] = y[:, i * LANES:(i + 1) * LANES]

    yy = y_scr[...] + d_ref[...] * u_ref[...]
    e = jnp.exp2(yy * (GELU_A + GELU_B * (yy * yy)))
    z_ref[...] = (yy / (1.0 + e)).astype(z_ref.dtype)


def _s5_mixer(u4d, d_row, bcmp, ccmp, pw):
    ngr, bsz, grp, d = u4d.shape
    m = ngr * bsz * grp
    nj = d // LANES
    rows = m // SSM_CHUNK
    cg, ns2 = bcmp.shape[1:]
    yw = SSM_CHUNK * LANES
    z = pl.pallas_call(
        functools.partial(_s5_kernel, n_seq=bsz),
        grid=(nj,),
        in_specs=[
            pl.BlockSpec((m, LANES), lambda j: (0, j)),
            pl.BlockSpec((1, LANES), lambda j: (0, j)),
            pl.BlockSpec((1, cg, ns2), lambda j: (j, 0, 0)),
            pl.BlockSpec((1, cg, ns2), lambda j: (j, 0, 0)),
            pl.BlockSpec((1, 16, ns2), lambda j: (j, 0, 0)),
        ],
        out_specs=pl.BlockSpec((m, LANES), lambda j: (0, j)),
        out_shape=jax.ShapeDtypeStruct((m, d), BF16),
        scratch_shapes=[
            pltpu.VMEM((yw, yw + ns2), BF16),
            pltpu.VMEM((yw, ns2), BF16),
            pltpu.VMEM((rows, yw), BF16),
            pltpu.VMEM((ns2 // LANES, rows, LANES), F32),
            pltpu.VMEM((rows, yw), F32),
            pltpu.VMEM((m, LANES), F32),
        ],
        compiler_params=_cparams(("parallel",)),
        name="s5_mixer",
    )(u4d.reshape(m, d), d_row, bcmp, ccmp, pw)
    return z.reshape(ngr, bsz, grp, d)


def _glu_kernel(z_ref, w_ref, b_ref, res_ref, o_ref, w16_scr):
    tm, n = o_ref.shape

    @pl.when((pl.program_id(0) == 0) & (pl.program_id(1) == 0))
    def _():
        w16_scr[...] = w_ref[...].astype(BF16)

    z = z_ref[...].reshape(tm, n)
    a = jnp.dot(z, w16_scr[...], preferred_element_type=F32) + b_ref[...]
    o_ref[...] = res_ref[...] + z.astype(F32) * jax.nn.sigmoid(a)


def _glu_residual(z4d, w, layer, b_row, res, tm=512):
    ngr, bsz, grp, k = z4d.shape
    m = ngr * bsz * grp
    n = w.shape[2]
    assert k == n and (ngr * grp) % tm == 0 and tm % grp == 0
    tiles = ngr * grp // tm
    return pl.pallas_call(
        _glu_kernel,
        grid=(bsz, tiles),
        in_specs=[
            pl.BlockSpec((tm // grp, None, grp, k), lambda b, i: (i, b, 0, 0)),
            pl.BlockSpec((None, k, n), lambda b, i: (layer, 0, 0), pipeline_mode=pl.Buffered(1)),
            pl.BlockSpec((1, n), lambda b, i: (0, 0)),
            pl.BlockSpec((tm, n), lambda b, i: (b * tiles + i, 0)),
        ],
        out_specs=pl.BlockSpec((tm, n), lambda b, i: (b * tiles + i, 0)),
        out_shape=jax.ShapeDtypeStruct((m, n), F32),
        scratch_shapes=[pltpu.VMEM((k, n), BF16)],
        compiler_params=_cparams(("arbitrary", "arbitrary")),
        name="glu_residual",
    )(z4d, w, b_row, res)


def _proj_res_kernel(x_ref, w_ref, res_ref, o_ref, w16_scr):
    @pl.when(pl.program_id(0) == 0)
    def _():
        w16_scr[...] = w_ref[...].astype(BF16)

    o_ref[...] = res_ref[...] + jnp.dot(x_ref[...], w16_scr[...], preferred_element_type=F32)


def _proj_residual(x, w, layer, res, tm=512):
    m, k = x.shape
    n = w.shape[2]
    assert m % tm == 0
    return pl.pallas_call(
        _proj_res_kernel,
        grid=(m // tm,),
        in_specs=[
            pl.BlockSpec((tm, k), lambda i: (i, 0)),
            pl.BlockSpec((None, k, n), lambda i: (layer, 0, 0), pipeline_mode=pl.Buffered(1)),
            pl.BlockSpec((tm, n), lambda i: (i, 0)),
        ],
        out_specs=pl.BlockSpec((tm, n), lambda i: (i, 0)),
        out_shape=jax.ShapeDtypeStruct((m, n), F32),
        scratch_shapes=[pltpu.VMEM((k, n), BF16)],
        compiler_params=_cparams(("arbitrary",)),
        name="proj_residual",
    )(x, w, res)


def _qkv_kernel(x_ref, gq_ref, gkv_ref, wq_ref, wkv_ref, o_ref, xn_scr, *, n_q_tiles, q_scale):
    j = pl.program_id(1)

    @pl.when(j == 0)
    def _():
        x = x_ref[...]
        xn_scr[...] = (x * _rms_scale(x)).astype(BF16)

    @pl.when(j < n_q_tiles)
    def _():
        q = jnp.dot(xn_scr[...], (wq_ref[...] * gq_ref[...]).astype(BF16), preferred_element_type=F32)
        o_ref[...] = (q * q_scale).astype(o_ref.dtype)

    @pl.when(j >= n_q_tiles)
    def _():
        kv = jnp.dot(xn_scr[...], (wkv_ref[...] * gkv_ref[...]).astype(BF16), preferred_element_type=F32)
        o_ref[...] = kv.astype(o_ref.dtype)


def _qkv_proj(h2d, gq_col, gkv_col, w_q, layer, w_kv, q_scale, tm=1024, tn=512):
    m, k = h2d.shape
    nq = w_q.shape[2] // tn
    nkv = w_kv.shape[1] // tn
    return pl.pallas_call(
        functools.partial(_qkv_kernel, n_q_tiles=nq, q_scale=q_scale),
        grid=(m // tm, nq + nkv),
        in_specs=[
            pl.BlockSpec((tm, k), lambda i, j: (i, 0)),
            pl.BlockSpec((k, 1), lambda i, j: (0, 0)),
            pl.BlockSpec((k, 1), lambda i, j: (0, 0)),
            pl.BlockSpec((None, k, tn), lambda i, j: (layer, 0, jnp.minimum(j, nq - 1))),
            pl.BlockSpec((k, tn), lambda i, j: (0, jnp.maximum(j - nq, 0))),
        ],
        out_specs=pl.BlockSpec((tm, tn), lambda i, j: (i, j)),
        out_shape=jax.ShapeDtypeStruct((m, (nq + nkv) * tn), BF16),
        scratch_shapes=[pltpu.VMEM((tm, k), BF16)],
        compiler_params=_cparams(("parallel", "arbitrary")),
        name="qkv_proj",
    )(h2d, gq_col, gkv_col, w_q, w_kv)


def _ffn_kernel(h_ref, g_ref, wg_ref, wu_ref, wd_ref, gfin_ref, o_ref, u_scr, *, final_norm):
    f = pl.program_id(1)

    @pl.when(f == 0)
    def _():
        x = h_ref[...]
        u_scr[...] = (x * _rms_scale(x) * g_ref[...]).astype(BF16)
        o_ref[...] = x

    u = u_scr[...]
    gate = jnp.dot(u, wg_ref[...].astype(BF16), preferred_element_type=F32)
    up = jnp.dot(u, wu_ref[...].astype(BF16), preferred_element_type=F32)
    act = (jax.nn.silu(gate) * up).astype(BF16)
    o_ref[...] += jnp.dot(act, wd_ref[...].astype(BF16), preferred_element_type=F32)

    if final_norm:
        @pl.when(f == pl.num_programs(1) - 1)
        def _():
            y = o_ref[...]
            o_ref[...] = y * _rms_scale(y) * gfin_ref[...]


def _ffn_residual(h2d, g_row, w_gate, w_up, w_down, layer, gfin_row, final_norm, tm=1024, tf=256):
    m, d = h2d.shape
    ff = w_gate.shape[2]
    return pl.pallas_call(
        functools.partial(_ffn_kernel, final_norm=final_norm),
        grid=(m // tm, ff // tf),
        in_specs=[
            pl.BlockSpec((tm, d), lambda i, f: (i, 0)),
            pl.BlockSpec((1, d), lambda i, f: (0, 0)),
            pl.BlockSpec((None, d, tf), lambda i, f: (layer, 0, f)),
            pl.BlockSpec((None, d, tf), lambda i, f: (layer, 0, f)),
            pl.BlockSpec((None, tf, d), lambda i, f: (layer, f, 0)),
            pl.BlockSpec((1, d), lambda i, f: (0, 0)),
        ],
        out_specs=pl.BlockSpec((tm, d), lambda i, f: (i, 0)),
        out_shape=jax.ShapeDtypeStruct((m, d), F32),
        scratch_shapes=[pltpu.VMEM((tm, d), BF16)],
        compiler_params=_cparams(("parallel", "arbitrary")),
        name="ffn_residual",
    )(h2d, g_row, w_gate, w_up, w_down, gfin_row)


def _moba_tables(n_heads, seq_len):
    nb = seq_len // KEY_BLOCK
    nt = MOBA_BIAS_TERMS
    assert nb + 2 * nt <= LANES
    slopes = np.exp2(-8.0 * np.arange(1, n_heads + 1, dtype=np.float64) / n_heads) * LOG2_E
    bias = slopes[:, None] * np.arange(seq_len, dtype=np.float64)[None, :]
    top = 2.0 ** math.ceil(math.log2(LOG2_E * seq_len))

    terms, rest, quantum = [], bias, top / 256.0
    for _ in range(nt - 1):
        term = np.floor(rest / quantum) * quantum
        terms.append(term)
        rest = rest - term
        quantum = quantum / 256.0
    terms.append(rest)

    qx = np.zeros((n_heads, seq_len, LANES), np.float64)
    kx = np.zeros((n_heads, seq_len, LANES), np.float64)
    kx[:, np.arange(seq_len), np.arange(seq_len) // KEY_BLOCK] = 1.0
    for c, term in enumerate(terms):
        qx[:, :, nb + c] = -term
        qx[:, :, nb + nt + c] = 1.0
        kx[:, :, nb + c] = 1.0
        kx[:, :, nb + nt + c] = term
    return jnp.asarray(qx.astype(jnp.bfloat16)), jnp.asarray(kx.astype(jnp.bfloat16))


def _moba_kernel(q_ref, k_ref, v_ref, qx_ref, kx_ref, o_ref, kmean_scr, causal_scr, qa_scr, ka_scr, va_scr,
                 s_scr, p_scr):
    kb = KEY_BLOCK
    hd = HEAD_DIM
    rc_rows = MOBA_ROW_CHUNK
    seq = k_ref.shape[1]
    nb = seq // kb
    neg_inf = -jnp.inf
    nt = (((1,), (1,)), ((), ()))

    qa_scr[:, :hd] = q_ref[0]
    ka_scr[:, :hd] = k_ref[0]
    ka_scr[:, hd:] = kx_ref[0]
    va_scr[:, :hd] = v_ref[0]
    va_scr[:, hd:] = jnp.ones((seq, hd), BF16)
    kmean_scr[...] = jnp.zeros_like(kmean_scr)
    for n in range(nb):
        kmean_scr[n:n + 1, :] = jnp.mean(k_ref[0, n * kb:(n + 1) * kb, :].astype(F32), axis=0, keepdims=True)
    row = lax.broadcasted_iota(jnp.int32, (kb, kb), 0)
    colk = lax.broadcasted_iota(jnp.int32, (kb, kb), 1)
    causal_scr[...] = jnp.where(row >= colk, 0.0, neg_inf)

    g0 = min((TOP_K_BLOCKS + 1) * kb, seq)
    qa_scr[:g0, hd:] = qx_ref[0, :g0, :]
    if seq > g0:
        ng = seq - g0
        gate = lax.dot_general(q_ref[0, g0:, :].astype(F32), kmean_scr[...], nt,
                               precision=lax.Precision.HIGHEST, preferred_element_type=F32)
        col = lax.broadcasted_iota(jnp.int32, (ng, LANES), 1)
        colf = col.astype(F32)
        own_v = lax.div(lax.broadcasted_iota(jnp.int32, (ng, LANES), 0), jnp.int32(kb)) + (TOP_K_BLOCKS + 1)
        past = col < own_v
        gate = jnp.where(past, gate, neg_inf)
        mask = jnp.where(past, MASK_BIG, 0.0)
        for _ in range(TOP_K_BLOCKS):
            mx = jnp.max(gate, axis=-1, keepdims=True)
            idx = jnp.min(jnp.where(gate == mx, colf, float(LANES)), axis=-1, keepdims=True)
            hit = colf == idx
            mask = jnp.where(hit, 0.0, mask)
            gate = jnp.where(hit, neg_inf, gate)
        qa_scr[g0:, hd:] = (qx_ref[0, g0:, :].astype(F32) + mask).astype(BF16)

    for slot, t0 in enumerate(range(0, nb, MOBA_TILE_GROUP)):
        slot = slot % 2
        tiles = range(t0, min(t0 + MOBA_TILE_GROUP, nb))
        q0 = t0 * kb
        rows = len(tiles) * kb
        width = (tiles[-1] + 1) * kb
        s_scr[slot, 0:rows, 0:width] = lax.dot_general(qa_scr[q0:q0 + rows, :], ka_scr[0:width, :], nt,
                                                       preferred_element_type=F32)
        for own in tiles:
            base = (own - t0) * kb
            for rc in range(kb // rc_rows):
                r0, r1 = base + rc * rc_rows, base + (rc + 1) * rc_rows
                parts = [s_scr[slot, r0:r1, n * kb:(n + 1) * kb] for n in range(own)]
                parts.append(s_scr[slot, r0:r1, own * kb:(own + 1) * kb] + causal_scr[r0 - base:r1 - base, :])
                m = jnp.max(functools.reduce(jnp.maximum, parts), axis=-1, keepdims=True)
                for n in range(own + 1):
                    p_scr[slot, r0:r1, n * kb:(n + 1) * kb] = jnp.exp2(parts[n] - m).astype(BF16)
            if (own + 1) * kb < width:
                p_scr[slot, base:base + kb, (own + 1) * kb:width] = jnp.zeros((kb, width - (own + 1) * kb), BF16)

        for own in tiles:
            base = (own - t0) * kb
            res = jnp.dot(p_scr[slot, base:base + kb, 0:width], va_scr[0:width, :],
                          preferred_element_type=F32)
            o_ref[0, own * kb:(own + 1) * kb, :] = (res[:, :hd] / res[:, hd:]).astype(o_ref.dtype)


def _moba_attention(qkv3d, d_model):
    b, l, _ = qkv3d.shape
    n_heads = d_model // HEAD_DIM
    kb = KEY_BLOCK
    qx, kx = _moba_tables(n_heads, l)
    return pl.pallas_call(
        _moba_kernel,
        grid=(b, n_heads),
        in_specs=[
            pl.BlockSpec((1, l, HEAD_DIM), lambda i, h: (i, 0, h)),
            pl.BlockSpec((1, l, HEAD_DIM), lambda i, h: (i, 0, n_heads + h)),
            pl.BlockSpec((1, l, HEAD_DIM), lambda i, h: (i, 0, 2 * n_heads + h)),
            pl.BlockSpec((1, l, LANES), lambda i, h: (h, 0, 0)),
            pl.BlockSpec((1, l, LANES), lambda i, h: (h, 0, 0)),
        ],
        out_specs=pl.BlockSpec((1, l, HEAD_DIM), lambda i, h: (i, 0, h)),
        out_shape=jax.ShapeDtypeStruct((b, l, d_model), BF16),
        scratch_shapes=[
            pltpu.VMEM((LANES, HEAD_DIM), F32),
            pltpu.VMEM((kb, kb), F32),
            pltpu.VMEM((l, HEAD_DIM + LANES), BF16),
            pltpu.VMEM((l, HEAD_DIM + LANES), BF16),
            pltpu.VMEM((l, 2 * HEAD_DIM), BF16),
            pltpu.VMEM((2, MOBA_TILE_GROUP * kb, l), F32),
            pltpu.VMEM((2, MOBA_TILE_GROUP * kb, l), BF16),
        ],
        compiler_params=_cparams(("parallel", "parallel")),
        name="moba_attention",
    )(qkv3d, qkv3d, qkv3d, qx, kx)


def kernel(x, g_mix, lambda_re, lambda_im, log_dt, b_re, b_im, c_re, c_im, d_skip, w_glu, b_glu, g_kv, w_kv,
           w_q, w_o, g_ffn, w_gate, w_up, w_down, g_final):
    bsz, l, d = x.shape
    depth = g_mix.shape[0]
    n_a = lambda_re.shape[0]
    m = bsz * l
    assert l % KEY_BLOCK == 0 and d % HEAD_DIM == 0 and l // KEY_BLOCK <= LANES
    row = lambda v: v.reshape(1, -1).astype(F32)

    h = x.astype(F32).reshape(m, d)
    qkv = None
    for i in range(depth):
        if i < n_a:
            tables = _s5_tables(lambda_re[i], lambda_im[i], log_dt[i], b_re[i], b_im[i], c_re[i], c_im[i])
            u = _rmsnorm_interleaved(h, row(g_mix[i]), bsz)
            z = _s5_mixer(u, row(d_skip[i]), *tables)
            h = _glu_residual(z, w_glu, i, row(b_glu[i]), h)
        else:
            j = i - n_a
            col = lambda v: v.reshape(-1, 1).astype(F32)
            qkv_i = _qkv_proj(h, col(g_mix[i]), col(g_kv), w_q, j, w_kv, HEAD_DIM ** -0.5 * LOG2_E)
            if qkv is None:
                qkv = qkv_i
            else:
                qkv = jnp.concatenate([qkv_i[:, :d], qkv[:, d:]], axis=1)
            attn = _moba_attention(qkv.reshape(bsz, l, 3 * d), d)
            h = _proj_residual(attn.reshape(m, d), w_o, j, h)
        h = _ffn_residual(h, row(g_ffn[i]), w_gate, w_up, w_down, i, row(g_final), final_norm=(i == depth - 1))
    return h.reshape(bsz, l, d).astype(x.dtype)
```

```python
import functools
import math

import jax
import jax.numpy as jnp
import numpy as np
from jax import lax
from jax.experimental import pallas as pl
from jax.experimental.pallas import tpu as pltpu

F32 = jnp.float32
BF16 = jnp.bfloat16

EPS = 1e-6
LANES = 128
HEAD_DIM = 128
KEY_BLOCK = 256
TOP_K_BLOCKS = 3
MOBA_ROW_CHUNK = 16
MOBA_TILE_GROUP = 2
MOBA_HEADS = 4
MOBA_SLOTS = 1
MOBA_BIAS_TERMS = 4
LOG2_E = math.log2(math.e)
MASK_BIG = -2.0 ** 100
SSM_CHUNK = 8
S5_GROUP = 16
VMEM_LIMIT = 56 * 1024 * 1024


def _cparams(sem):
    return pltpu.CompilerParams(dimension_semantics=sem, vmem_limit_bytes=VMEM_LIMIT)


def _rms_scale(x):
    return lax.rsqrt(jnp.mean(x * x, axis=-1, keepdims=True) + EPS)


def _rmsnorm_kernel(x_ref, g_ref, o_ref):
    x = x_ref[...]
    o_ref[...] = (x * _rms_scale(x) * g_ref[...]).reshape(o_ref.shape)


def _rmsnorm_interleaved(x2d, g_row, bsz, tm=512):
    m, d = x2d.shape
    tiles = m // bsz // tm
    return pl.pallas_call(
        _rmsnorm_kernel,
        grid=(bsz, tiles),
        in_specs=[pl.BlockSpec((tm, d), lambda b, i: (b * tiles + i, 0)), pl.BlockSpec((1, d), lambda b, i: (0, 0))],
        out_specs=pl.BlockSpec((tm // S5_GROUP, None, S5_GROUP, d), lambda b, i: (i, b, 0, 0)),
        out_shape=jax.ShapeDtypeStruct((m // bsz // S5_GROUP, bsz, S5_GROUP, d), F32),
        compiler_params=_cparams(("parallel", "parallel")),
        name="rmsnorm",
    )(x2d, g_row)


def _s5_tables(lam_re, lam_im, log_dt, b_re, b_im, c_re, c_im):
    g, p, c = b_re.shape
    t = SSM_CHUNK
    gl = LANES // c
    nj = g // gl
    ns = gl * p
    lr, li = lam_re.astype(F32), lam_im.astype(F32)
    dt = jnp.exp(log_dt.astype(F32))[:, None]
    ar, ai = lr * dt, li * dt
    mag = jnp.exp(ar)
    nr, ni = mag * jnp.cos(ai) - 1.0, mag * jnp.sin(ai)
    den = lr * lr + li * li
    fr, fi = (nr * lr + ni * li) / den, (ni * lr - nr * li) / den
    br, bi = b_re.astype(F32), b_im.astype(F32)
    bbr = fr[..., None] * br - fi[..., None] * bi
    bbi = fr[..., None] * bi + fi[..., None] * br
    dd = jnp.arange(t + 1, dtype=F32)[:, None, None]
    pmag = jnp.exp(ar[None] * dd)
    pr, pi = pmag * jnp.cos(ai[None] * dd), pmag * jnp.sin(ai[None] * dd)

    def per_tile(x):
        return jnp.transpose(x.reshape(nj, gl, c, p), (0, 2, 1, 3)).reshape(nj, c, ns)

    bcmp = jnp.concatenate([per_tile(jnp.swapaxes(bbr, 1, 2)), per_tile(jnp.swapaxes(bbi, 1, 2))], axis=-1)
    ccmp = jnp.concatenate([per_tile(c_re.astype(F32)), per_tile(c_im.astype(F32))], axis=-1)
    pw = jnp.concatenate([pr.reshape(t + 1, nj, ns), pi.reshape(t + 1, nj, ns)], axis=-1)
    pw = jnp.pad(jnp.swapaxes(pw, 0, 1), ((0, 0), (0, 16 - (t + 1)), (0, 0)))
    return bcmp, ccmp, pw


def _s5_build_tables(bcmp_ref, ccmp_ref, pw_ref, w1_scr, w2_scr):
    t = SSM_CHUNK
    cg = bcmp_ref.shape[1]
    ns = bcmp_ref.shape[2] // 2
    sg = ns * cg // LANES
    yw = t * LANES
    same_group = (lax.broadcasted_iota(jnp.int32, (LANES, ns), 0) // cg
                  == lax.broadcasted_iota(jnp.int32, (LANES, ns), 1) // sg)

    def block_diag(x):
        return jnp.where(same_group, jnp.tile(x, (LANES // cg, 1)), 0.0)

    bre, bim = bcmp_ref[0, :, :ns], bcmp_ref[0, :, ns:]
    cre, cim = ccmp_ref[0, :, :ns], ccmp_ref[0, :, ns:]
    bcat16 = jnp.concatenate([block_diag(bre), block_diag(bim)], axis=1).astype(BF16)
    nt = (((1,), (1,)), ((), ()))
    for d in range(t + 1):
        pr, pi = pw_ref[0, d:d + 1, :ns], pw_ref[0, d:d + 1, ns:]
        mo = jnp.concatenate([block_diag(cre * pr - cim * pi), block_diag(-(cre * pi + cim * pr))],
                             axis=1).astype(BF16)
        if d >= 1:
            w2_scr[(d - 1) * LANES:d * LANES, :] = mo
        if d < t:
            kd = lax.dot_general(bcat16, mo, nt, preferred_element_type=F32).astype(BF16)
            for ip in range(t - d):
                w1_scr[ip * LANES:(ip + 1) * LANES, (ip + d) * LANES:(ip + d + 1) * LANES] = kd
            ip = t - 1 - d
            w1_scr[ip * LANES:(ip + 1) * LANES, yw:yw + ns] = block_diag(bre * pr - bim * pi).astype(BF16)
            w1_scr[ip * LANES:(ip + 1) * LANES, yw + ns:] = block_diag(bre * pi + bim * pr).astype(BF16)
    for ip in range(1, t):
        w1_scr[ip * LANES:(ip + 1) * LANES, 0:ip * LANES] = jnp.zeros((LANES, ip * LANES), BF16)


def _s5_kernel(u_ref, d_ref, bcmp_ref, ccmp_ref, pw_ref, z_ref, w1_scr, w2_scr, lhs_scr, s_scr, yc_scr, y_scr,
               *, n_seq):
    t = SSM_CHUNK
    cpg = S5_GROUP // t
    rows = lhs_scr.shape[0]
    ngroups = rows // (n_seq * cpg)
    nsl = s_scr.shape[0] // 2
    yw = t * LANES

    _s5_build_tables(bcmp_ref, ccmp_ref, pw_ref, w1_scr, w2_scr)

    for i in range(t):
        lhs_scr[:, i * LANES:(i + 1) * LANES] = u_ref[pl.ds(i, rows, stride=t), :].astype(BF16)
    strip = 2 * LANES
    for c0 in range(0, yw, strip):
        c1 = c0 + strip
        yc_scr[:, c0:c1] = jnp.dot(lhs_scr[:, :c1], w1_scr[:c1, c0:c1], preferred_element_type=F32)
    s_in_all = jnp.dot(lhs_scr[...], w1_scr[:, yw:], preferred_element_type=F32)
    for q in range(2 * nsl):
        s_scr[q] = s_in_all[:, q * LANES:(q + 1) * LANES]

    a_re = [pw_ref[0, t:t + 1, q * LANES:(q + 1) * LANES] for q in range(nsl)]
    a_im = [pw_ref[0, t:t + 1, (nsl + q) * LANES:(nsl + q + 1) * LANES] for q in range(nsl)]

    def scan_group(g, carry):
        e = list(carry)
        for c in range(cpg):
            sel = pl.ds(g * (n_seq * cpg) + c, n_seq, stride=cpg)
            for q in range(nsl):
                e_re, e_im = e[q], e[nsl + q]
                in_re, in_im = s_scr[q, sel, :], s_scr[nsl + q, sel, :]
                s_scr[q, sel, :] = e_re
                s_scr[nsl + q, sel, :] = e_im
                e[q] = a_re[q] * e_re - a_im[q] * e_im + in_re
                e[nsl + q] = a_re[q] * e_im + a_im[q] * e_re + in_im
        return tuple(e)

    lax.fori_loop(0, ngroups, scan_group, (jnp.zeros((n_seq, LANES), F32),) * (2 * nsl))

    s_prev = jnp.concatenate([s_scr[q].astype(BF16) for q in range(2 * nsl)], axis=1)
    y = yc_scr[...] + lax.dot_general(s_prev, w2_scr[...], (((1,), (1,)), ((), ())), preferred_element_type=F32)
    for i in range(t):
        y_scr[pl.ds(i, rows, stride=t), :] = y[:, i * LANES:(i + 1) * LANES]

    yy = y_scr[...] + d_ref[...] * u_ref[...]
    z_ref[...] = jax.nn.gelu(yy).astype(z_ref.dtype)


def _s5_mixer(u4d, d_row, bcmp, ccmp, pw):
    ngr, bsz, grp, d = u4d.shape
    m = ngr * bsz * grp
    nj = d // LANES
    rows = m // SSM_CHUNK
    cg, ns2 = bcmp.shape[1:]
    yw = SSM_CHUNK * LANES
    z = pl.pallas_call(
        functools.partial(_s5_kernel, n_seq=bsz),
        grid=(nj,),
        in_specs=[
            pl.BlockSpec((m, LANES), lambda j: (0, j)),
            pl.BlockSpec((1, LANES), lambda j: (0, j)),
            pl.BlockSpec((1, cg, ns2), lambda j: (j, 0, 0)),
            pl.BlockSpec((1, cg, ns2), lambda j: (j, 0, 0)),
            pl.BlockSpec((1, 16, ns2), lambda j: (j, 0, 0)),
        ],
        out_specs=pl.BlockSpec((m, LANES), lambda j: (0, j)),
        out_shape=jax.ShapeDtypeStruct((m, d), BF16),
        scratch_shapes=[
            pltpu.VMEM((yw, yw + ns2), BF16),
            pltpu.VMEM((yw, ns2), BF16),
            pltpu.VMEM((rows, yw), BF16),
            pltpu.VMEM((ns2 // LANES, rows, LANES), F32),
            pltpu.VMEM((rows, yw), F32),
            pltpu.VMEM((m, LANES), F32),
        ],
        compiler_params=_cparams(("parallel",)),
        name="s5_mixer",
    )(u4d.reshape(m, d), d_row, bcmp, ccmp, pw)
    return z.reshape(ngr, bsz, grp, d)


def _glu_kernel(z_ref, w_ref, b_ref, res_ref, o_ref, w16_scr):
    tm, n = o_ref.shape

    @pl.when((pl.program_id(0) == 0) & (pl.program_id(1) == 0))
    def _():
        w16_scr[...] = w_ref[...].astype(BF16)

    z = z_ref[...].reshape(tm, n)
    a = jnp.dot(z, w16_scr[...], preferred_element_type=F32) + b_ref[...]
    o_ref[...] = res_ref[...] + z.astype(F32) * jax.nn.sigmoid(a)


def _glu_residual(z4d, w, layer, b_row, res, tm=512):
    ngr, bsz, grp, k = z4d.shape
    m = ngr * bsz * grp
    n = w.shape[2]
    assert k == n and (ngr * grp) % tm == 0 and tm % grp == 0
    tiles = ngr * grp // tm
    return pl.pallas_call(
        _glu_kernel,
        grid=(bsz, tiles),
        in_specs=[
            pl.BlockSpec((tm // grp, None, grp, k), lambda b, i: (i, b, 0, 0)),
            pl.BlockSpec((None, k, n), lambda b, i: (layer, 0, 0), pipeline_mode=pl.Buffered(1)),
            pl.BlockSpec((1, n), lambda b, i: (0, 0)),
            pl.BlockSpec((tm, n), lambda b, i: (b * tiles + i, 0)),
        ],
        out_specs=pl.BlockSpec((tm, n), lambda b, i: (b * tiles + i, 0)),
        out_shape=jax.ShapeDtypeStruct((m, n), F32),
        scratch_shapes=[pltpu.VMEM((k, n), BF16)],
        compiler_params=_cparams(("arbitrary", "arbitrary")),
        name="glu_residual",
    )(z4d, w, b_row, res)


def _proj_res_kernel(x_ref, w_ref, res_ref, o_ref, w16_scr):
    @pl.when(pl.program_id(0) == 0)
    def _():
        w16_scr[...] = w_ref[...].astype(BF16)

    o_ref[...] = res_ref[...] + jnp.dot(x_ref[...], w16_scr[...], preferred_element_type=F32)


def _proj_residual(x, w, layer, res, tm=512):
    m, k = x.shape
    n = w.shape[2]
    assert m % tm == 0
    return pl.pallas_call(
        _proj_res_kernel,
        grid=(m // tm,),
        in_specs=[
            pl.BlockSpec((tm, k), lambda i: (i, 0)),
            pl.BlockSpec((None, k, n), lambda i: (layer, 0, 0), pipeline_mode=pl.Buffered(1)),
            pl.BlockSpec((tm, n), lambda i: (i, 0)),
        ],
        out_specs=pl.BlockSpec((tm, n), lambda i: (i, 0)),
        out_shape=jax.ShapeDtypeStruct((m, n), F32),
        scratch_shapes=[pltpu.VMEM((k, n), BF16)],
        compiler_params=_cparams(("arbitrary",)),
        name="proj_residual",
    )(x, w, res)


def _qkv_kernel(x_ref, gq_ref, gkv_ref, wq_ref, wkv_ref, o_ref, xn_scr, *, n_q_tiles, q_scale):
    j = pl.program_id(1)

    @pl.when(j == 0)
    def _():
        x = x_ref[...]
        xn_scr[...] = (x * _rms_scale(x)).astype(BF16)

    @pl.when(j < n_q_tiles)
    def _():
        q = jnp.dot(xn_scr[...], (wq_ref[...] * gq_ref[...]).astype(BF16), preferred_element_type=F32)
        o_ref[...] = (q * q_scale).astype(o_ref.dtype)

    @pl.when(j >= n_q_tiles)
    def _():
        kv = jnp.dot(xn_scr[...], (wkv_ref[...] * gkv_ref[...]).astype(BF16), preferred_element_type=F32)
        o_ref[...] = kv.astype(o_ref.dtype)


def _qkv_proj(h2d, gq_col, gkv_col, w_q, layer, w_kv, q_scale, tm=1024, tn=512):
    m, k = h2d.shape
    nq = w_q.shape[2] // tn
    nkv = w_kv.shape[1] // tn
    return pl.pallas_call(
        functools.partial(_qkv_kernel, n_q_tiles=nq, q_scale=q_scale),
        grid=(m // tm, nq + nkv),
        in_specs=[
            pl.BlockSpec((tm, k), lambda i, j: (i, 0)),
            pl.BlockSpec((k, 1), lambda i, j: (0, 0)),
            pl.BlockSpec((k, 1), lambda i, j: (0, 0)),
            pl.BlockSpec((None, k, tn), lambda i, j: (layer, 0, jnp.minimum(j, nq - 1))),
            pl.BlockSpec((k, tn), lambda i, j: (0, jnp.maximum(j - nq, 0))),
        ],
        out_specs=pl.BlockSpec((tm, tn), lambda i, j: (i, j)),
        out_shape=jax.ShapeDtypeStruct((m, (nq + nkv) * tn), BF16),
        scratch_shapes=[pltpu.VMEM((tm, k), BF16)],
        compiler_params=_cparams(("parallel", "arbitrary")),
        name="qkv_proj",
    )(h2d, gq_col, gkv_col, w_q, w_kv)


def _ffn_kernel(h_ref, g_ref, wg_ref, wu_ref, wd_ref, gfin_ref, o_ref, u_scr, *, final_norm):
    f = pl.program_id(1)

    @pl.when(f == 0)
    def _():
        x = h_ref[...]
        u_scr[...] = (x * _rms_scale(x) * g_ref[...]).astype(BF16)
        o_ref[...] = x

    u = u_scr[...]
    gate = jnp.dot(u, wg_ref[...].astype(BF16), preferred_element_type=F32)
    up = jnp.dot(u, wu_ref[...].astype(BF16), preferred_element_type=F32)
    act = (jax.nn.silu(gate) * up).astype(BF16)
    o_ref[...] += jnp.dot(act, wd_ref[...].astype(BF16), preferred_element_type=F32)

    if final_norm:
        @pl.when(f == pl.num_programs(1) - 1)
        def _():
            y = o_ref[...]
            o_ref[...] = y * _rms_scale(y) * gfin_ref[...]


def _ffn_residual(h2d, g_row, w_gate, w_up, w_down, layer, gfin_row, final_norm, tm=1024, tf=256):
    m, d = h2d.shape
    ff = w_gate.shape[2]
    return pl.pallas_call(
        functools.partial(_ffn_kernel, final_norm=final_norm),
        grid=(m // tm, ff // tf),
        in_specs=[
            pl.BlockSpec((tm, d), lambda i, f: (i, 0)),
            pl.BlockSpec((1, d), lambda i, f: (0, 0)),
            pl.BlockSpec((None, d, tf), lambda i, f: (layer, 0, f)),
            pl.BlockSpec((None, d, tf), lambda i, f: (layer, 0, f)),
            pl.BlockSpec((None, tf, d), lambda i, f: (layer, f, 0)),
            pl.BlockSpec((1, d), lambda i, f: (0, 0)),
        ],
        out_specs=pl.BlockSpec((tm, d), lambda i, f: (i, 0)),
        out_shape=jax.ShapeDtypeStruct((m, d), F32),
        scratch_shapes=[pltpu.VMEM((tm, d), BF16)],
        compiler_params=_cparams(("parallel", "arbitrary")),
        name="ffn_residual",
    )(h2d, g_row, w_gate, w_up, w_down, gfin_row)


def _moba_tables(n_heads, seq_len):
    nb = seq_len // KEY_BLOCK
    nt = MOBA_BIAS_TERMS
    assert nb + 2 * nt <= LANES
    slopes = np.exp2(-8.0 * np.arange(1, n_heads + 1, dtype=np.float64) / n_heads) * LOG2_E
    bias = slopes[:, None] * np.arange(seq_len, dtype=np.float64)[None, :]
    top = 2.0 ** math.ceil(math.log2(LOG2_E * seq_len))

    terms, rest, quantum = [], bias, top / 256.0
    for _ in range(nt - 1):
        term = np.floor(rest / quantum) * quantum
        terms.append(term)
        rest = rest - term
        quantum = quantum / 256.0
    terms.append(rest)

    qx = np.zeros((n_heads, seq_len, LANES), np.float64)
    kx = np.zeros((n_heads, seq_len, LANES), np.float64)
    kx[:, np.arange(seq_len), np.arange(seq_len) // KEY_BLOCK] = 1.0
    for c, term in enumerate(terms):
        qx[:, :, nb + c] = -term
        qx[:, :, nb + nt + c] = 1.0
        kx[:, :, nb + c] = 1.0
        kx[:, :, nb + nt + c] = term
    return jnp.asarray(qx.astype(jnp.bfloat16)), jnp.asarray(kx.astype(jnp.bfloat16))


def _moba_kernel(q_ref, k_ref, v_ref, qx_ref, kx_ref, o_ref, kmean_scr, causal_scr, qa_scr, ka_scr, va_scr,
                 s_scr, p_scr):
    kb = KEY_BLOCK
    hd = HEAD_DIM
    rc_rows = MOBA_ROW_CHUNK
    seq = k_ref.shape[1]
    nb = seq // kb
    nh = qx_ref.shape[0]
    neg_inf = -jnp.inf
    nt = (((1,), (1,)), ((), ()))

    row = lax.broadcasted_iota(jnp.int32, (kb, kb), 0)
    colk = lax.broadcasted_iota(jnp.int32, (kb, kb), 1)
    causal_scr[...] = jnp.where(row >= colk, 0.0, neg_inf)
    for hh in range(nh):
        _moba_prepare(hh, q_ref, k_ref, v_ref, qx_ref, kx_ref, kmean_scr, qa_scr, ka_scr, va_scr)

    for slot, t0 in enumerate(range(0, nb, MOBA_TILE_GROUP)):
        slot = slot % s_scr.shape[1]
        tiles = range(t0, min(t0 + MOBA_TILE_GROUP, nb))
        q0 = t0 * kb
        rows = len(tiles) * kb
        width = (tiles[-1] + 1) * kb
        for hh in range(nh):
            s_scr[hh, slot, 0:rows, 0:width] = lax.dot_general(
                qa_scr[hh, q0:q0 + rows, :], ka_scr[hh, 0:width, :], nt, preferred_element_type=F32)
        for hh in range(nh):
            for own in tiles:
                base = (own - t0) * kb
                for rc in range(kb // rc_rows):
                    r0, r1 = base + rc * rc_rows, base + (rc + 1) * rc_rows
                    parts = [s_scr[hh, slot, r0:r1, n * kb:(n + 1) * kb] for n in range(own)]
                    parts.append(s_scr[hh, slot, r0:r1, own * kb:(own + 1) * kb]
                                 + causal_scr[r0 - base:r1 - base, :])
                    m = jnp.max(functools.reduce(jnp.maximum, parts), axis=-1, keepdims=True)
                    for n in range(own + 1):
                        p_scr[hh, slot, r0:r1, n * kb:(n + 1) * kb] = jnp.exp2(parts[n] - m).astype(BF16)
                if (own + 1) * kb < width:
                    p_scr[hh, slot, base:base + kb, (own + 1) * kb:width] = jnp.zeros(
                        (kb, width - (own + 1) * kb), BF16)
        for hh in range(nh):
            for own in tiles:
                base = (own - t0) * kb
                res = jnp.dot(p_scr[hh, slot, base:base + kb, 0:width], va_scr[hh, 0:width, :],
                              preferred_element_type=F32)
                o_ref[0, own * kb:(own + 1) * kb, hh * hd:(hh + 1) * hd] = (
                    res[:, :hd] / res[:, hd:]).astype(o_ref.dtype)


def _moba_prepare(hh, q_ref, k_ref, v_ref, qx_ref, kx_ref, kmean_scr, qa_scr, ka_scr, va_scr):
    kb = KEY_BLOCK
    hd = HEAD_DIM
    seq = k_ref.shape[1]
    nb = seq // kb
    neg_inf = -jnp.inf
    nt = (((1,), (1,)), ((), ()))
    cols = slice(hh * hd, (hh + 1) * hd)

    qa_scr[hh, :, :hd] = q_ref[0, :, cols]
    ka_scr[hh, :, :hd] = k_ref[0, :, cols]
    ka_scr[hh, :, hd:] = kx_ref[hh]
    va_scr[hh, :, :hd] = v_ref[0, :, cols]
    va_scr[hh, :, hd:] = jnp.ones((seq, hd), BF16)
    kmean_scr[hh] = jnp.zeros(kmean_scr.shape[1:], F32)
    for n in range(nb):
        kmean_scr[hh, n:n + 1, :] = jnp.mean(k_ref[0, n * kb:(n + 1) * kb, cols].astype(F32), axis=0, keepdims=True)

    g0 = min((TOP_K_BLOCKS + 1) * kb, seq)
    qa_scr[hh, :g0, hd:] = qx_ref[hh, :g0, :]
    if seq > g0:
        ng = seq - g0
        gate = lax.dot_general(q_ref[0, g0:, cols].astype(F32), kmean_scr[hh], nt,
                               precision=lax.Precision.HIGHEST, preferred_element_type=F32)
        col = lax.broadcasted_iota(jnp.int32, (ng, LANES), 1)
        colf = col.astype(F32)
        own_v = lax.div(lax.broadcasted_iota(jnp.int32, (ng, LANES), 0), jnp.int32(kb)) + (TOP_K_BLOCKS + 1)
        past = col < own_v
        gate = jnp.where(past, gate, neg_inf)
        mask = jnp.where(past, MASK_BIG, 0.0)
        for _ in range(TOP_K_BLOCKS):
            mx = jnp.max(gate, axis=-1, keepdims=True)
            idx = jnp.min(jnp.where(gate == mx, colf, float(LANES)), axis=-1, keepdims=True)
            hit = colf == idx
            mask = jnp.where(hit, 0.0, mask)
            gate = jnp.where(hit, neg_inf, gate)
        qa_scr[hh, g0:, hd:] = (qx_ref[hh, g0:, :].astype(F32) + mask).astype(BF16)


def _moba_attention(qkv3d, d_model):
    b, l, _ = qkv3d.shape
    n_heads = d_model // HEAD_DIM
    kb = KEY_BLOCK
    nh = math.gcd(n_heads, MOBA_HEADS)
    wide = nh * HEAD_DIM
    qx, kx = _moba_tables(n_heads, l)
    return pl.pallas_call(
        _moba_kernel,
        grid=(b, n_heads // nh),
        in_specs=[
            pl.BlockSpec((1, l, wide), lambda i, h: (i, 0, h)),
            pl.BlockSpec((1, l, wide), lambda i, h: (i, 0, n_heads // nh + h)),
            pl.BlockSpec((1, l, wide), lambda i, h: (i, 0, 2 * (n_heads // nh) + h)),
            pl.BlockSpec((nh, l, LANES), lambda i, h: (h, 0, 0)),
            pl.BlockSpec((nh, l, LANES), lambda i, h: (h, 0, 0)),
        ],
        out_specs=pl.BlockSpec((1, l, wide), lambda i, h: (i, 0, h)),
        out_shape=jax.ShapeDtypeStruct((b, l, d_model), BF16),
        scratch_shapes=[
            pltpu.VMEM((nh, LANES, HEAD_DIM), F32),
            pltpu.VMEM((kb, kb), F32),
            pltpu.VMEM((nh, l, HEAD_DIM + LANES), BF16),
            pltpu.VMEM((nh, l, HEAD_DIM + LANES), BF16),
            pltpu.VMEM((nh, l, 2 * HEAD_DIM), BF16),
            pltpu.VMEM((nh, MOBA_SLOTS, MOBA_TILE_GROUP * kb, l), F32),
            pltpu.VMEM((nh, MOBA_SLOTS, MOBA_TILE_GROUP * kb, l), BF16),
        ],
        compiler_params=_cparams(("parallel", "parallel")),
        name="moba_attention",
    )(qkv3d, qkv3d, qkv3d, qx, kx)


def kernel(x, g_mix, lambda_re, lambda_im, log_dt, b_re, b_im, c_re, c_im, d_skip, w_glu, b_glu, g_kv, w_kv,
           w_q, w_o, g_ffn, w_gate, w_up, w_down, g_final):
    bsz, l, d = x.shape
    depth = g_mix.shape[0]
    n_a = lambda_re.shape[0]
    m = bsz * l
    assert l % KEY_BLOCK == 0 and d % HEAD_DIM == 0 and l // KEY_BLOCK <= LANES
    row = lambda v: v.reshape(1, -1).astype(F32)

    h = x.astype(F32).reshape(m, d)
    qkv = None
    for i in range(depth):
        if i < n_a:
            tables = _s5_tables(lambda_re[i], lambda_im[i], log_dt[i], b_re[i], b_im[i], c_re[i], c_im[i])
            u = _rmsnorm_interleaved(h, row(g_mix[i]), bsz)
            z = _s5_mixer(u, row(d_skip[i]), *tables)
            h = _glu_residual(z, w_glu, i, row(b_glu[i]), h)
        else:
            j = i - n_a
            col = lambda v: v.reshape(-1, 1).astype(F32)
            qkv_i = _qkv_proj(h, col(g_mix[i]), col(g_kv), w_q, j, w_kv, HEAD_DIM ** -0.5 * LOG2_E)
            if qkv is None:
                qkv = qkv_i
            else:
                qkv = jnp.concatenate([qkv_i[:, :d], qkv[:, d:]], axis=1)
            attn = _moba_attention(qkv.reshape(bsz, l, 3 * d), d)
            h = _proj_residual(attn.reshape(m, d), w_o, j, h)
        h = _ffn_residual(h, row(g_ffn[i]), w_gate, w_up, w_down, i, row(g_final), final_norm=(i == depth - 1))
    return h.reshape(bsz, l, d).astype(x.dtype)
```

```python
import functools
import math

import jax
import jax.numpy as jnp
import numpy as np
from jax import lax
from jax.experimental import pallas as pl
from jax.experimental.pallas import tpu as pltpu

F32 = jnp.float32
BF16 = jnp.bfloat16

EPS = 1e-6
LANES = 128
HEAD_DIM = 128
KEY_BLOCK = 256
TOP_K_BLOCKS = 3
MOBA_ROW_CHUNK = 16
MOBA_TILE_GROUP = 2
MOBA_HEADS = 4
MOBA_SLOTS = 1
MOBA_BIAS_TERMS = 4
LOG2_E = math.log2(math.e)
MASK_BIG = -2.0 ** 100
SSM_CHUNK = 8
S5_GROUP = 16
V7X_VMEM_BYTES = 64 * 1024 * 1024
VMEM_LIMIT = V7X_VMEM_BYTES // 8 * 7


def _cparams(sem):
    return pltpu.CompilerParams(dimension_semantics=sem, vmem_limit_bytes=VMEM_LIMIT)


def _rms_scale(x):
    return lax.rsqrt(jnp.mean(x * x, axis=-1, keepdims=True) + EPS)


def _rmsnorm_kernel(x_ref, g_ref, o_ref):
    x = x_ref[...]
    o_ref[...] = (x * _rms_scale(x) * g_ref[...]).reshape(o_ref.shape)


def _rmsnorm_interleaved(x2d, g_row, bsz, tm=512):
    m, d = x2d.shape
    assert m % (bsz * tm) == 0 and tm % S5_GROUP == 0
    tiles = m // bsz // tm
    return pl.pallas_call(
        _rmsnorm_kernel,
        grid=(bsz, tiles),
        in_specs=[pl.BlockSpec((tm, d), lambda b, i: (b * tiles + i, 0)), pl.BlockSpec((1, d), lambda b, i: (0, 0))],
        out_specs=pl.BlockSpec((tm // S5_GROUP, None, S5_GROUP, d), lambda b, i: (i, b, 0, 0)),
        out_shape=jax.ShapeDtypeStruct((m // bsz // S5_GROUP, bsz, S5_GROUP, d), F32),
        compiler_params=_cparams(("parallel", "parallel")),
        name="rmsnorm",
    )(x2d, g_row)


def _s5_tables(lam_re, lam_im, log_dt, b_re, b_im, c_re, c_im):
    g, p, c = b_re.shape
    t = SSM_CHUNK
    gl = LANES // c
    nj = g // gl
    ns = gl * p
    lr, li = lam_re.astype(F32), lam_im.astype(F32)
    dt = jnp.exp(log_dt.astype(F32))[:, None]
    ar, ai = lr * dt, li * dt
    mag = jnp.exp(ar)
    nr, ni = mag * jnp.cos(ai) - 1.0, mag * jnp.sin(ai)
    den = lr * lr + li * li
    fr, fi = (nr * lr + ni * li) / den, (ni * lr - nr * li) / den
    br, bi = b_re.astype(F32), b_im.astype(F32)
    bbr = fr[..., None] * br - fi[..., None] * bi
    bbi = fr[..., None] * bi + fi[..., None] * br
    dd = jnp.arange(t + 1, dtype=F32)[:, None, None]
    pmag = jnp.exp(ar[None] * dd)
    pr, pi = pmag * jnp.cos(ai[None] * dd), pmag * jnp.sin(ai[None] * dd)

    def per_tile(x):
        return jnp.transpose(x.reshape(nj, gl, c, p), (0, 2, 1, 3)).reshape(nj, c, ns)

    bcmp = jnp.concatenate([per_tile(jnp.swapaxes(bbr, 1, 2)), per_tile(jnp.swapaxes(bbi, 1, 2))], axis=-1)
    ccmp = jnp.concatenate([per_tile(c_re.astype(F32)), per_tile(c_im.astype(F32))], axis=-1)
    pw = jnp.concatenate([pr.reshape(t + 1, nj, ns), pi.reshape(t + 1, nj, ns)], axis=-1)
    pw = jnp.pad(jnp.swapaxes(pw, 0, 1), ((0, 0), (0, 16 - (t + 1)), (0, 0)))
    return bcmp, ccmp, pw


def _s5_build_tables(bcmp_ref, ccmp_ref, pw_ref, w1_scr, w2_scr):
    t = SSM_CHUNK
    cg = bcmp_ref.shape[1]
    ns = bcmp_ref.shape[2] // 2
    sg = ns * cg // LANES
    yw = t * LANES
    same_group = (lax.broadcasted_iota(jnp.int32, (LANES, ns), 0) // cg
                  == lax.broadcasted_iota(jnp.int32, (LANES, ns), 1) // sg)

    def block_diag(x):
        return jnp.where(same_group, jnp.tile(x, (LANES // cg, 1)), 0.0)

    bre, bim = bcmp_ref[0, :, :ns], bcmp_ref[0, :, ns:]
    cre, cim = ccmp_ref[0, :, :ns], ccmp_ref[0, :, ns:]
    bcat16 = jnp.concatenate([block_diag(bre), block_diag(bim)], axis=1).astype(BF16)
    nt = (((1,), (1,)), ((), ()))
    for d in range(t + 1):
        pr, pi = pw_ref[0, d:d + 1, :ns], pw_ref[0, d:d + 1, ns:]
        mo = jnp.concatenate([block_diag(cre * pr - cim * pi), block_diag(-(cre * pi + cim * pr))],
                             axis=1).astype(BF16)
        if d >= 1:
            w2_scr[(d - 1) * LANES:d * LANES, :] = mo
        if d < t:
            kd = lax.dot_general(bcat16, mo, nt, preferred_element_type=F32).astype(BF16)
            for ip in range(t - d):
                w1_scr[ip * LANES:(ip + 1) * LANES, (ip + d) * LANES:(ip + d + 1) * LANES] = kd
            ip = t - 1 - d
            w1_scr[ip * LANES:(ip + 1) * LANES, yw:yw + ns] = block_diag(bre * pr - bim * pi).astype(BF16)
            w1_scr[ip * LANES:(ip + 1) * LANES, yw + ns:] = block_diag(bre * pi + bim * pr).astype(BF16)
    for ip in range(1, t):
        w1_scr[ip * LANES:(ip + 1) * LANES, 0:ip * LANES] = jnp.zeros((LANES, ip * LANES), BF16)


def _s5_kernel(u_ref, d_ref, bcmp_ref, ccmp_ref, pw_ref, z_ref, w1_scr, w2_scr, lhs_scr, s_scr, yc_scr, y_scr,
               *, n_seq):
    t = SSM_CHUNK
    cpg = S5_GROUP // t
    rows = lhs_scr.shape[0]
    ngroups = rows // (n_seq * cpg)
    nsl = s_scr.shape[0] // 2
    yw = t * LANES

    _s5_build_tables(bcmp_ref, ccmp_ref, pw_ref, w1_scr, w2_scr)

    for i in range(t):
        lhs_scr[:, i * LANES:(i + 1) * LANES] = u_ref[pl.ds(i, rows, stride=t), :].astype(BF16)
    strip = 2 * LANES
    for c0 in range(0, yw, strip):
        c1 = c0 + strip
        yc_scr[:, c0:c1] = jnp.dot(lhs_scr[:, :c1], w1_scr[:c1, c0:c1], preferred_element_type=F32)
    s_in_all = jnp.dot(lhs_scr[...], w1_scr[:, yw:], preferred_element_type=F32)
    for q in range(2 * nsl):
        s_scr[q] = s_in_all[:, q * LANES:(q + 1) * LANES]

    a_re = [pw_ref[0, t:t + 1, q * LANES:(q + 1) * LANES] for q in range(nsl)]
    a_im = [pw_ref[0, t:t + 1, (nsl + q) * LANES:(nsl + q + 1) * LANES] for q in range(nsl)]

    def scan_group(g, carry):
        e = list(carry)
        for c in range(cpg):
            sel = pl.ds(g * (n_seq * cpg) + c, n_seq, stride=cpg)
            for q in range(nsl):
                e_re, e_im = e[q], e[nsl + q]
                in_re, in_im = s_scr[q, sel, :], s_scr[nsl + q, sel, :]
                s_scr[q, sel, :] = e_re
                s_scr[nsl + q, sel, :] = e_im
                e[q] = a_re[q] * e_re - a_im[q] * e_im + in_re
                e[nsl + q] = a_re[q] * e_im + a_im[q] * e_re + in_im
        return tuple(e)

    lax.fori_loop(0, ngroups, scan_group, (jnp.zeros((n_seq, LANES), F32),) * (2 * nsl))

    s_prev = jnp.concatenate([s_scr[q].astype(BF16) for q in range(2 * nsl)], axis=1)
    y = yc_scr[...] + lax.dot_general(s_prev, w2_scr[...], (((1,), (1,)), ((), ())), preferred_element_type=F32)
    for i in range(t):
        y_scr[pl.ds(i, rows, stride=t), :] = y[:, i * LANES:(i + 1) * LANES]

    yy = y_scr[...] + d_ref[...] * u_ref[...]
    z_ref[...] = jax.nn.gelu(yy).astype(z_ref.dtype)


def _s5_mixer(u4d, d_row, bcmp, ccmp, pw):
    ngr, bsz, grp, d = u4d.shape
    m = ngr * bsz * grp
    assert grp == S5_GROUP and d % LANES == 0
    nj = d // LANES
    rows = m // SSM_CHUNK
    cg, ns2 = bcmp.shape[1:]
    yw = SSM_CHUNK * LANES
    z = pl.pallas_call(
        functools.partial(_s5_kernel, n_seq=bsz),
        grid=(nj,),
        in_specs=[
            pl.BlockSpec((m, LANES), lambda j: (0, j)),
            pl.BlockSpec((1, LANES), lambda j: (0, j)),
            pl.BlockSpec((1, cg, ns2), lambda j: (j, 0, 0)),
            pl.BlockSpec((1, cg, ns2), lambda j: (j, 0, 0)),
            pl.BlockSpec((1, 16, ns2), lambda j: (j, 0, 0)),
        ],
        out_specs=pl.BlockSpec((m, LANES), lambda j: (0, j)),
        out_shape=jax.ShapeDtypeStruct((m, d), BF16),
        scratch_shapes=[
            pltpu.VMEM((yw, yw + ns2), BF16),
            pltpu.VMEM((yw, ns2), BF16),
            pltpu.VMEM((rows, yw), BF16),
            pltpu.VMEM((ns2 // LANES, rows, LANES), F32),
            pltpu.VMEM((rows, yw), F32),
            pltpu.VMEM((m, LANES), F32),
        ],
        compiler_params=_cparams(("parallel",)),
        name="s5_mixer",
    )(u4d.reshape(m, d), d_row, bcmp, ccmp, pw)
    return z.reshape(ngr, bsz, grp, d)


def _glu_kernel(z_ref, w_ref, b_ref, res_ref, o_ref, w16_scr):
    tm, n = o_ref.shape

    @pl.when((pl.program_id(0) == 0) & (pl.program_id(1) == 0))
    def _():
        w16_scr[...] = w_ref[...].astype(BF16)

    z = z_ref[...].reshape(tm, n)
    a = jnp.dot(z, w16_scr[...], preferred_element_type=F32) + b_ref[...]
    o_ref[...] = res_ref[...] + z.astype(F32) * jax.nn.sigmoid(a)


def _glu_residual(z4d, w, layer, b_row, res, tm=512):
    ngr, bsz, grp, k = z4d.shape
    m = ngr * bsz * grp
    n = w.shape[2]
    assert k == n and (ngr * grp) % tm == 0 and tm % grp == 0
    tiles = ngr * grp // tm
    return pl.pallas_call(
        _glu_kernel,
        grid=(bsz, tiles),
        in_specs=[
            pl.BlockSpec((tm // grp, None, grp, k), lambda b, i: (i, b, 0, 0)),
            pl.BlockSpec((None, k, n), lambda b, i: (layer, 0, 0), pipeline_mode=pl.Buffered(1)),
            pl.BlockSpec((1, n), lambda b, i: (0, 0)),
            pl.BlockSpec((tm, n), lambda b, i: (b * tiles + i, 0)),
        ],
        out_specs=pl.BlockSpec((tm, n), lambda b, i: (b * tiles + i, 0)),
        out_shape=jax.ShapeDtypeStruct((m, n), F32),
        scratch_shapes=[pltpu.VMEM((k, n), BF16)],
        compiler_params=_cparams(("arbitrary", "arbitrary")),
        name="glu_residual",
    )(z4d, w, b_row, res)


def _proj_res_kernel(x_ref, w_ref, res_ref, o_ref, w16_scr):
    @pl.when(pl.program_id(0) == 0)
    def _():
        w16_scr[...] = w_ref[...].astype(BF16)

    o_ref[...] = res_ref[...] + jnp.dot(x_ref[...], w16_scr[...], preferred_element_type=F32)


def _proj_residual(x, w, layer, res, tm=512):
    m, k = x.shape
    n = w.shape[2]
    assert m % tm == 0
    return pl.pallas_call(
        _proj_res_kernel,
        grid=(m // tm,),
        in_specs=[
            pl.BlockSpec((tm, k), lambda i: (i, 0)),
            pl.BlockSpec((None, k, n), lambda i: (layer, 0, 0), pipeline_mode=pl.Buffered(1)),
            pl.BlockSpec((tm, n), lambda i: (i, 0)),
        ],
        out_specs=pl.BlockSpec((tm, n), lambda i: (i, 0)),
        out_shape=jax.ShapeDtypeStruct((m, n), F32),
        scratch_shapes=[pltpu.VMEM((k, n), BF16)],
        compiler_params=_cparams(("arbitrary",)),
        name="proj_residual",
    )(x, w, res)


def _qkv_kernel(x_ref, gq_ref, gkv_ref, wq_ref, wkv_ref, o_ref, xn_scr, *, n_q_tiles, q_scale):
    j = pl.program_id(1)

    @pl.when(j == 0)
    def _():
        x = x_ref[...]
        xn_scr[...] = (x * _rms_scale(x)).astype(BF16)

    @pl.when(j < n_q_tiles)
    def _():
        q = jnp.dot(xn_scr[...], (wq_ref[...] * gq_ref[...]).astype(BF16), preferred_element_type=F32)
        o_ref[...] = (q * q_scale).astype(o_ref.dtype)

    @pl.when(j >= n_q_tiles)
    def _():
        kv = jnp.dot(xn_scr[...], (wkv_ref[...] * gkv_ref[...]).astype(BF16), preferred_element_type=F32)
        o_ref[...] = kv.astype(o_ref.dtype)


def _qkv_proj(h2d, gq_col, gkv_col, w_q, layer, w_kv, q_scale, tm=1024, tn=512):
    m, k = h2d.shape
    assert m % tm == 0 and w_q.shape[2] % tn == 0 and w_kv.shape[1] % tn == 0
    nq = w_q.shape[2] // tn
    nkv = w_kv.shape[1] // tn
    return pl.pallas_call(
        functools.partial(_qkv_kernel, n_q_tiles=nq, q_scale=q_scale),
        grid=(m // tm, nq + nkv),
        in_specs=[
            pl.BlockSpec((tm, k), lambda i, j: (i, 0)),
            pl.BlockSpec((k, 1), lambda i, j: (0, 0)),
            pl.BlockSpec((k, 1), lambda i, j: (0, 0)),
            pl.BlockSpec((None, k, tn), lambda i, j: (layer, 0, jnp.minimum(j, nq - 1))),
            pl.BlockSpec((k, tn), lambda i, j: (0, jnp.maximum(j - nq, 0))),
        ],
        out_specs=pl.BlockSpec((tm, tn), lambda i, j: (i, j)),
        out_shape=jax.ShapeDtypeStruct((m, (nq + nkv) * tn), BF16),
        scratch_shapes=[pltpu.VMEM((tm, k), BF16)],
        compiler_params=_cparams(("parallel", "arbitrary")),
        name="qkv_proj",
    )(h2d, gq_col, gkv_col, w_q, w_kv)


def _ffn_kernel(h_ref, g_ref, wg_ref, wu_ref, wd_ref, gfin_ref, o_ref, u_scr, *, final_norm):
    f = pl.program_id(1)

    @pl.when(f == 0)
    def _():
        x = h_ref[...]
        u_scr[...] = (x * _rms_scale(x) * g_ref[...]).astype(BF16)
        o_ref[...] = x

    u = u_scr[...]
    gate = jnp.dot(u, wg_ref[...].astype(BF16), preferred_element_type=F32)
    up = jnp.dot(u, wu_ref[...].astype(BF16), preferred_element_type=F32)
    act = (jax.nn.silu(gate) * up).astype(BF16)
    o_ref[...] += jnp.dot(act, wd_ref[...].astype(BF16), preferred_element_type=F32)

    if final_norm:
        @pl.when(f == pl.num_programs(1) - 1)
        def _():
            y = o_ref[...]
            o_ref[...] = y * _rms_scale(y) * gfin_ref[...]


def _ffn_residual(h2d, g_row, w_gate, w_up, w_down, layer, gfin_row, final_norm, tm=1024, tf=256):
    m, d = h2d.shape
    ff = w_gate.shape[2]
    assert m % tm == 0 and ff % tf == 0
    return pl.pallas_call(
        functools.partial(_ffn_kernel, final_norm=final_norm),
        grid=(m // tm, ff // tf),
        in_specs=[
            pl.BlockSpec((tm, d), lambda i, f: (i, 0)),
            pl.BlockSpec((1, d), lambda i, f: (0, 0)),
            pl.BlockSpec((None, d, tf), lambda i, f: (layer, 0, f)),
            pl.BlockSpec((None, d, tf), lambda i, f: (layer, 0, f)),
            pl.BlockSpec((None, tf, d), lambda i, f: (layer, f, 0)),
            pl.BlockSpec((1, d), lambda i, f: (0, 0)),
        ],
        out_specs=pl.BlockSpec((tm, d), lambda i, f: (i, 0)),
        out_shape=jax.ShapeDtypeStruct((m, d), F32),
        scratch_shapes=[pltpu.VMEM((tm, d), BF16)],
        compiler_params=_cparams(("parallel", "arbitrary")),
        name="ffn_residual",
    )(h2d, g_row, w_gate, w_up, w_down, gfin_row)


def _moba_tables(n_heads, seq_len):
    nb = seq_len // KEY_BLOCK
    nt = MOBA_BIAS_TERMS
    assert nb + 2 * nt <= LANES
    slopes = np.exp2(-8.0 * np.arange(1, n_heads + 1, dtype=np.float64) / n_heads) * LOG2_E
    bias = slopes[:, None] * np.arange(seq_len, dtype=np.float64)[None, :]
    top = 2.0 ** math.ceil(math.log2(LOG2_E * seq_len))

    terms, rest, quantum = [], bias, top / 256.0
    for _ in range(nt - 1):
        term = np.floor(rest / quantum) * quantum
        terms.append(term)
        rest = rest - term
        quantum = quantum / 256.0
    terms.append(rest)

    qx = np.zeros((n_heads, seq_len, LANES), np.float64)
    kx = np.zeros((n_heads, seq_len, LANES), np.float64)
    kx[:, np.arange(seq_len), np.arange(seq_len) // KEY_BLOCK] = 1.0
    for c, term in enumerate(terms):
        qx[:, :, nb + c] = -term
        qx[:, :, nb + nt + c] = 1.0
        kx[:, :, nb + c] = 1.0
        kx[:, :, nb + nt + c] = term
    return jnp.asarray(qx.astype(jnp.bfloat16)), jnp.asarray(kx.astype(jnp.bfloat16))


def _moba_kernel(q_ref, k_ref, v_ref, qx_ref, kx_ref, o_ref, kmean_scr, causal_scr, qa_scr, ka_scr, va_scr,
                 s_scr, p_scr):
    kb = KEY_BLOCK
    hd = HEAD_DIM
    rc_rows = MOBA_ROW_CHUNK
    seq = k_ref.shape[1]
    nb = seq // kb
    nh = qx_ref.shape[0]
    neg_inf = -jnp.inf
    nt = (((1,), (1,)), ((), ()))

    row = lax.broadcasted_iota(jnp.int32, (kb, kb), 0)
    colk = lax.broadcasted_iota(jnp.int32, (kb, kb), 1)
    causal_scr[...] = jnp.where(row >= colk, 0.0, neg_inf)
    for hh in range(nh):
        _moba_prepare(hh, q_ref, k_ref, v_ref, qx_ref, kx_ref, kmean_scr, qa_scr, ka_scr, va_scr)

    for slot, t0 in enumerate(range(0, nb, MOBA_TILE_GROUP)):
        slot = slot % s_scr.shape[1]
        tiles = range(t0, min(t0 + MOBA_TILE_GROUP, nb))
        q0 = t0 * kb
        rows = len(tiles) * kb
        width = (tiles[-1] + 1) * kb
        for hh in range(nh):
            s_scr[hh, slot, 0:rows, 0:width] = lax.dot_general(
                qa_scr[hh, q0:q0 + rows, :], ka_scr[hh, 0:width, :], nt, preferred_element_type=F32)
        for hh in range(nh):
            for own in tiles:
                base = (own - t0) * kb
                for rc in range(kb // rc_rows):
                    r0, r1 = base + rc * rc_rows, base + (rc + 1) * rc_rows
                    parts = [s_scr[hh, slot, r0:r1, n * kb:(n + 1) * kb] for n in range(own)]
                    parts.append(s_scr[hh, slot, r0:r1, own * kb:(own + 1) * kb]
                                 + causal_scr[r0 - base:r1 - base, :])
                    m = jnp.max(functools.reduce(jnp.maximum, parts), axis=-1, keepdims=True)
                    for n in range(own + 1):
                        p_scr[hh, slot, r0:r1, n * kb:(n + 1) * kb] = jnp.exp2(parts[n] - m).astype(BF16)
                if (own + 1) * kb < width:
                    p_scr[hh, slot, base:base + kb, (own + 1) * kb:width] = jnp.zeros(
                        (kb, width - (own + 1) * kb), BF16)
        for hh in range(nh):
            for own in tiles:
                base = (own - t0) * kb
                res = jnp.dot(p_scr[hh, slot, base:base + kb, 0:width], va_scr[hh, 0:width, :],
                              preferred_element_type=F32)
                o_ref[0, own * kb:(own + 1) * kb, hh * hd:(hh + 1) * hd] = (
                    res[:, :hd] / res[:, hd:]).astype(o_ref.dtype)


def _moba_prepare(hh, q_ref, k_ref, v_ref, qx_ref, kx_ref, kmean_scr, qa_scr, ka_scr, va_scr):
    kb = KEY_BLOCK
    hd = HEAD_DIM
    seq = k_ref.shape[1]
    nb = seq // kb
    neg_inf = -jnp.inf
    nt = (((1,), (1,)), ((), ()))
    cols = slice(hh * hd, (hh + 1) * hd)

    qa_scr[hh, :, :hd] = q_ref[0, :, cols]
    ka_scr[hh, :, :hd] = k_ref[0, :, cols]
    ka_scr[hh, :, hd:] = kx_ref[hh]
    va_scr[hh, :, :hd] = v_ref[0, :, cols]
    va_scr[hh, :, hd:] = jnp.ones((seq, hd), BF16)
    kmean_scr[hh] = jnp.zeros(kmean_scr.shape[1:], F32)
    for n in range(nb):
        kmean_scr[hh, n:n + 1, :] = jnp.mean(k_ref[0, n * kb:(n + 1) * kb, cols].astype(F32), axis=0, keepdims=True)

    g0 = min((TOP_K_BLOCKS + 1) * kb, seq)
    qa_scr[hh, :g0, hd:] = qx_ref[hh, :g0, :]
    if seq > g0:
        ng = seq - g0
        gate = lax.dot_general(q_ref[0, g0:, cols].astype(F32), kmean_scr[hh], nt,
                               precision=lax.Precision.HIGHEST, preferred_element_type=F32)
        col = lax.broadcasted_iota(jnp.int32, (ng, LANES), 1)
        colf = col.astype(F32)
        own_v = lax.div(lax.broadcasted_iota(jnp.int32, (ng, LANES), 0), jnp.int32(kb)) + (TOP_K_BLOCKS + 1)
        past = col < own_v
        gate = jnp.where(past, gate, neg_inf)
        mask = jnp.where(past, MASK_BIG, 0.0)
        for _ in range(TOP_K_BLOCKS):
            mx = jnp.max(gate, axis=-1, keepdims=True)
            idx = jnp.min(jnp.where(gate == mx, colf, float(LANES)), axis=-1, keepdims=True)
            hit = colf == idx
            mask = jnp.where(hit, 0.0, mask)
            gate = jnp.where(hit, neg_inf, gate)
        qa_scr[hh, g0:, hd:] = (qx_ref[hh, g0:, :].astype(F32) + mask).astype(BF16)


def _moba_attention(qkv3d, d_model):
    b, l, _ = qkv3d.shape
    n_heads = d_model // HEAD_DIM
    kb = KEY_BLOCK
    nh = math.gcd(n_heads, MOBA_HEADS)
    wide = nh * HEAD_DIM
    qx, kx = _moba_tables(n_heads, l)
    return pl.pallas_call(
        _moba_kernel,
        grid=(b, n_heads // nh),
        in_specs=[
            pl.BlockSpec((1, l, wide), lambda i, h: (i, 0, h)),
            pl.BlockSpec((1, l, wide), lambda i, h: (i, 0, n_heads // nh + h)),
            pl.BlockSpec((1, l, wide), lambda i, h: (i, 0, 2 * (n_heads // nh) + h)),
            pl.BlockSpec((nh, l, LANES), lambda i, h: (h, 0, 0)),
            pl.BlockSpec((nh, l, LANES), lambda i, h: (h, 0, 0)),
        ],
        out_specs=pl.BlockSpec((1, l, wide), lambda i, h: (i, 0, h)),
        out_shape=jax.ShapeDtypeStruct((b, l, d_model), BF16),
        scratch_shapes=[
            pltpu.VMEM((nh, LANES, HEAD_DIM), F32),
            pltpu.VMEM((kb, kb), F32),
            pltpu.VMEM((nh, l, HEAD_DIM + LANES), BF16),
            pltpu.VMEM((nh, l, HEAD_DIM + LANES), BF16),
            pltpu.VMEM((nh, l, 2 * HEAD_DIM), BF16),
            pltpu.VMEM((nh, MOBA_SLOTS, MOBA_TILE_GROUP * kb, l), F32),
            pltpu.VMEM((nh, MOBA_SLOTS, MOBA_TILE_GROUP * kb, l), BF16),
        ],
        compiler_params=_cparams(("parallel", "parallel")),
        name="moba_attention",
    )(qkv3d, qkv3d, qkv3d, qx, kx)


def kernel(x, g_mix, lambda_re, lambda_im, log_dt, b_re, b_im, c_re, c_im, d_skip, w_glu, b_glu, g_kv, w_kv,
           w_q, w_o, g_ffn, w_gate, w_up, w_down, g_final):
    bsz, l, d = x.shape
    depth = g_mix.shape[0]
    n_a = lambda_re.shape[0]
    m = bsz * l
    assert l % KEY_BLOCK == 0 and d % HEAD_DIM == 0 and l // KEY_BLOCK <= LANES
    row = lambda v: v.reshape(1, -1).astype(F32)

    h = x.astype(F32).reshape(m, d)
    qkv = None
    for i in range(depth):
        if i < n_a:
            tables = _s5_tables(lambda_re[i], lambda_im[i], log_dt[i], b_re[i], b_im[i], c_re[i], c_im[i])
            u = _rmsnorm_interleaved(h, row(g_mix[i]), bsz)
            z = _s5_mixer(u, row(d_skip[i]), *tables)
            h = _glu_residual(z, w_glu, i, row(b_glu[i]), h)
        else:
            j = i - n_a
            col = lambda v: v.reshape(-1, 1).astype(F32)
            qkv_i = _qkv_proj(h, col(g_mix[i]), col(g_kv), w_q, j, w_kv, HEAD_DIM ** -0.5 * LOG2_E)
            if qkv is None:
                qkv = qkv_i
            else:
                qkv = jnp.concatenate([qkv_i[:, :d], qkv[:, d:]], axis=1)
            attn = _moba_attention(qkv.reshape(bsz, l, 3 * d), d)
            h = _proj_residual(attn.reshape(m, d), w_o, j, h)
        h = _ffn_residual(h, row(g_ffn[i]), w_gate, w_up, w_down, i, row(g_final), final_norm=(i == depth - 1))
    return h.reshape(bsz, l, d).astype(x.dtype)
```

```python
import functools
import math

import jax
import jax.numpy as jnp
import numpy as np
from jax import lax
from jax.experimental import pallas as pl
from jax.experimental.pallas import tpu as pltpu

F32 = jnp.float32
BF16 = jnp.bfloat16

EPS = 1e-6
LANES = 128
HEAD_DIM = 128
KEY_BLOCK = 256
TOP_K_BLOCKS = 3
MOBA_ROW_CHUNK = 16
MOBA_TILE_GROUP = 2
MOBA_HEADS = 4
MOBA_SLOTS = 1
MOBA_BIAS_TERMS = 4
LOG2_E = math.log2(math.e)
MASK_BIG = -2.0 ** 100
SSM_CHUNK = 8
S5_GROUP = 16
V7X_VMEM_BYTES = 64 * 1024 * 1024
VMEM_LIMIT = V7X_VMEM_BYTES // 8 * 7


def _cparams(sem):
    return pltpu.CompilerParams(dimension_semantics=sem, vmem_limit_bytes=VMEM_LIMIT)


def _rms_scale(x):
    return lax.rsqrt(jnp.mean(x * x, axis=-1, keepdims=True) + EPS)


def _rmsnorm_kernel(x_ref, g_ref, o_ref):
    x = x_ref[...]
    o_ref[...] = (x * _rms_scale(x) * g_ref[...]).reshape(o_ref.shape)


def _rmsnorm_interleaved(x2d, g_row, bsz, tm=1024):
    m, d = x2d.shape
    assert m % (bsz * tm) == 0 and tm % S5_GROUP == 0
    tiles = m // bsz // tm
    return pl.pallas_call(
        _rmsnorm_kernel,
        grid=(bsz, tiles),
        in_specs=[pl.BlockSpec((tm, d), lambda b, i: (b * tiles + i, 0)), pl.BlockSpec((1, d), lambda b, i: (0, 0))],
        out_specs=pl.BlockSpec((tm // S5_GROUP, None, S5_GROUP, d), lambda b, i: (i, b, 0, 0)),
        out_shape=jax.ShapeDtypeStruct((m // bsz // S5_GROUP, bsz, S5_GROUP, d), F32),
        compiler_params=_cparams(("parallel", "parallel")),
        name="rmsnorm",
    )(x2d, g_row)


def _s5_tables(lam_re, lam_im, log_dt, b_re, b_im, c_re, c_im):
    g, p, c = b_re.shape
    t = SSM_CHUNK
    gl = LANES // c
    nj = g // gl
    ns = gl * p
    lr, li = lam_re.astype(F32), lam_im.astype(F32)
    dt = jnp.exp(log_dt.astype(F32))[:, None]
    ar, ai = lr * dt, li * dt
    mag = jnp.exp(ar)
    nr, ni = mag * jnp.cos(ai) - 1.0, mag * jnp.sin(ai)
    den = lr * lr + li * li
    fr, fi = (nr * lr + ni * li) / den, (ni * lr - nr * li) / den
    br, bi = b_re.astype(F32), b_im.astype(F32)
    bbr = fr[..., None] * br - fi[..., None] * bi
    bbi = fr[..., None] * bi + fi[..., None] * br
    dd = jnp.arange(t + 1, dtype=F32)[:, None, None]
    pmag = jnp.exp(ar[None] * dd)
    pr, pi = pmag * jnp.cos(ai[None] * dd), pmag * jnp.sin(ai[None] * dd)

    def per_tile(x):
        return jnp.transpose(x.reshape(nj, gl, c, p), (0, 2, 1, 3)).reshape(nj, c, ns)

    bcmp = jnp.concatenate([per_tile(jnp.swapaxes(bbr, 1, 2)), per_tile(jnp.swapaxes(bbi, 1, 2))], axis=-1)
    ccmp = jnp.concatenate([per_tile(c_re.astype(F32)), per_tile(c_im.astype(F32))], axis=-1)
    pw = jnp.concatenate([pr.reshape(t + 1, nj, ns), pi.reshape(t + 1, nj, ns)], axis=-1)
    pw = jnp.pad(jnp.swapaxes(pw, 0, 1), ((0, 0), (0, 16 - (t + 1)), (0, 0)))
    return bcmp, ccmp, pw


def _s5_build_tables(bcmp_ref, ccmp_ref, pw_ref, w1_scr, w2_scr):
    t = SSM_CHUNK
    cg = bcmp_ref.shape[1]
    ns = bcmp_ref.shape[2] // 2
    sg = ns * cg // LANES
    yw = t * LANES
    same_group = (lax.broadcasted_iota(jnp.int32, (LANES, ns), 0) // cg
                  == lax.broadcasted_iota(jnp.int32, (LANES, ns), 1) // sg)

    def block_diag(x):
        return jnp.where(same_group, jnp.tile(x, (LANES // cg, 1)), 0.0)

    bre, bim = bcmp_ref[0, :, :ns], bcmp_ref[0, :, ns:]
    cre, cim = ccmp_ref[0, :, :ns], ccmp_ref[0, :, ns:]
    bcat16 = jnp.concatenate([block_diag(bre), block_diag(bim)], axis=1).astype(BF16)
    nt = (((1,), (1,)), ((), ()))
    for d in range(t + 1):
        pr, pi = pw_ref[0, d:d + 1, :ns], pw_ref[0, d:d + 1, ns:]
        mo = jnp.concatenate([block_diag(cre * pr - cim * pi), block_diag(-(cre * pi + cim * pr))],
                             axis=1).astype(BF16)
        if d >= 1:
            w2_scr[(d - 1) * LANES:d * LANES, :] = mo
        if d < t:
            kd = lax.dot_general(bcat16, mo, nt, preferred_element_type=F32).astype(BF16)
            for ip in range(t - d):
                w1_scr[ip * LANES:(ip + 1) * LANES, (ip + d) * LANES:(ip + d + 1) * LANES] = kd
            ip = t - 1 - d
            w1_scr[ip * LANES:(ip + 1) * LANES, yw:yw + ns] = block_diag(bre * pr - bim * pi).astype(BF16)
            w1_scr[ip * LANES:(ip + 1) * LANES, yw + ns:] = block_diag(bre * pi + bim * pr).astype(BF16)
    for ip in range(1, t):
        w1_scr[ip * LANES:(ip + 1) * LANES, 0:ip * LANES] = jnp.zeros((LANES, ip * LANES), BF16)


def _s5_kernel(u_ref, d_ref, bcmp_ref, ccmp_ref, pw_ref, z_ref, w1_scr, w2_scr, lhs_scr, s_scr, yc_scr, y_scr,
               *, n_seq):
    t = SSM_CHUNK
    cpg = S5_GROUP // t
    rows = lhs_scr.shape[0]
    ngroups = rows // (n_seq * cpg)
    nsl = s_scr.shape[0] // 2
    yw = t * LANES

    _s5_build_tables(bcmp_ref, ccmp_ref, pw_ref, w1_scr, w2_scr)

    for i in range(t):
        lhs_scr[:, i * LANES:(i + 1) * LANES] = u_ref[pl.ds(i, rows, stride=t), :].astype(BF16)
    strip = 2 * LANES
    for c0 in range(0, yw, strip):
        c1 = c0 + strip
        yc_scr[:, c0:c1] = jnp.dot(lhs_scr[:, :c1], w1_scr[:c1, c0:c1], preferred_element_type=F32)
    s_in_all = jnp.dot(lhs_scr[...], w1_scr[:, yw:], preferred_element_type=F32)
    for q in range(2 * nsl):
        s_scr[q] = s_in_all[:, q * LANES:(q + 1) * LANES]

    a_re = [pw_ref[0, t:t + 1, q * LANES:(q + 1) * LANES] for q in range(nsl)]
    a_im = [pw_ref[0, t:t + 1, (nsl + q) * LANES:(nsl + q + 1) * LANES] for q in range(nsl)]

    def scan_group(g, carry):
        e = list(carry)
        for c in range(cpg):
            sel = pl.ds(g * (n_seq * cpg) + c, n_seq, stride=cpg)
            for q in range(nsl):
                e_re, e_im = e[q], e[nsl + q]
                in_re, in_im = s_scr[q, sel, :], s_scr[nsl + q, sel, :]
                s_scr[q, sel, :] = e_re
                s_scr[nsl + q, sel, :] = e_im
                e[q] = a_re[q] * e_re - a_im[q] * e_im + in_re
                e[nsl + q] = a_re[q] * e_im + a_im[q] * e_re + in_im
        return tuple(e)

    lax.fori_loop(0, ngroups, scan_group, (jnp.zeros((n_seq, LANES), F32),) * (2 * nsl))

    s_prev = jnp.concatenate([s_scr[q].astype(BF16) for q in range(2 * nsl)], axis=1)
    y = yc_scr[...] + lax.dot_general(s_prev, w2_scr[...], (((1,), (1,)), ((), ())), preferred_element_type=F32)
    for i in range(t):
        y_scr[pl.ds(i, rows, stride=t), :] = y[:, i * LANES:(i + 1) * LANES]

    yy = y_scr[...] + d_ref[...] * u_ref[...]
    z_ref[...] = jax.nn.gelu(yy).astype(z_ref.dtype)


def _s5_mixer(u4d, d_row, bcmp, ccmp, pw):
    ngr, bsz, grp, d = u4d.shape
    m = ngr * bsz * grp
    assert grp == S5_GROUP and d % LANES == 0
    nj = d // LANES
    rows = m // SSM_CHUNK
    cg, ns2 = bcmp.shape[1:]
    yw = SSM_CHUNK * LANES
    z = pl.pallas_call(
        functools.partial(_s5_kernel, n_seq=bsz),
        grid=(nj,),
        in_specs=[
            pl.BlockSpec((m, LANES), lambda j: (0, j)),
            pl.BlockSpec((1, LANES), lambda j: (0, j)),
            pl.BlockSpec((1, cg, ns2), lambda j: (j, 0, 0)),
            pl.BlockSpec((1, cg, ns2), lambda j: (j, 0, 0)),
            pl.BlockSpec((1, 16, ns2), lambda j: (j, 0, 0)),
        ],
        out_specs=pl.BlockSpec((m, LANES), lambda j: (0, j)),
        out_shape=jax.ShapeDtypeStruct((m, d), BF16),
        scratch_shapes=[
            pltpu.VMEM((yw, yw + ns2), BF16),
            pltpu.VMEM((yw, ns2), BF16),
            pltpu.VMEM((rows, yw), BF16),
            pltpu.VMEM((ns2 // LANES, rows, LANES), F32),
            pltpu.VMEM((rows, yw), F32),
            pltpu.VMEM((m, LANES), F32),
        ],
        compiler_params=_cparams(("parallel",)),
        name="s5_mixer",
    )(u4d.reshape(m, d), d_row, bcmp, ccmp, pw)
    return z.reshape(ngr, bsz, grp, d)


def _glu_kernel(z_ref, w_ref, b_ref, res_ref, o_ref, w16_scr):
    tm, n = o_ref.shape

    @pl.when((pl.program_id(0) == 0) & (pl.program_id(1) == 0))
    def _():
        w16_scr[...] = w_ref[...].astype(BF16)

    z = z_ref[...].reshape(tm, n)
    a = jnp.dot(z, w16_scr[...], preferred_element_type=F32) + b_ref[...]
    o_ref[...] = res_ref[...] + z.astype(F32) * jax.nn.sigmoid(a)


def _glu_residual(z4d, w, layer, b_row, res, tm=512):
    ngr, bsz, grp, k = z4d.shape
    m = ngr * bsz * grp
    n = w.shape[2]
    assert k == n and (ngr * grp) % tm == 0 and tm % grp == 0
    tiles = ngr * grp // tm
    return pl.pallas_call(
        _glu_kernel,
        grid=(bsz, tiles),
        in_specs=[
            pl.BlockSpec((tm // grp, None, grp, k), lambda b, i: (i, b, 0, 0)),
            pl.BlockSpec((None, k, n), lambda b, i: (layer, 0, 0), pipeline_mode=pl.Buffered(1)),
            pl.BlockSpec((1, n), lambda b, i: (0, 0)),
            pl.BlockSpec((tm, n), lambda b, i: (b * tiles + i, 0)),
        ],
        out_specs=pl.BlockSpec((tm, n), lambda b, i: (b * tiles + i, 0)),
        out_shape=jax.ShapeDtypeStruct((m, n), F32),
        scratch_shapes=[pltpu.VMEM((k, n), BF16)],
        compiler_params=_cparams(("arbitrary", "arbitrary")),
        name="glu_residual",
    )(z4d, w, b_row, res)


def _proj_res_kernel(x_ref, w_ref, res_ref, o_ref, w16_scr):
    @pl.when(pl.program_id(0) == 0)
    def _():
        w16_scr[...] = w_ref[...].astype(BF16)

    o_ref[...] = res_ref[...] + jnp.dot(x_ref[...], w16_scr[...], preferred_element_type=F32)


def _proj_residual(x, w, layer, res, tm=512):
    m, k = x.shape
    n = w.shape[2]
    assert m % tm == 0
    return pl.pallas_call(
        _proj_res_kernel,
        grid=(m // tm,),
        in_specs=[
            pl.BlockSpec((tm, k), lambda i: (i, 0)),
            pl.BlockSpec((None, k, n), lambda i: (layer, 0, 0), pipeline_mode=pl.Buffered(1)),
            pl.BlockSpec((tm, n), lambda i: (i, 0)),
        ],
        out_specs=pl.BlockSpec((tm, n), lambda i: (i, 0)),
        out_shape=jax.ShapeDtypeStruct((m, n), F32),
        scratch_shapes=[pltpu.VMEM((k, n), BF16)],
        compiler_params=_cparams(("arbitrary",)),
        name="proj_residual",
    )(x, w, res)


def _qkv_kernel(x_ref, gq_ref, gkv_ref, wq_ref, wkv_ref, o_ref, xn_scr, *, n_q_tiles, q_scale):
    j = pl.program_id(1)

    @pl.when(j == 0)
    def _():
        x = x_ref[...]
        xn_scr[...] = (x * _rms_scale(x)).astype(BF16)

    @pl.when(j < n_q_tiles)
    def _():
        q = jnp.dot(xn_scr[...], (wq_ref[...] * gq_ref[...]).astype(BF16), preferred_element_type=F32)
        o_ref[...] = (q * q_scale).astype(o_ref.dtype)

    @pl.when(j >= n_q_tiles)
    def _():
        kv = jnp.dot(xn_scr[...], (wkv_ref[...] * gkv_ref[...]).astype(BF16), preferred_element_type=F32)
        o_ref[...] = kv.astype(o_ref.dtype)


def _qkv_proj(h2d, gq_col, gkv_col, w_q, layer, w_kv, q_scale, tm=1024, tn=512):
    m, k = h2d.shape
    assert m % tm == 0 and w_q.shape[2] % tn == 0 and w_kv.shape[1] % tn == 0
    nq = w_q.shape[2] // tn
    nkv = w_kv.shape[1] // tn
    return pl.pallas_call(
        functools.partial(_qkv_kernel, n_q_tiles=nq, q_scale=q_scale),
        grid=(m // tm, nq + nkv),
        in_specs=[
            pl.BlockSpec((tm, k), lambda i, j: (i, 0)),
            pl.BlockSpec((k, 1), lambda i, j: (0, 0)),
            pl.BlockSpec((k, 1), lambda i, j: (0, 0)),
            pl.BlockSpec((None, k, tn), lambda i, j: (layer, 0, jnp.minimum(j, nq - 1))),
            pl.BlockSpec((k, tn), lambda i, j: (0, jnp.maximum(j - nq, 0))),
        ],
        out_specs=pl.BlockSpec((tm, tn), lambda i, j: (i, j)),
        out_shape=jax.ShapeDtypeStruct((m, (nq + nkv) * tn), BF16),
        scratch_shapes=[pltpu.VMEM((tm, k), BF16)],
        compiler_params=_cparams(("parallel", "arbitrary")),
        name="qkv_proj",
    )(h2d, gq_col, gkv_col, w_q, w_kv)


def _ffn_kernel(h_ref, g_ref, wg_ref, wu_ref, wd_ref, gfin_ref, o_ref, u_scr, *, final_norm):
    f = pl.program_id(1)

    @pl.when(f == 0)
    def _():
        x = h_ref[...]
        u_scr[...] = (x * _rms_scale(x) * g_ref[...]).astype(BF16)
        o_ref[...] = x

    u = u_scr[...]
    gate = jnp.dot(u, wg_ref[...].astype(BF16), preferred_element_type=F32)
    up = jnp.dot(u, wu_ref[...].astype(BF16), preferred_element_type=F32)
    act = (jax.nn.silu(gate) * up).astype(BF16)
    o_ref[...] += jnp.dot(act, wd_ref[...].astype(BF16), preferred_element_type=F32)

    if final_norm:
        @pl.when(f == pl.num_programs(1) - 1)
        def _():
            y = o_ref[...]
            o_ref[...] = y * _rms_scale(y) * gfin_ref[...]


def _ffn_residual(h2d, g_row, w_gate, w_up, w_down, layer, gfin_row, final_norm, tm=1024, tf=256):
    m, d = h2d.shape
    ff = w_gate.shape[2]
    assert m % tm == 0 and ff % tf == 0
    return pl.pallas_call(
        functools.partial(_ffn_kernel, final_norm=final_norm),
        grid=(m // tm, ff // tf),
        in_specs=[
            pl.BlockSpec((tm, d), lambda i, f: (i, 0)),
            pl.BlockSpec((1, d), lambda i, f: (0, 0)),
            pl.BlockSpec((None, d, tf), lambda i, f: (layer, 0, f)),
            pl.BlockSpec((None, d, tf), lambda i, f: (layer, 0, f)),
            pl.BlockSpec((None, tf, d), lambda i, f: (layer, f, 0)),
            pl.BlockSpec((1, d), lambda i, f: (0, 0)),
        ],
        out_specs=pl.BlockSpec((tm, d), lambda i, f: (i, 0)),
        out_shape=jax.ShapeDtypeStruct((m, d), F32),
        scratch_shapes=[pltpu.VMEM((tm, d), BF16)],
        compiler_params=_cparams(("parallel", "arbitrary")),
        name="ffn_residual",
    )(h2d, g_row, w_gate, w_up, w_down, gfin_row)


def _moba_tables(n_heads, seq_len):
    nb = seq_len // KEY_BLOCK
    nt = MOBA_BIAS_TERMS
    assert nb + 2 * nt <= LANES
    slopes = np.exp2(-8.0 * np.arange(1, n_heads + 1, dtype=np.float64) / n_heads) * LOG2_E
    bias = slopes[:, None] * np.arange(seq_len, dtype=np.float64)[None, :]
    top = 2.0 ** math.ceil(math.log2(LOG2_E * seq_len))

    terms, rest, quantum = [], bias, top / 256.0
    for _ in range(nt - 1):
        term = np.floor(rest / quantum) * quantum
        terms.append(term)
        rest = rest - term
        quantum = quantum / 256.0
    terms.append(rest)

    qx = np.zeros((n_heads, seq_len, LANES), np.float64)
    kx = np.zeros((n_heads, seq_len, LANES), np.float64)
    kx[:, np.arange(seq_len), np.arange(seq_len) // KEY_BLOCK] = 1.0
    for c, term in enumerate(terms):
        qx[:, :, nb + c] = -term
        qx[:, :, nb + nt + c] = 1.0
        kx[:, :, nb + c] = 1.0
        kx[:, :, nb + nt + c] = term
    return jnp.asarray(qx.astype(jnp.bfloat16)), jnp.asarray(kx.astype(jnp.bfloat16))


def _moba_kernel(q_ref, k_ref, v_ref, qx_ref, kx_ref, o_ref, kmean_scr, causal_scr, qa_scr, ka_scr, va_scr,
                 s_scr, p_scr):
    kb = KEY_BLOCK
    hd = HEAD_DIM
    rc_rows = MOBA_ROW_CHUNK
    seq = k_ref.shape[1]
    nb = seq // kb
    nh = qx_ref.shape[0]
    neg_inf = -jnp.inf
    nt = (((1,), (1,)), ((), ()))

    row = lax.broadcasted_iota(jnp.int32, (kb, kb), 0)
    colk = lax.broadcasted_iota(jnp.int32, (kb, kb), 1)
    causal_scr[...] = jnp.where(row >= colk, 0.0, neg_inf)
    for hh in range(nh):
        _moba_prepare(hh, q_ref, k_ref, v_ref, qx_ref, kx_ref, kmean_scr, qa_scr, ka_scr, va_scr)

    for slot, t0 in enumerate(range(0, nb, MOBA_TILE_GROUP)):
        slot = slot % s_scr.shape[1]
        tiles = range(t0, min(t0 + MOBA_TILE_GROUP, nb))
        q0 = t0 * kb
        rows = len(tiles) * kb
        width = (tiles[-1] + 1) * kb
        for hh in range(nh):
            s_scr[hh, slot, 0:rows, 0:width] = lax.dot_general(
                qa_scr[hh, q0:q0 + rows, :], ka_scr[hh, 0:width, :], nt, preferred_element_type=F32)
        for hh in range(nh):
            for own in tiles:
                base = (own - t0) * kb
                for rc in range(kb // rc_rows):
                    r0, r1 = base + rc * rc_rows, base + (rc + 1) * rc_rows
                    parts = [s_scr[hh, slot, r0:r1, n * kb:(n + 1) * kb] for n in range(own)]
                    parts.append(s_scr[hh, slot, r0:r1, own * kb:(own + 1) * kb]
                                 + causal_scr[r0 - base:r1 - base, :])
                    m = jnp.max(functools.reduce(jnp.maximum, parts), axis=-1, keepdims=True)
                    for n in range(own + 1):
                        p_scr[hh, slot, r0:r1, n * kb:(n + 1) * kb] = jnp.exp2(parts[n] - m).astype(BF16)
                if (own + 1) * kb < width:
                    p_scr[hh, slot, base:base + kb, (own + 1) * kb:width] = jnp.zeros(
                        (kb, width - (own + 1) * kb), BF16)
        for hh in range(nh):
            for own in tiles:
                base = (own - t0) * kb
                res = jnp.dot(p_scr[hh, slot, base:base + kb, 0:width], va_scr[hh, 0:width, :],
                              preferred_element_type=F32)
                o_ref[0, own * kb:(own + 1) * kb, hh * hd:(hh + 1) * hd] = (
                    res[:, :hd] / res[:, hd:]).astype(o_ref.dtype)


def _moba_prepare(hh, q_ref, k_ref, v_ref, qx_ref, kx_ref, kmean_scr, qa_scr, ka_scr, va_scr):
    kb = KEY_BLOCK
    hd = HEAD_DIM
    seq = k_ref.shape[1]
    nb = seq // kb
    neg_inf = -jnp.inf
    nt = (((1,), (1,)), ((), ()))
    cols = slice(hh * hd, (hh + 1) * hd)

    qa_scr[hh, :, :hd] = q_ref[0, :, cols]
    ka_scr[hh, :, :hd] = k_ref[0, :, cols]
    ka_scr[hh, :, hd:] = kx_ref[hh]
    va_scr[hh, :, :hd] = v_ref[0, :, cols]
    va_scr[hh, :, hd:] = jnp.ones((seq, hd), BF16)
    kmean_scr[hh] = jnp.zeros(kmean_scr.shape[1:], F32)
    for n in range(nb):
        kmean_scr[hh, n:n + 1, :] = jnp.mean(k_ref[0, n * kb:(n + 1) * kb, cols].astype(F32), axis=0, keepdims=True)

    g0 = min((TOP_K_BLOCKS + 1) * kb, seq)
    qa_scr[hh, :g0, hd:] = qx_ref[hh, :g0, :]
    if seq > g0:
        ng = seq - g0
        q_gated = q_ref[0, g0:, cols]
        rest = kmean_scr[hh]
        gate = jnp.zeros((ng, LANES), F32)
        for _ in range(3):
            term = rest.astype(BF16)
            gate = gate + lax.dot_general(q_gated, term, nt, preferred_element_type=F32)
            rest = rest - term.astype(F32)
        col = lax.broadcasted_iota(jnp.int32, (ng, LANES), 1)
        colf = col.astype(F32)
        own_v = lax.div(lax.broadcasted_iota(jnp.int32, (ng, LANES), 0), jnp.int32(kb)) + (TOP_K_BLOCKS + 1)
        past = col < own_v
        gate = jnp.where(past, gate, neg_inf)
        mask = jnp.where(past, MASK_BIG, 0.0)
        for _ in range(TOP_K_BLOCKS):
            mx = jnp.max(gate, axis=-1, keepdims=True)
            idx = jnp.min(jnp.where(gate == mx, colf, float(LANES)), axis=-1, keepdims=True)
            hit = colf == idx
            mask = jnp.where(hit, 0.0, mask)
            gate = jnp.where(hit, neg_inf, gate)
        qa_scr[hh, g0:, hd:] = (qx_ref[hh, g0:, :].astype(F32) + mask).astype(BF16)


def _moba_attention(qkv3d, d_model):
    b, l, _ = qkv3d.shape
    n_heads = d_model // HEAD_DIM
    kb = KEY_BLOCK
    nh = math.gcd(n_heads, MOBA_HEADS)
    wide = nh * HEAD_DIM
    qx, kx = _moba_tables(n_heads, l)
    return pl.pallas_call(
        _moba_kernel,
        grid=(b, n_heads // nh),
        in_specs=[
            pl.BlockSpec((1, l, wide), lambda i, h: (i, 0, h)),
            pl.BlockSpec((1, l, wide), lambda i, h: (i, 0, n_heads // nh + h)),
            pl.BlockSpec((1, l, wide), lambda i, h: (i, 0, 2 * (n_heads // nh) + h)),
            pl.BlockSpec((nh, l, LANES), lambda i, h: (h, 0, 0)),
            pl.BlockSpec((nh, l, LANES), lambda i, h: (h, 0, 0)),
        ],
        out_specs=pl.BlockSpec((1, l, wide), lambda i, h: (i, 0, h)),
        out_shape=jax.ShapeDtypeStruct((b, l, d_model), BF16),
        scratch_shapes=[
            pltpu.VMEM((nh, LANES, HEAD_DIM), F32),
            pltpu.VMEM((kb, kb), F32),
            pltpu.VMEM((nh, l, HEAD_DIM + LANES), BF16),
            pltpu.VMEM((nh, l, HEAD_DIM + LANES), BF16),
            pltpu.VMEM((nh, l, 2 * HEAD_DIM), BF16),
            pltpu.VMEM((nh, MOBA_SLOTS, MOBA_TILE_GROUP * kb, l), F32),
            pltpu.VMEM((nh, MOBA_SLOTS, MOBA_TILE_GROUP * kb, l), BF16),
        ],
        compiler_params=_cparams(("parallel", "parallel")),
        name="moba_attention",
    )(qkv3d, qkv3d, qkv3d, qx, kx)


def kernel(x, g_mix, lambda_re, lambda_im, log_dt, b_re, b_im, c_re, c_im, d_skip, w_glu, b_glu, g_kv, w_kv,
           w_q, w_o, g_ffn, w_gate, w_up, w_down, g_final):
    bsz, l, d = x.shape
    depth = g_mix.shape[0]
    n_a = lambda_re.shape[0]
    m = bsz * l
    assert l % KEY_BLOCK == 0 and d % HEAD_DIM == 0 and l // KEY_BLOCK <= LANES
    row = lambda v: v.reshape(1, -1).astype(F32)

    h = x.astype(F32).reshape(m, d)
    qkv = None
    for i in range(depth):
        if i < n_a:
            tables = _s5_tables(lambda_re[i], lambda_im[i], log_dt[i], b_re[i], b_im[i], c_re[i], c_im[i])
            u = _rmsnorm_interleaved(h, row(g_mix[i]), bsz)
            z = _s5_mixer(u, row(d_skip[i]), *tables)
            h = _glu_residual(z, w_glu, i, row(b_glu[i]), h)
        else:
            j = i - n_a
            col = lambda v: v.reshape(-1, 1).astype(F32)
            qkv_i = _qkv_proj(h, col(g_mix[i]), col(g_kv), w_q, j, w_kv, HEAD_DIM ** -0.5 * LOG2_E)
            if qkv is None:
                qkv = qkv_i
            else:
                qkv = jnp.concatenate([qkv_i[:, :d], qkv[:, d:]], axis=1)
            attn = _moba_attention(qkv.reshape(bsz, l, 3 * d), d)
            h = _proj_residual(attn.reshape(m, d), w_o, j, h)
        h = _ffn_residual(h, row(g_ffn[i]), w_gate, w_up, w_down, i, row(g_final), final_norm=(i == depth - 1))
    return h.reshape(bsz, l, d).astype(x.dtype)
```

```python
import functools
import math

import jax
import jax.numpy as jnp
import numpy as np
from jax import lax
from jax.experimental import pallas as pl
from jax.experimental.pallas import tpu as pltpu

F32 = jnp.float32
BF16 = jnp.bfloat16

EPS = 1e-6
LANES = 128
HEAD_DIM = 128
KEY_BLOCK = 256
TOP_K_BLOCKS = 3
MOBA_ROW_CHUNK = 16
MOBA_TILE_GROUP = 2
MOBA_HEADS = 4
MOBA_SLOTS = 1
MOBA_BIAS_TERMS = 4
LOG2_E = math.log2(math.e)
MASK_BIG = -2.0 ** 100
SSM_CHUNK = 8
S5_GROUP = 16
V7X_VMEM_BYTES = 64 * 1024 * 1024
VMEM_LIMIT = V7X_VMEM_BYTES // 8 * 7


def _cparams(sem):
    return pltpu.CompilerParams(dimension_semantics=sem, vmem_limit_bytes=VMEM_LIMIT)


def _rms_scale(x):
    return lax.rsqrt(jnp.mean(x * x, axis=-1, keepdims=True) + EPS)


def _rmsnorm_kernel(x_ref, g_ref, o_ref):
    x = x_ref[...]
    o_ref[...] = (x * _rms_scale(x) * g_ref[...]).reshape(o_ref.shape)


def _rmsnorm_interleaved(x2d, g_row, bsz, tm=1024):
    m, d = x2d.shape
    assert m % (bsz * tm) == 0 and tm % S5_GROUP == 0
    tiles = m // bsz // tm
    return pl.pallas_call(
        _rmsnorm_kernel,
        grid=(bsz, tiles),
        in_specs=[pl.BlockSpec((tm, d), lambda b, i: (b * tiles + i, 0)), pl.BlockSpec((1, d), lambda b, i: (0, 0))],
        out_specs=pl.BlockSpec((tm // S5_GROUP, None, S5_GROUP, d), lambda b, i: (i, b, 0, 0)),
        out_shape=jax.ShapeDtypeStruct((m // bsz // S5_GROUP, bsz, S5_GROUP, d), F32),
        compiler_params=_cparams(("parallel", "parallel")),
        name="rmsnorm",
    )(x2d, g_row)


def _s5_tables(lam_re, lam_im, log_dt, b_re, b_im, c_re, c_im):
    g, p, c = b_re.shape
    t = SSM_CHUNK
    gl = LANES // c
    nj = g // gl
    ns = gl * p
    lr, li = lam_re.astype(F32), lam_im.astype(F32)
    dt = jnp.exp(log_dt.astype(F32))[:, None]
    ar, ai = lr * dt, li * dt
    mag = jnp.exp(ar)
    nr, ni = mag * jnp.cos(ai) - 1.0, mag * jnp.sin(ai)
    den = lr * lr + li * li
    fr, fi = (nr * lr + ni * li) / den, (ni * lr - nr * li) / den
    br, bi = b_re.astype(F32), b_im.astype(F32)
    bbr = fr[..., None] * br - fi[..., None] * bi
    bbi = fr[..., None] * bi + fi[..., None] * br
    dd = jnp.arange(t + 1, dtype=F32)[:, None, None]
    pmag = jnp.exp(ar[None] * dd)
    pr, pi = pmag * jnp.cos(ai[None] * dd), pmag * jnp.sin(ai[None] * dd)

    def per_tile(x):
        return jnp.transpose(x.reshape(nj, gl, c, p), (0, 2, 1, 3)).reshape(nj, c, ns)

    bcmp = jnp.concatenate([per_tile(jnp.swapaxes(bbr, 1, 2)), per_tile(jnp.swapaxes(bbi, 1, 2))], axis=-1)
    ccmp = jnp.concatenate([per_tile(c_re.astype(F32)), per_tile(c_im.astype(F32))], axis=-1)
    pw = jnp.concatenate([pr.reshape(t + 1, nj, ns), pi.reshape(t + 1, nj, ns)], axis=-1)
    pw = jnp.pad(jnp.swapaxes(pw, 0, 1), ((0, 0), (0, 16 - (t + 1)), (0, 0)))
    return bcmp, ccmp, pw


def _s5_build_tables(bcmp_ref, ccmp_ref, pw_ref, w1_scr, w2_scr):
    t = SSM_CHUNK
    cg = bcmp_ref.shape[1]
    ns = bcmp_ref.shape[2] // 2
    sg = ns * cg // LANES
    yw = t * LANES
    same_group = (lax.broadcasted_iota(jnp.int32, (LANES, ns), 0) // cg
                  == lax.broadcasted_iota(jnp.int32, (LANES, ns), 1) // sg)

    def block_diag(x):
        return jnp.where(same_group, jnp.tile(x, (LANES // cg, 1)), 0.0)

    bre, bim = bcmp_ref[0, :, :ns], bcmp_ref[0, :, ns:]
    cre, cim = ccmp_ref[0, :, :ns], ccmp_ref[0, :, ns:]
    bcat16 = jnp.concatenate([block_diag(bre), block_diag(bim)], axis=1).astype(BF16)
    nt = (((1,), (1,)), ((), ()))
    for d in range(t + 1):
        pr, pi = pw_ref[0, d:d + 1, :ns], pw_ref[0, d:d + 1, ns:]
        mo = jnp.concatenate([block_diag(cre * pr - cim * pi), block_diag(-(cre * pi + cim * pr))],
                             axis=1).astype(BF16)
        if d >= 1:
            w2_scr[(d - 1) * LANES:d * LANES, :] = mo
        if d < t:
            kd = lax.dot_general(bcat16, mo, nt, preferred_element_type=F32).astype(BF16)
            for ip in range(t - d):
                w1_scr[ip * LANES:(ip + 1) * LANES, (ip + d) * LANES:(ip + d + 1) * LANES] = kd
            ip = t - 1 - d
            w1_scr[ip * LANES:(ip + 1) * LANES, yw:yw + ns] = block_diag(bre * pr - bim * pi).astype(BF16)
            w1_scr[ip * LANES:(ip + 1) * LANES, yw + ns:] = block_diag(bre * pi + bim * pr).astype(BF16)
    for ip in range(1, t):
        w1_scr[ip * LANES:(ip + 1) * LANES, 0:ip * LANES] = jnp.zeros((LANES, ip * LANES), BF16)


def _s5_kernel(u_ref, d_ref, bcmp_ref, ccmp_ref, pw_ref, z_ref, w1_scr, w2_scr, lhs_scr, s_scr, yc_scr, y_scr,
               *, n_seq):
    t = SSM_CHUNK
    cpg = S5_GROUP // t
    rows = lhs_scr.shape[0]
    ngroups = rows // (n_seq * cpg)
    nsl = s_scr.shape[0] // 2
    yw = t * LANES

    _s5_build_tables(bcmp_ref, ccmp_ref, pw_ref, w1_scr, w2_scr)

    for i in range(t):
        lhs_scr[:, i * LANES:(i + 1) * LANES] = u_ref[pl.ds(i, rows, stride=t), :].astype(BF16)
    strip = 2 * LANES
    for c0 in range(0, yw, strip):
        c1 = c0 + strip
        yc_scr[:, c0:c1] = jnp.dot(lhs_scr[:, :c1], w1_scr[:c1, c0:c1], preferred_element_type=F32)
    s_in_all = jnp.dot(lhs_scr[...], w1_scr[:, yw:], preferred_element_type=F32)
    for q in range(2 * nsl):
        s_scr[q] = s_in_all[:, q * LANES:(q + 1) * LANES]

    a_re = [pw_ref[0, t:t + 1, q * LANES:(q + 1) * LANES] for q in range(nsl)]
    a_im = [pw_ref[0, t:t + 1, (nsl + q) * LANES:(nsl + q + 1) * LANES] for q in range(nsl)]

    def scan_group(g, carry):
        e = list(carry)
        for c in range(cpg):
            sel = pl.ds(g * (n_seq * cpg) + c, n_seq, stride=cpg)
            for q in range(nsl):
                e_re, e_im = e[q], e[nsl + q]
                in_re, in_im = s_scr[q, sel, :], s_scr[nsl + q, sel, :]
                s_scr[q, sel, :] = e_re
                s_scr[nsl + q, sel, :] = e_im
                e[q] = a_re[q] * e_re - a_im[q] * e_im + in_re
                e[nsl + q] = a_re[q] * e_im + a_im[q] * e_re + in_im
        return tuple(e)

    lax.fori_loop(0, ngroups, scan_group, (jnp.zeros((n_seq, LANES), F32),) * (2 * nsl))

    s_prev = jnp.concatenate([s_scr[q].astype(BF16) for q in range(2 * nsl)], axis=1)
    y = yc_scr[...] + lax.dot_general(s_prev, w2_scr[...], (((1,), (1,)), ((), ())), preferred_element_type=F32)
    for i in range(t):
        y_scr[pl.ds(i, rows, stride=t), :] = y[:, i * LANES:(i + 1) * LANES]

    yy = y_scr[...] + d_ref[...] * u_ref[...]
    z_ref[...] = jax.nn.gelu(yy).astype(z_ref.dtype)


def _s5_mixer(u4d, d_row, bcmp, ccmp, pw):
    ngr, bsz, grp, d = u4d.shape
    m = ngr * bsz * grp
    assert grp == S5_GROUP and d % LANES == 0
    nj = d // LANES
    rows = m // SSM_CHUNK
    cg, ns2 = bcmp.shape[1:]
    yw = SSM_CHUNK * LANES
    z = pl.pallas_call(
        functools.partial(_s5_kernel, n_seq=bsz),
        grid=(nj,),
        in_specs=[
            pl.BlockSpec((m, LANES), lambda j: (0, j)),
            pl.BlockSpec((1, LANES), lambda j: (0, j)),
            pl.BlockSpec((1, cg, ns2), lambda j: (j, 0, 0)),
            pl.BlockSpec((1, cg, ns2), lambda j: (j, 0, 0)),
            pl.BlockSpec((1, 16, ns2), lambda j: (j, 0, 0)),
        ],
        out_specs=pl.BlockSpec((m, LANES), lambda j: (0, j)),
        out_shape=jax.ShapeDtypeStruct((m, d), BF16),
        scratch_shapes=[
            pltpu.VMEM((yw, yw + ns2), BF16),
            pltpu.VMEM((yw, ns2), BF16),
            pltpu.VMEM((rows, yw), BF16),
            pltpu.VMEM((ns2 // LANES, rows, LANES), F32),
            pltpu.VMEM((rows, yw), F32),
            pltpu.VMEM((m, LANES), F32),
        ],
        compiler_params=_cparams(("parallel",)),
        name="s5_mixer",
    )(u4d.reshape(m, d), d_row, bcmp, ccmp, pw)
    return z.reshape(ngr, bsz, grp, d)


def _glu_kernel(z_ref, w_ref, b_ref, res_ref, o_ref, w16_scr):
    tm, n = o_ref.shape

    @pl.when((pl.program_id(0) == 0) & (pl.program_id(1) == 0))
    def _():
        w16_scr[...] = w_ref[...].astype(BF16)

    z = z_ref[...].reshape(tm, n)
    a = jnp.dot(z, w16_scr[...], preferred_element_type=F32) + b_ref[...]
    o_ref[...] = res_ref[...] + z.astype(F32) * jax.nn.sigmoid(a)


def _glu_residual(z4d, w, layer, b_row, res, tm=512):
    ngr, bsz, grp, k = z4d.shape
    m = ngr * bsz * grp
    n = w.shape[2]
    assert k == n and (ngr * grp) % tm == 0 and tm % grp == 0
    tiles = ngr * grp // tm
    return pl.pallas_call(
        _glu_kernel,
        grid=(bsz, tiles),
        in_specs=[
            pl.BlockSpec((tm // grp, None, grp, k), lambda b, i: (i, b, 0, 0)),
            pl.BlockSpec((None, k, n), lambda b, i: (layer, 0, 0), pipeline_mode=pl.Buffered(1)),
            pl.BlockSpec((1, n), lambda b, i: (0, 0)),
            pl.BlockSpec((tm, n), lambda b, i: (b * tiles + i, 0)),
        ],
        out_specs=pl.BlockSpec((tm, n), lambda b, i: (b * tiles + i, 0)),
        out_shape=jax.ShapeDtypeStruct((m, n), F32),
        scratch_shapes=[pltpu.VMEM((k, n), BF16)],
        compiler_params=_cparams(("arbitrary", "arbitrary")),
        name="glu_residual",
    )(z4d, w, b_row, res)


def _proj_res_kernel(x_ref, w_ref, res_ref, o_ref, w16_scr):
    @pl.when(pl.program_id(0) == 0)
    def _():
        w16_scr[...] = w_ref[...].astype(BF16)

    o_ref[...] = res_ref[...] + jnp.dot(x_ref[...], w16_scr[...], preferred_element_type=F32)


def _proj_residual(x, w, layer, res, tm=512):
    m, k = x.shape
    n = w.shape[2]
    assert m % tm == 0
    return pl.pallas_call(
        _proj_res_kernel,
        grid=(m // tm,),
        in_specs=[
            pl.BlockSpec((tm, k), lambda i: (i, 0)),
            pl.BlockSpec((None, k, n), lambda i: (layer, 0, 0), pipeline_mode=pl.Buffered(1)),
            pl.BlockSpec((tm, n), lambda i: (i, 0)),
        ],
        out_specs=pl.BlockSpec((tm, n), lambda i: (i, 0)),
        out_shape=jax.ShapeDtypeStruct((m, n), F32),
        scratch_shapes=[pltpu.VMEM((k, n), BF16)],
        compiler_params=_cparams(("arbitrary",)),
        name="proj_residual",
    )(x, w, res)


def _qkv_kernel(x_ref, gq_ref, gkv_ref, wq_ref, wkv_ref, o_ref, xn_scr, *, n_q_tiles, q_scale):
    j = pl.program_id(1)

    @pl.when(j == 0)
    def _():
        x = x_ref[...]
        xn_scr[...] = (x * _rms_scale(x)).astype(BF16)

    @pl.when(j < n_q_tiles)
    def _():
        q = jnp.dot(xn_scr[...], (wq_ref[...] * gq_ref[...]).astype(BF16), preferred_element_type=F32)
        o_ref[...] = (q * q_scale).astype(o_ref.dtype)

    @pl.when(j >= n_q_tiles)
    def _():
        kv = jnp.dot(xn_scr[...], (wkv_ref[...] * gkv_ref[...]).astype(BF16), preferred_element_type=F32)
        o_ref[...] = kv.astype(o_ref.dtype)


def _qkv_proj(h2d, gq_col, gkv_col, w_q, layer, w_kv, q_scale, tm=1024, tn=512):
    m, k = h2d.shape
    assert m % tm == 0 and w_q.shape[2] % tn == 0 and w_kv.shape[1] % tn == 0
    nq = w_q.shape[2] // tn
    nkv = w_kv.shape[1] // tn
    return pl.pallas_call(
        functools.partial(_qkv_kernel, n_q_tiles=nq, q_scale=q_scale),
        grid=(m // tm, nq + nkv),
        in_specs=[
            pl.BlockSpec((tm, k), lambda i, j: (i, 0)),
            pl.BlockSpec((k, 1), lambda i, j: (0, 0)),
            pl.BlockSpec((k, 1), lambda i, j: (0, 0)),
            pl.BlockSpec((None, k, tn), lambda i, j: (layer, 0, jnp.minimum(j, nq - 1))),
            pl.BlockSpec((k, tn), lambda i, j: (0, jnp.maximum(j - nq, 0))),
        ],
        out_specs=pl.BlockSpec((tm, tn), lambda i, j: (i, j)),
        out_shape=jax.ShapeDtypeStruct((m, (nq + nkv) * tn), BF16),
        scratch_shapes=[pltpu.VMEM((tm, k), BF16)],
        compiler_params=_cparams(("parallel", "arbitrary")),
        name="qkv_proj",
    )(h2d, gq_col, gkv_col, w_q, w_kv)


def _ffn_kernel(h_ref, g_ref, wg_ref, wu_ref, wd_ref, gfin_ref, o_ref, u_scr, *, final_norm):
    f = pl.program_id(1)

    @pl.when(f == 0)
    def _():
        x = h_ref[...]
        u_scr[...] = (x * _rms_scale(x) * g_ref[...]).astype(BF16)
        o_ref[...] = x

    u = u_scr[...]
    gate = jnp.dot(u, wg_ref[...].astype(BF16), preferred_element_type=F32)
    up = jnp.dot(u, wu_ref[...].astype(BF16), preferred_element_type=F32)
    act = (jax.nn.silu(gate) * up).astype(BF16)
    o_ref[...] += jnp.dot(act, wd_ref[...].astype(BF16), preferred_element_type=F32)

    if final_norm:
        @pl.when(f == pl.num_programs(1) - 1)
        def _():
            y = o_ref[...]
            o_ref[...] = y * _rms_scale(y) * gfin_ref[...]


def _ffn_residual(h2d, g_row, w_gate, w_up, w_down, layer, gfin_row, final_norm, tm=1024, tf=256):
    m, d = h2d.shape
    ff = w_gate.shape[2]
    assert m % tm == 0 and ff % tf == 0
    return pl.pallas_call(
        functools.partial(_ffn_kernel, final_norm=final_norm),
        grid=(m // tm, ff // tf),
        in_specs=[
            pl.BlockSpec((tm, d), lambda i, f: (i, 0)),
            pl.BlockSpec((1, d), lambda i, f: (0, 0)),
            pl.BlockSpec((None, d, tf), lambda i, f: (layer, 0, f)),
            pl.BlockSpec((None, d, tf), lambda i, f: (layer, 0, f)),
            pl.BlockSpec((None, tf, d), lambda i, f: (layer, f, 0)),
            pl.BlockSpec((1, d), lambda i, f: (0, 0)),
        ],
        out_specs=pl.BlockSpec((tm, d), lambda i, f: (i, 0)),
        out_shape=jax.ShapeDtypeStruct((m, d), F32),
        scratch_shapes=[pltpu.VMEM((tm, d), BF16)],
        compiler_params=_cparams(("parallel", "arbitrary")),
        name="ffn_residual",
    )(h2d, g_row, w_gate, w_up, w_down, gfin_row)


def _moba_tables(n_heads, seq_len):
    nb = seq_len // KEY_BLOCK
    nt = MOBA_BIAS_TERMS
    assert nb + 2 * nt <= LANES
    slopes = np.exp2(-8.0 * np.arange(1, n_heads + 1, dtype=np.float64) / n_heads) * LOG2_E
    bias = slopes[:, None] * np.arange(seq_len, dtype=np.float64)[None, :]
    top = 2.0 ** math.ceil(math.log2(LOG2_E * seq_len))

    terms, rest, quantum = [], bias, top / 256.0
    for _ in range(nt - 1):
        term = np.floor(rest / quantum) * quantum
        terms.append(term)
        rest = rest - term
        quantum = quantum / 256.0
    terms.append(rest)

    qx = np.zeros((n_heads, seq_len, LANES), np.float64)
    kx = np.zeros((n_heads, seq_len, LANES), np.float64)
    kx[:, np.arange(seq_len), np.arange(seq_len) // KEY_BLOCK] = 1.0
    for c, term in enumerate(terms):
        qx[:, :, nb + c] = -term
        qx[:, :, nb + nt + c] = 1.0
        kx[:, :, nb + c] = 1.0
        kx[:, :, nb + nt + c] = term
    return jnp.asarray(qx.astype(jnp.bfloat16)), jnp.asarray(kx.astype(jnp.bfloat16))


def _moba_kernel(q_ref, k_ref, v_ref, qx_ref, kx_ref, o_ref, kmean_scr, causal_scr, qa_scr, ka_scr, va_scr,
                 s_scr, p_scr):
    kb = KEY_BLOCK
    hd = HEAD_DIM
    rc_rows = MOBA_ROW_CHUNK
    seq = k_ref.shape[1]
    nb = seq // kb
    nh = qx_ref.shape[0]
    neg_inf = -jnp.inf
    nt = (((1,), (1,)), ((), ()))

    row = lax.broadcasted_iota(jnp.int32, (kb, kb), 0)
    colk = lax.broadcasted_iota(jnp.int32, (kb, kb), 1)
    causal_scr[...] = jnp.where(row >= colk, 0.0, neg_inf)
    for hh in range(nh):
        _moba_prepare(hh, q_ref, k_ref, v_ref, qx_ref, kx_ref, kmean_scr, qa_scr, ka_scr, va_scr)

    for slot, t0 in enumerate(range(0, nb, MOBA_TILE_GROUP)):
        slot = slot % s_scr.shape[1]
        tiles = range(t0, min(t0 + MOBA_TILE_GROUP, nb))
        q0 = t0 * kb
        rows = len(tiles) * kb
        width = (tiles[-1] + 1) * kb
        shared = (t0 + 1) * kb
        for hh in range(nh):
            s_scr[hh, slot, 0:rows, 0:shared] = lax.dot_general(
                qa_scr[hh, q0:q0 + rows, :], ka_scr[hh, 0:shared, :], nt, preferred_element_type=F32)
            for own in tiles[1:]:
                base = (own - t0) * kb
                s_scr[hh, slot, base:rows, own * kb:(own + 1) * kb] = lax.dot_general(
                    qa_scr[hh, q0 + base:q0 + rows, :], ka_scr[hh, own * kb:(own + 1) * kb, :], nt,
                    preferred_element_type=F32)
        for hh in range(nh):
            for own in tiles:
                base = (own - t0) * kb
                for rc in range(kb // rc_rows):
                    r0, r1 = base + rc * rc_rows, base + (rc + 1) * rc_rows
                    parts = [s_scr[hh, slot, r0:r1, n * kb:(n + 1) * kb] for n in range(own)]
                    parts.append(s_scr[hh, slot, r0:r1, own * kb:(own + 1) * kb]
                                 + causal_scr[r0 - base:r1 - base, :])
                    m = jnp.max(functools.reduce(jnp.maximum, parts), axis=-1, keepdims=True)
                    for n in range(own + 1):
                        p_scr[hh, slot, r0:r1, n * kb:(n + 1) * kb] = jnp.exp2(parts[n] - m).astype(BF16)
        for hh in range(nh):
            for own in tiles:
                base = (own - t0) * kb
                res = jnp.dot(p_scr[hh, slot, base:base + kb, 0:(own + 1) * kb], va_scr[hh, 0:(own + 1) * kb, :],
                              preferred_element_type=F32)
                o_ref[0, own * kb:(own + 1) * kb, hh * hd:(hh + 1) * hd] = (
                    res[:, :hd] / res[:, hd:]).astype(o_ref.dtype)


def _moba_prepare(hh, q_ref, k_ref, v_ref, qx_ref, kx_ref, kmean_scr, qa_scr, ka_scr, va_scr):
    kb = KEY_BLOCK
    hd = HEAD_DIM
    seq = k_ref.shape[1]
    nb = seq // kb
    neg_inf = -jnp.inf
    nt = (((1,), (1,)), ((), ()))
    cols = slice(hh * hd, (hh + 1) * hd)

    qa_scr[hh, :, :hd] = q_ref[0, :, cols]
    ka_scr[hh, :, :hd] = k_ref[0, :, cols]
    ka_scr[hh, :, hd:] = kx_ref[hh]
    va_scr[hh, :, :hd] = v_ref[0, :, cols]
    va_scr[hh, :, hd:] = jnp.ones((seq, hd), BF16)
    kmean_scr[hh] = jnp.zeros(kmean_scr.shape[1:], F32)
    for n in range(nb):
        kmean_scr[hh, n:n + 1, :] = jnp.mean(k_ref[0, n * kb:(n + 1) * kb, cols].astype(F32), axis=0, keepdims=True)

    g0 = min((TOP_K_BLOCKS + 1) * kb, seq)
    qa_scr[hh, :g0, hd:] = qx_ref[hh, :g0, :]
    if seq > g0:
        ng = seq - g0
        q_gated = q_ref[0, g0:, cols]
        rest = kmean_scr[hh]
        gate = jnp.zeros((ng, LANES), F32)
        for _ in range(3):
            term = rest.astype(BF16)
            gate = gate + lax.dot_general(q_gated, term, nt, preferred_element_type=F32)
            rest = rest - term.astype(F32)
        col = lax.broadcasted_iota(jnp.int32, (ng, LANES), 1)
        colf = col.astype(F32)
        own_v = lax.div(lax.broadcasted_iota(jnp.int32, (ng, LANES), 0), jnp.int32(kb)) + (TOP_K_BLOCKS + 1)
        past = col < own_v
        gate = jnp.where(past, gate, neg_inf)
        mask = jnp.where(past, MASK_BIG, 0.0)
        for _ in range(TOP_K_BLOCKS):
            mx = jnp.max(gate, axis=-1, keepdims=True)
            idx = jnp.min(jnp.where(gate == mx, colf, float(LANES)), axis=-1, keepdims=True)
            hit = colf == idx
            mask = jnp.where(hit, 0.0, mask)
            gate = jnp.where(hit, neg_inf, gate)
        qa_scr[hh, g0:, hd:] = (qx_ref[hh, g0:, :].astype(F32) + mask).astype(BF16)


def _moba_attention(qkv3d, d_model):
    b, l, _ = qkv3d.shape
    n_heads = d_model // HEAD_DIM
    kb = KEY_BLOCK
    nh = math.gcd(n_heads, MOBA_HEADS)
    wide = nh * HEAD_DIM
    qx, kx = _moba_tables(n_heads, l)
    return pl.pallas_call(
        _moba_kernel,
        grid=(b, n_heads // nh),
        in_specs=[
            pl.BlockSpec((1, l, wide), lambda i, h: (i, 0, h)),
            pl.BlockSpec((1, l, wide), lambda i, h: (i, 0, n_heads // nh + h)),
            pl.BlockSpec((1, l, wide), lambda i, h: (i, 0, 2 * (n_heads // nh) + h)),
            pl.BlockSpec((nh, l, LANES), lambda i, h: (h, 0, 0)),
            pl.BlockSpec((nh, l, LANES), lambda i, h: (h, 0, 0)),
        ],
        out_specs=pl.BlockSpec((1, l, wide), lambda i, h: (i, 0, h)),
        out_shape=jax.ShapeDtypeStruct((b, l, d_model), BF16),
        scratch_shapes=[
            pltpu.VMEM((nh, LANES, HEAD_DIM), F32),
            pltpu.VMEM((kb, kb), F32),
            pltpu.VMEM((nh, l, HEAD_DIM + LANES), BF16),
            pltpu.VMEM((nh, l, HEAD_DIM + LANES), BF16),
            pltpu.VMEM((nh, l, 2 * HEAD_DIM), BF16),
            pltpu.VMEM((nh, MOBA_SLOTS, MOBA_TILE_GROUP * kb, l), F32),
            pltpu.VMEM((nh, MOBA_SLOTS, MOBA_TILE_GROUP * kb, l), BF16),
        ],
        compiler_params=_cparams(("parallel", "parallel")),
        name="moba_attention",
    )(qkv3d, qkv3d, qkv3d, qx, kx)


def kernel(x, g_mix, lambda_re, lambda_im, log_dt, b_re, b_im, c_re, c_im, d_skip, w_glu, b_glu, g_kv, w_kv,
           w_q, w_o, g_ffn, w_gate, w_up, w_down, g_final):
    bsz, l, d = x.shape
    depth = g_mix.shape[0]
    n_a = lambda_re.shape[0]
    m = bsz * l
    assert l % KEY_BLOCK == 0 and d % HEAD_DIM == 0 and l // KEY_BLOCK <= LANES
    row = lambda v: v.reshape(1, -1).astype(F32)

    h = x.astype(F32).reshape(m, d)
    qkv = None
    for i in range(depth):
        if i < n_a:
            tables = _s5_tables(lambda_re[i], lambda_im[i], log_dt[i], b_re[i], b_im[i], c_re[i], c_im[i])
            u = _rmsnorm_interleaved(h, row(g_mix[i]), bsz)
            z = _s5_mixer(u, row(d_skip[i]), *tables)
            h = _glu_residual(z, w_glu, i, row(b_glu[i]), h)
        else:
            j = i - n_a
            col = lambda v: v.reshape(-1, 1).astype(F32)
            qkv_i = _qkv_proj(h, col(g_mix[i]), col(g_kv), w_q, j, w_kv, HEAD_DIM ** -0.5 * LOG2_E)
            if qkv is None:
                qkv = qkv_i
            else:
                qkv = jnp.concatenate([qkv_i[:, :d], qkv[:, d:]], axis=1)
            attn = _moba_attention(qkv.reshape(bsz, l, 3 * d), d)
            h = _proj_residual(attn.reshape(m, d), w_o, j, h)
        h = _ffn_residual(h, row(g_ffn[i]), w_gate, w_up, w_down, i, row(g_final), final_norm=(i == depth - 1))
    return h.reshape(bsz, l, d).astype(x.dtype)
```

```python
import functools
import math

import jax
import jax.numpy as jnp
import numpy as np
from jax import lax
from jax.experimental import pallas as pl
from jax.experimental.pallas import tpu as pltpu

F32 = jnp.float32
BF16 = jnp.bfloat16

EPS = 1e-6
LANES = 128
HEAD_DIM = 128
KEY_BLOCK = 256
TOP_K_BLOCKS = 3
MOBA_ROW_CHUNK = 16
MOBA_TILE_GROUP = 2
MOBA_HEADS = 4
MOBA_SLOTS = 1
MOBA_BIAS_TERMS = 4
LOG2_E = math.log2(math.e)
MASK_BIG = -2.0 ** 100
SSM_CHUNK = 8
S5_GROUP = 16
V7X_VMEM_BYTES = 64 * 1024 * 1024
VMEM_LIMIT = V7X_VMEM_BYTES // 8 * 7


def _cparams(sem):
    return pltpu.CompilerParams(dimension_semantics=sem, vmem_limit_bytes=VMEM_LIMIT)


def _rms_scale(x):
    return lax.rsqrt(jnp.mean(x * x, axis=-1, keepdims=True) + EPS)


def _rmsnorm_kernel(x_ref, g_ref, o_ref):
    x = x_ref[...]
    o_ref[...] = (x * _rms_scale(x) * g_ref[...]).reshape(o_ref.shape)


def _rmsnorm_interleaved(x2d, g_row, bsz, tm=1024):
    m, d = x2d.shape
    assert m % (bsz * tm) == 0 and tm % S5_GROUP == 0
    tiles = m // bsz // tm
    return pl.pallas_call(
        _rmsnorm_kernel,
        grid=(bsz, tiles),
        in_specs=[pl.BlockSpec((tm, d), lambda b, i: (b * tiles + i, 0)), pl.BlockSpec((1, d), lambda b, i: (0, 0))],
        out_specs=pl.BlockSpec((tm // S5_GROUP, None, S5_GROUP, d), lambda b, i: (i, b, 0, 0)),
        out_shape=jax.ShapeDtypeStruct((m // bsz // S5_GROUP, bsz, S5_GROUP, d), F32),
        compiler_params=_cparams(("parallel", "parallel")),
        name="rmsnorm",
    )(x2d, g_row)


def _s5_tables(lam_re, lam_im, log_dt, b_re, b_im, c_re, c_im):
    g, p, c = b_re.shape
    t = SSM_CHUNK
    gl = LANES // c
    nj = g // gl
    ns = gl * p
    lr, li = lam_re.astype(F32), lam_im.astype(F32)
    dt = jnp.exp(log_dt.astype(F32))[:, None]
    ar, ai = lr * dt, li * dt
    mag = jnp.exp(ar)
    nr, ni = mag * jnp.cos(ai) - 1.0, mag * jnp.sin(ai)
    den = lr * lr + li * li
    fr, fi = (nr * lr + ni * li) / den, (ni * lr - nr * li) / den
    br, bi = b_re.astype(F32), b_im.astype(F32)
    bbr = fr[..., None] * br - fi[..., None] * bi
    bbi = fr[..., None] * bi + fi[..., None] * br
    dd = jnp.arange(t + 1, dtype=F32)[:, None, None]
    pmag = jnp.exp(ar[None] * dd)
    pr, pi = pmag * jnp.cos(ai[None] * dd), pmag * jnp.sin(ai[None] * dd)

    def per_tile(x):
        return jnp.transpose(x.reshape(nj, gl, c, p), (0, 2, 1, 3)).reshape(nj, c, ns)

    bcmp = jnp.concatenate([per_tile(jnp.swapaxes(bbr, 1, 2)), per_tile(jnp.swapaxes(bbi, 1, 2))], axis=-1)
    ccmp = jnp.concatenate([per_tile(c_re.astype(F32)), per_tile(c_im.astype(F32))], axis=-1)
    pw = jnp.concatenate([pr.reshape(t + 1, nj, ns), pi.reshape(t + 1, nj, ns)], axis=-1)
    pw = jnp.pad(jnp.swapaxes(pw, 0, 1), ((0, 0), (0, 16 - (t + 1)), (0, 0)))
    return bcmp, ccmp, pw


def _s5_build_tables(bcmp_ref, ccmp_ref, pw_ref, w1_scr, w2_scr):
    t = SSM_CHUNK
    cg = bcmp_ref.shape[1]
    ns = bcmp_ref.shape[2] // 2
    sg = ns * cg // LANES
    yw = t * LANES
    same_group = (lax.broadcasted_iota(jnp.int32, (LANES, ns), 0) // cg
                  == lax.broadcasted_iota(jnp.int32, (LANES, ns), 1) // sg)

    def block_diag(x):
        return jnp.where(same_group, jnp.tile(x, (LANES // cg, 1)), 0.0)

    bre, bim = bcmp_ref[0, :, :ns], bcmp_ref[0, :, ns:]
    cre, cim = ccmp_ref[0, :, :ns], ccmp_ref[0, :, ns:]
    bcat16 = jnp.concatenate([block_diag(bre), block_diag(bim)], axis=1).astype(BF16)
    nt = (((1,), (1,)), ((), ()))
    for d in range(t + 1):
        pr, pi = pw_ref[0, d:d + 1, :ns], pw_ref[0, d:d + 1, ns:]
        mo = jnp.concatenate([block_diag(cre * pr - cim * pi), block_diag(-(cre * pi + cim * pr))],
                             axis=1).astype(BF16)
        if d >= 1:
            w2_scr[(d - 1) * LANES:d * LANES, :] = mo
        if d < t:
            kd = lax.dot_general(bcat16, mo, nt, preferred_element_type=F32).astype(BF16)
            for ip in range(t - d):
                w1_scr[ip * LANES:(ip + 1) * LANES, (ip + d) * LANES:(ip + d + 1) * LANES] = kd
            ip = t - 1 - d
            w1_scr[ip * LANES:(ip + 1) * LANES, yw:yw + ns] = block_diag(bre * pr - bim * pi).astype(BF16)
            w1_scr[ip * LANES:(ip + 1) * LANES, yw + ns:] = block_diag(bre * pi + bim * pr).astype(BF16)
    for ip in range(1, t):
        w1_scr[ip * LANES:(ip + 1) * LANES, 0:ip * LANES] = jnp.zeros((LANES, ip * LANES), BF16)


def _s5_kernel(*refs, n_seq, n_jobs):
    u_ref, d_ref, bcmp_ref, ccmp_ref, pw_ref = refs[:5]
    job_in = refs[5:5 + 2 * n_jobs]
    z_ref = refs[5 + 2 * n_jobs]
    job_out = refs[6 + 2 * n_jobs:6 + 3 * n_jobs]
    w1_scr, w2_scr, lhs_scr, s_scr, yc_scr, y_scr = refs[6 + 3 * n_jobs:]
    for k in range(n_jobs):
        job_out[k][...] = (job_in[2 * k][...] * job_in[2 * k + 1][...]).astype(BF16)
    t = SSM_CHUNK
    cpg = S5_GROUP // t
    rows = lhs_scr.shape[0]
    ngroups = rows // (n_seq * cpg)
    nsl = s_scr.shape[0] // 2
    yw = t * LANES

    _s5_build_tables(bcmp_ref, ccmp_ref, pw_ref, w1_scr, w2_scr)

    for i in range(t):
        lhs_scr[:, i * LANES:(i + 1) * LANES] = u_ref[pl.ds(i, rows, stride=t), :].astype(BF16)
    strip = 2 * LANES
    for c0 in range(0, yw, strip):
        c1 = c0 + strip
        yc_scr[:, c0:c1] = jnp.dot(lhs_scr[:, :c1], w1_scr[:c1, c0:c1], preferred_element_type=F32)
    s_in_all = jnp.dot(lhs_scr[...], w1_scr[:, yw:], preferred_element_type=F32)
    for q in range(2 * nsl):
        s_scr[q] = s_in_all[:, q * LANES:(q + 1) * LANES]

    a_re = [pw_ref[0, t:t + 1, q * LANES:(q + 1) * LANES] for q in range(nsl)]
    a_im = [pw_ref[0, t:t + 1, (nsl + q) * LANES:(nsl + q + 1) * LANES] for q in range(nsl)]

    def scan_group(g, carry):
        e = list(carry)
        for c in range(cpg):
            sel = pl.ds(g * (n_seq * cpg) + c, n_seq, stride=cpg)
            for q in range(nsl):
                e_re, e_im = e[q], e[nsl + q]
                in_re, in_im = s_scr[q, sel, :], s_scr[nsl + q, sel, :]
                s_scr[q, sel, :] = e_re
                s_scr[nsl + q, sel, :] = e_im
                e[q] = a_re[q] * e_re - a_im[q] * e_im + in_re
                e[nsl + q] = a_re[q] * e_im + a_im[q] * e_re + in_im
        return tuple(e)

    lax.fori_loop(0, ngroups, scan_group, (jnp.zeros((n_seq, LANES), F32),) * (2 * nsl))

    s_prev = jnp.concatenate([s_scr[q].astype(BF16) for q in range(2 * nsl)], axis=1)
    y = yc_scr[...] + lax.dot_general(s_prev, w2_scr[...], (((1,), (1,)), ((), ())), preferred_element_type=F32)
    for i in range(t):
        y_scr[pl.ds(i, rows, stride=t), :] = y[:, i * LANES:(i + 1) * LANES]

    yy = y_scr[...] + d_ref[...] * u_ref[...]
    z_ref[...] = jax.nn.gelu(yy).astype(z_ref.dtype)


def _s5_mixer(u4d, d_row, bcmp, ccmp, pw, weight_jobs=()):
    ngr, bsz, grp, d = u4d.shape
    m = ngr * bsz * grp
    assert grp == S5_GROUP and d % LANES == 0
    nj = d // LANES
    rows = m // SSM_CHUNK
    cg, ns2 = bcmp.shape[1:]
    yw = SSM_CHUNK * LANES
    job_specs, job_args, job_out_specs, job_out_shapes = [], [], [], []
    for w, layer, gain_col in weight_jobs:
        k, n = w.shape[-2:]
        assert k % (nj * 16) == 0
        if layer is None:
            job_specs.append(pl.BlockSpec((k // nj, n), lambda j: (j, 0)))
        else:
            job_specs.append(pl.BlockSpec((None, k // nj, n), lambda j, layer=layer: (layer, j, 0)))
        job_specs.append(pl.BlockSpec((k // nj, 1), lambda j: (j, 0)))
        job_args += [w, gain_col]
        job_out_specs.append(pl.BlockSpec((k // nj, n), lambda j: (j, 0)))
        job_out_shapes.append(jax.ShapeDtypeStruct((k, n), BF16))
    z, *cast = pl.pallas_call(
        functools.partial(_s5_kernel, n_seq=bsz, n_jobs=len(weight_jobs)),
        grid=(nj,),
        in_specs=[
            pl.BlockSpec((m, LANES), lambda j: (0, j)),
            pl.BlockSpec((1, LANES), lambda j: (0, j)),
            pl.BlockSpec((1, cg, ns2), lambda j: (j, 0, 0)),
            pl.BlockSpec((1, cg, ns2), lambda j: (j, 0, 0)),
            pl.BlockSpec((1, 16, ns2), lambda j: (j, 0, 0)),
        ] + job_specs,
        out_specs=[pl.BlockSpec((m, LANES), lambda j: (0, j))] + job_out_specs,
        out_shape=[jax.ShapeDtypeStruct((m, d), BF16)] + job_out_shapes,
        scratch_shapes=[
            pltpu.VMEM((yw, yw + ns2), BF16),
            pltpu.VMEM((yw, ns2), BF16),
            pltpu.VMEM((rows, yw), BF16),
            pltpu.VMEM((ns2 // LANES, rows, LANES), F32),
            pltpu.VMEM((rows, yw), F32),
            pltpu.VMEM((m, LANES), F32),
        ],
        compiler_params=_cparams(("parallel",)),
        name="s5_mixer",
    )(u4d.reshape(m, d), d_row, bcmp, ccmp, pw, *job_args)
    return z.reshape(ngr, bsz, grp, d), cast


def _glu_kernel(z_ref, w_ref, b_ref, res_ref, o_ref, w16_scr):
    tm, n = o_ref.shape

    @pl.when((pl.program_id(0) == 0) & (pl.program_id(1) == 0))
    def _():
        w16_scr[...] = w_ref[...].astype(BF16)

    z = z_ref[...].reshape(tm, n)
    a = jnp.dot(z, w16_scr[...], preferred_element_type=F32) + b_ref[...]
    o_ref[...] = res_ref[...] + z.astype(F32) * jax.nn.sigmoid(a)


def _glu_residual(z4d, w, layer, b_row, res, tm=512):
    ngr, bsz, grp, k = z4d.shape
    m = ngr * bsz * grp
    n = w.shape[2]
    assert k == n and (ngr * grp) % tm == 0 and tm % grp == 0
    tiles = ngr * grp // tm
    return pl.pallas_call(
        _glu_kernel,
        grid=(bsz, tiles),
        in_specs=[
            pl.BlockSpec((tm // grp, None, grp, k), lambda b, i: (i, b, 0, 0)),
            pl.BlockSpec((None, k, n), lambda b, i: (layer, 0, 0), pipeline_mode=pl.Buffered(1)),
            pl.BlockSpec((1, n), lambda b, i: (0, 0)),
            pl.BlockSpec((tm, n), lambda b, i: (b * tiles + i, 0)),
        ],
        out_specs=pl.BlockSpec((tm, n), lambda b, i: (b * tiles + i, 0)),
        out_shape=jax.ShapeDtypeStruct((m, n), F32),
        scratch_shapes=[pltpu.VMEM((k, n), BF16)],
        compiler_params=_cparams(("arbitrary", "arbitrary")),
        name="glu_residual",
    )(z4d, w, b_row, res)


def _proj_res_kernel(x_ref, w_ref, res_ref, o_ref, w16_scr):
    @pl.when(pl.program_id(0) == 0)
    def _():
        w16_scr[...] = w_ref[...].astype(BF16)

    o_ref[...] = res_ref[...] + jnp.dot(x_ref[...], w16_scr[...], preferred_element_type=F32)


def _proj_residual(x, w, layer, res, tm=512):
    m, k = x.shape
    n = w.shape[2]
    assert m % tm == 0
    return pl.pallas_call(
        _proj_res_kernel,
        grid=(m // tm,),
        in_specs=[
            pl.BlockSpec((tm, k), lambda i: (i, 0)),
            pl.BlockSpec((None, k, n), lambda i: (layer, 0, 0), pipeline_mode=pl.Buffered(1)),
            pl.BlockSpec((tm, n), lambda i: (i, 0)),
        ],
        out_specs=pl.BlockSpec((tm, n), lambda i: (i, 0)),
        out_shape=jax.ShapeDtypeStruct((m, n), F32),
        scratch_shapes=[pltpu.VMEM((k, n), BF16)],
        compiler_params=_cparams(("arbitrary",)),
        name="proj_residual",
    )(x, w, res)


def _qkv_kernel(x_ref, wq_ref, wkv_ref, o_ref, xn_scr, *, n_q_tiles, q_scale):
    j = pl.program_id(1)

    @pl.when(j == 0)
    def _():
        x = x_ref[...]
        xn_scr[...] = (x * _rms_scale(x)).astype(BF16)

    @pl.when(j < n_q_tiles)
    def _():
        q = jnp.dot(xn_scr[...], wq_ref[...], preferred_element_type=F32)
        o_ref[...] = (q * q_scale).astype(o_ref.dtype)

    @pl.when(j >= n_q_tiles)
    def _():
        o_ref[...] = jnp.dot(xn_scr[...], wkv_ref[...], preferred_element_type=F32).astype(o_ref.dtype)


def _qkv_proj(h2d, wq16, wkv16, q_scale, tm=1024, tn=1024):
    m, k = h2d.shape
    assert m % tm == 0 and wq16.shape[1] % tn == 0 and wkv16.shape[1] % tn == 0
    nq = wq16.shape[1] // tn
    nkv = wkv16.shape[1] // tn
    return pl.pallas_call(
        functools.partial(_qkv_kernel, n_q_tiles=nq, q_scale=q_scale),
        grid=(m // tm, nq + nkv),
        in_specs=[
            pl.BlockSpec((tm, k), lambda i, j: (i, 0)),
            pl.BlockSpec((k, tn), lambda i, j: (0, jnp.minimum(j, nq - 1))),
            pl.BlockSpec((k, tn), lambda i, j: (0, jnp.maximum(j - nq, 0))),
        ],
        out_specs=pl.BlockSpec((tm, tn), lambda i, j: (i, j)),
        out_shape=jax.ShapeDtypeStruct((m, (nq + nkv) * tn), BF16),
        scratch_shapes=[pltpu.VMEM((tm, k), BF16)],
        compiler_params=_cparams(("parallel", "arbitrary")),
        name="qkv_proj",
    )(h2d, wq16, wkv16)


def _ffn_kernel(h_ref, g_ref, wg_ref, wu_ref, wd_ref, gfin_ref, o_ref, u_scr, *, final_norm):
    f = pl.program_id(1)

    @pl.when(f == 0)
    def _():
        x = h_ref[...]
        u_scr[...] = (x * _rms_scale(x) * g_ref[...]).astype(BF16)
        o_ref[...] = x

    u = u_scr[...]
    gate = jnp.dot(u, wg_ref[...].astype(BF16), preferred_element_type=F32)
    up = jnp.dot(u, wu_ref[...].astype(BF16), preferred_element_type=F32)
    act = (jax.nn.silu(gate) * up).astype(BF16)
    o_ref[...] += jnp.dot(act, wd_ref[...].astype(BF16), preferred_element_type=F32)

    if final_norm:
        @pl.when(f == pl.num_programs(1) - 1)
        def _():
            y = o_ref[...]
            o_ref[...] = y * _rms_scale(y) * gfin_ref[...]


def _ffn_residual(h2d, g_row, w_gate, w_up, w_down, layer, gfin_row, final_norm, tm=1024, tf=256):
    m, d = h2d.shape
    ff = w_gate.shape[2]
    assert m % tm == 0 and ff % tf == 0
    return pl.pallas_call(
        functools.partial(_ffn_kernel, final_norm=final_norm),
        grid=(m // tm, ff // tf),
        in_specs=[
            pl.BlockSpec((tm, d), lambda i, f: (i, 0)),
            pl.BlockSpec((1, d), lambda i, f: (0, 0)),
            pl.BlockSpec((None, d, tf), lambda i, f: (layer, 0, f)),
            pl.BlockSpec((None, d, tf), lambda i, f: (layer, 0, f)),
            pl.BlockSpec((None, tf, d), lambda i, f: (layer, f, 0)),
            pl.BlockSpec((1, d), lambda i, f: (0, 0)),
        ],
        out_specs=pl.BlockSpec((tm, d), lambda i, f: (i, 0)),
        out_shape=jax.ShapeDtypeStruct((m, d), F32),
        scratch_shapes=[pltpu.VMEM((tm, d), BF16)],
        compiler_params=_cparams(("parallel", "arbitrary")),
        name="ffn_residual",
    )(h2d, g_row, w_gate, w_up, w_down, gfin_row)


def _moba_tables(n_heads, seq_len):
    nb = seq_len // KEY_BLOCK
    nt = MOBA_BIAS_TERMS
    assert nb + 2 * nt <= LANES
    slopes = np.exp2(-8.0 * np.arange(1, n_heads + 1, dtype=np.float64) / n_heads) * LOG2_E
    bias = slopes[:, None] * np.arange(seq_len, dtype=np.float64)[None, :]
    top = 2.0 ** math.ceil(math.log2(LOG2_E * seq_len))

    terms, rest, quantum = [], bias, top / 256.0
    for _ in range(nt - 1):
        term = np.floor(rest / quantum) * quantum
        terms.append(term)
        rest = rest - term
        quantum = quantum / 256.0
    terms.append(rest)

    qx = np.zeros((n_heads, seq_len, LANES), np.float64)
    kx = np.zeros((n_heads, seq_len, LANES), np.float64)
    kx[:, np.arange(seq_len), np.arange(seq_len) // KEY_BLOCK] = 1.0
    for c, term in enumerate(terms):
        qx[:, :, nb + c] = -term
        qx[:, :, nb + nt + c] = 1.0
        kx[:, :, nb + c] = 1.0
        kx[:, :, nb + nt + c] = term
    return jnp.asarray(qx.astype(jnp.bfloat16)), jnp.asarray(kx.astype(jnp.bfloat16))


def _moba_kernel(q_ref, k_ref, v_ref, qx_ref, kx_ref, o_ref, kmean_scr, causal_scr, qa_scr, ka_scr, va_scr,
                 s_scr, p_scr):
    kb = KEY_BLOCK
    hd = HEAD_DIM
    rc_rows = MOBA_ROW_CHUNK
    seq = k_ref.shape[1]
    nb = seq // kb
    nh = qx_ref.shape[0]
    neg_inf = -jnp.inf
    nt = (((1,), (1,)), ((), ()))

    row = lax.broadcasted_iota(jnp.int32, (kb, kb), 0)
    colk = lax.broadcasted_iota(jnp.int32, (kb, kb), 1)
    causal_scr[...] = jnp.where(row >= colk, 0.0, neg_inf)
    for hh in range(nh):
        _moba_prepare(hh, q_ref, k_ref, v_ref, qx_ref, kx_ref, kmean_scr, qa_scr, ka_scr, va_scr)

    for slot, t0 in enumerate(range(0, nb, MOBA_TILE_GROUP)):
        slot = slot % s_scr.shape[1]
        tiles = range(t0, min(t0 + MOBA_TILE_GROUP, nb))
        q0 = t0 * kb
        rows = len(tiles) * kb
        width = (tiles[-1] + 1) * kb
        shared = (t0 + 1) * kb
        for hh in range(nh):
            s_scr[hh, slot, 0:rows, 0:shared] = lax.dot_general(
                qa_scr[hh, q0:q0 + rows, :], ka_scr[hh, 0:shared, :], nt, preferred_element_type=F32)
            for own in tiles[1:]:
                base = (own - t0) * kb
                s_scr[hh, slot, base:rows, own * kb:(own + 1) * kb] = lax.dot_general(
                    qa_scr[hh, q0 + base:q0 + rows, :], ka_scr[hh, own * kb:(own + 1) * kb, :], nt,
                    preferred_element_type=F32)
        for hh in range(nh):
            for own in tiles:
                base = (own - t0) * kb
                for rc in range(kb // rc_rows):
                    r0, r1 = base + rc * rc_rows, base + (rc + 1) * rc_rows
                    parts = [s_scr[hh, slot, r0:r1, n * kb:(n + 1) * kb] for n in range(own)]
                    parts.append(s_scr[hh, slot, r0:r1, own * kb:(own + 1) * kb]
                                 + causal_scr[r0 - base:r1 - base, :])
                    m = jnp.max(functools.reduce(jnp.maximum, parts), axis=-1, keepdims=True)
                    for n in range(own + 1):
                        p_scr[hh, slot, r0:r1, n * kb:(n + 1) * kb] = jnp.exp2(parts[n] - m).astype(BF16)
        for hh in range(nh):
            for own in tiles:
                base = (own - t0) * kb
                res = jnp.dot(p_scr[hh, slot, base:base + kb, 0:(own + 1) * kb], va_scr[hh, 0:(own + 1) * kb, :],
                              preferred_element_type=F32)
                o_ref[0, own * kb:(own + 1) * kb, hh * hd:(hh + 1) * hd] = (
                    res[:, :hd] / res[:, hd:]).astype(o_ref.dtype)


def _moba_prepare(hh, q_ref, k_ref, v_ref, qx_ref, kx_ref, kmean_scr, qa_scr, ka_scr, va_scr):
    kb = KEY_BLOCK
    hd = HEAD_DIM
    seq = k_ref.shape[1]
    nb = seq // kb
    neg_inf = -jnp.inf
    nt = (((1,), (1,)), ((), ()))
    cols = slice(hh * hd, (hh + 1) * hd)

    qa_scr[hh, :, :hd] = q_ref[0, :, cols]
    ka_scr[hh, :, :hd] = k_ref[0, :, cols]
    ka_scr[hh, :, hd:] = kx_ref[hh]
    va_scr[hh, :, :hd] = v_ref[0, :, cols]
    va_scr[hh, :, hd:] = jnp.ones((seq, hd), BF16)
    kmean_scr[hh] = jnp.zeros(kmean_scr.shape[1:], F32)
    for n in range(nb):
        kmean_scr[hh, n:n + 1, :] = jnp.mean(k_ref[0, n * kb:(n + 1) * kb, cols].astype(F32), axis=0, keepdims=True)

    g0 = min((TOP_K_BLOCKS + 1) * kb, seq)
    qa_scr[hh, :g0, hd:] = qx_ref[hh, :g0, :]
    if seq > g0:
        ng = seq - g0
        q_gated = q_ref[0, g0:, cols]
        rest = kmean_scr[hh]
        gate = jnp.zeros((ng, LANES), F32)
        for _ in range(3):
            term = rest.astype(BF16)
            gate = gate + lax.dot_general(q_gated, term, nt, preferred_element_type=F32)
            rest = rest - term.astype(F32)
        col = lax.broadcasted_iota(jnp.int32, (ng, LANES), 1)
        colf = col.astype(F32)
        own_v = lax.div(lax.broadcasted_iota(jnp.int32, (ng, LANES), 0), jnp.int32(kb)) + (TOP_K_BLOCKS + 1)
        past = col < own_v
        gate = jnp.where(past, gate, neg_inf)
        mask = jnp.where(past, MASK_BIG, 0.0)
        for _ in range(TOP_K_BLOCKS):
            mx = jnp.max(gate, axis=-1, keepdims=True)
            idx = jnp.min(jnp.where(gate == mx, colf, float(LANES)), axis=-1, keepdims=True)
            hit = colf == idx
            mask = jnp.where(hit, 0.0, mask)
            gate = jnp.where(hit, neg_inf, gate)
        qa_scr[hh, g0:, hd:] = (qx_ref[hh, g0:, :].astype(F32) + mask).astype(BF16)


def _moba_attention(qkv3d, d_model):
    b, l, _ = qkv3d.shape
    n_heads = d_model // HEAD_DIM
    kb = KEY_BLOCK
    nh = math.gcd(n_heads, MOBA_HEADS)
    wide = nh * HEAD_DIM
    qx, kx = _moba_tables(n_heads, l)
    return pl.pallas_call(
        _moba_kernel,
        grid=(b, n_heads // nh),
        in_specs=[
            pl.BlockSpec((1, l, wide), lambda i, h: (i, 0, h)),
            pl.BlockSpec((1, l, wide), lambda i, h: (i, 0, n_heads // nh + h)),
            pl.BlockSpec((1, l, wide), lambda i, h: (i, 0, 2 * (n_heads // nh) + h)),
            pl.BlockSpec((nh, l, LANES), lambda i, h: (h, 0, 0)),
            pl.BlockSpec((nh, l, LANES), lambda i, h: (h, 0, 0)),
        ],
        out_specs=pl.BlockSpec((1, l, wide), lambda i, h: (i, 0, h)),
        out_shape=jax.ShapeDtypeStruct((b, l, d_model), BF16),
        scratch_shapes=[
            pltpu.VMEM((nh, LANES, HEAD_DIM), F32),
            pltpu.VMEM((kb, kb), F32),
            pltpu.VMEM((nh, l, HEAD_DIM + LANES), BF16),
            pltpu.VMEM((nh, l, HEAD_DIM + LANES), BF16),
            pltpu.VMEM((nh, l, 2 * HEAD_DIM), BF16),
            pltpu.VMEM((nh, MOBA_SLOTS, MOBA_TILE_GROUP * kb, l), F32),
            pltpu.VMEM((nh, MOBA_SLOTS, MOBA_TILE_GROUP * kb, l), BF16),
        ],
        compiler_params=_cparams(("parallel", "parallel")),
        name="moba_attention",
    )(qkv3d, qkv3d, qkv3d, qx, kx)


def kernel(x, g_mix, lambda_re, lambda_im, log_dt, b_re, b_im, c_re, c_im, d_skip, w_glu, b_glu, g_kv, w_kv,
           w_q, w_o, g_ffn, w_gate, w_up, w_down, g_final):
    bsz, l, d = x.shape
    depth = g_mix.shape[0]
    n_a = lambda_re.shape[0]
    m = bsz * l
    assert l % KEY_BLOCK == 0 and d % HEAD_DIM == 0 and l // KEY_BLOCK <= LANES
    row = lambda v: v.reshape(1, -1).astype(F32)
    col = lambda v: v.reshape(-1, 1).astype(F32)

    h = x.astype(F32).reshape(m, d)
    qkv = None
    proj_weights = None
    for i in range(depth):
        if i < n_a:
            tables = _s5_tables(lambda_re[i], lambda_im[i], log_dt[i], b_re[i], b_im[i], c_re[i], c_im[i])
            u = _rmsnorm_interleaved(h, row(g_mix[i]), bsz)
            jobs = ()
            if i == n_a - 1 and depth > n_a:
                jobs = ((w_q, 0, col(g_mix[n_a])), (w_kv, None, col(g_kv)))
            z, cast = _s5_mixer(u, row(d_skip[i]), *tables, weight_jobs=jobs)
            proj_weights = cast or None
            h = _glu_residual(z, w_glu, i, row(b_glu[i]), h)
        else:
            j = i - n_a
            if j > 0 or proj_weights is None:
                proj_weights = [(w_q[j] * col(g_mix[i])).astype(BF16), (w_kv * col(g_kv)).astype(BF16)]
            qkv_i = _qkv_proj(h, *proj_weights, HEAD_DIM ** -0.5 * LOG2_E)
            if qkv is None:
                qkv = qkv_i
            else:
                qkv = jnp.concatenate([qkv_i[:, :d], qkv[:, d:]], axis=1)
            attn = _moba_attention(qkv.reshape(bsz, l, 3 * d), d)
            h = _proj_residual(attn.reshape(m, d), w_o, j, h)
        h = _ffn_residual(h, row(g_ffn[i]), w_gate, w_up, w_down, i, row(g_final), final_norm=(i == depth - 1))
    return h.reshape(bsz, l, d).astype(x.dtype)
```

```python
import functools
import math

import jax
import jax.numpy as jnp
import numpy as np
from jax import lax
from jax.experimental import pallas as pl
from jax.experimental.pallas import tpu as pltpu

F32 = jnp.float32
BF16 = jnp.bfloat16

EPS = 1e-6
LANES = 128
HEAD_DIM = 128
KEY_BLOCK = 256
TOP_K_BLOCKS = 3
MOBA_ROW_CHUNK = 16
MOBA_TILE_GROUP = 2
MOBA_HEADS = 4
MOBA_SLOTS = 1
MOBA_BIAS_TERMS = 4
LOG2_E = math.log2(math.e)
MASK_BIG = -2.0 ** 100
SSM_CHUNK = 8
S5_GROUP = 16
V7X_VMEM_BYTES = 64 * 1024 * 1024
VMEM_LIMIT = V7X_VMEM_BYTES // 8 * 7


FFN_VMEM_LIMIT = V7X_VMEM_BYTES // 16 * 15


def _cparams(sem, vmem_limit=VMEM_LIMIT):
    return pltpu.CompilerParams(dimension_semantics=sem, vmem_limit_bytes=vmem_limit)


def _rms_scale(x):
    return lax.rsqrt(jnp.mean(x * x, axis=-1, keepdims=True) + EPS)


def _rmsnorm_kernel(x_ref, g_ref, o_ref):
    x = x_ref[...]
    o_ref[...] = (x * _rms_scale(x) * g_ref[...]).reshape(o_ref.shape)


def _rmsnorm_interleaved(x2d, g_row, bsz, tm=1024):
    m, d = x2d.shape
    assert m % (bsz * tm) == 0 and tm % S5_GROUP == 0
    tiles = m // bsz // tm
    return pl.pallas_call(
        _rmsnorm_kernel,
        grid=(bsz, tiles),
        in_specs=[pl.BlockSpec((tm, d), lambda b, i: (b * tiles + i, 0)), pl.BlockSpec((1, d), lambda b, i: (0, 0))],
        out_specs=pl.BlockSpec((tm // S5_GROUP, None, S5_GROUP, d), lambda b, i: (i, b, 0, 0)),
        out_shape=jax.ShapeDtypeStruct((m // bsz // S5_GROUP, bsz, S5_GROUP, d), F32),
        compiler_params=_cparams(("parallel", "parallel")),
        name="rmsnorm",
    )(x2d, g_row)


def _s5_tables(lam_re, lam_im, log_dt, b_re, b_im, c_re, c_im):
    g, p, c = b_re.shape
    t = SSM_CHUNK
    gl = LANES // c
    nj = g // gl
    ns = gl * p
    lr, li = lam_re.astype(F32), lam_im.astype(F32)
    dt = jnp.exp(log_dt.astype(F32))[:, None]
    ar, ai = lr * dt, li * dt
    mag = jnp.exp(ar)
    nr, ni = mag * jnp.cos(ai) - 1.0, mag * jnp.sin(ai)
    den = lr * lr + li * li
    fr, fi = (nr * lr + ni * li) / den, (ni * lr - nr * li) / den
    br, bi = b_re.astype(F32), b_im.astype(F32)
    bbr = fr[..., None] * br - fi[..., None] * bi
    bbi = fr[..., None] * bi + fi[..., None] * br
    dd = jnp.arange(t + 1, dtype=F32)[:, None, None]
    pmag = jnp.exp(ar[None] * dd)
    pr, pi = pmag * jnp.cos(ai[None] * dd), pmag * jnp.sin(ai[None] * dd)

    def per_tile(x):
        return jnp.transpose(x.reshape(nj, gl, c, p), (0, 2, 1, 3)).reshape(nj, c, ns)

    bcmp = jnp.concatenate([per_tile(jnp.swapaxes(bbr, 1, 2)), per_tile(jnp.swapaxes(bbi, 1, 2))], axis=-1)
    ccmp = jnp.concatenate([per_tile(c_re.astype(F32)), per_tile(c_im.astype(F32))], axis=-1)
    pw = jnp.concatenate([pr.reshape(t + 1, nj, ns), pi.reshape(t + 1, nj, ns)], axis=-1)
    pw = jnp.pad(jnp.swapaxes(pw, 0, 1), ((0, 0), (0, 16 - (t + 1)), (0, 0)))
    return bcmp, ccmp, pw


def _s5_build_tables(bcmp_ref, ccmp_ref, pw_ref, w1_scr, w2_scr):
    t = SSM_CHUNK
    cg = bcmp_ref.shape[1]
    ns = bcmp_ref.shape[2] // 2
    sg = ns * cg // LANES
    yw = t * LANES
    same_group = (lax.broadcasted_iota(jnp.int32, (LANES, ns), 0) // cg
                  == lax.broadcasted_iota(jnp.int32, (LANES, ns), 1) // sg)

    def block_diag(x):
        return jnp.where(same_group, jnp.tile(x, (LANES // cg, 1)), 0.0)

    bre, bim = bcmp_ref[0, :, :ns], bcmp_ref[0, :, ns:]
    cre, cim = ccmp_ref[0, :, :ns], ccmp_ref[0, :, ns:]
    bcat16 = jnp.concatenate([block_diag(bre), block_diag(bim)], axis=1).astype(BF16)
    nt = (((1,), (1,)), ((), ()))
    for d in range(t + 1):
        pr, pi = pw_ref[0, d:d + 1, :ns], pw_ref[0, d:d + 1, ns:]
        mo = jnp.concatenate([block_diag(cre * pr - cim * pi), block_diag(-(cre * pi + cim * pr))],
                             axis=1).astype(BF16)
        if d >= 1:
            w2_scr[(d - 1) * LANES:d * LANES, :] = mo
        if d < t:
            kd = lax.dot_general(bcat16, mo, nt, preferred_element_type=F32).astype(BF16)
            for ip in range(t - d):
                w1_scr[ip * LANES:(ip + 1) * LANES, (ip + d) * LANES:(ip + d + 1) * LANES] = kd
            ip = t - 1 - d
            w1_scr[ip * LANES:(ip + 1) * LANES, yw:yw + ns] = block_diag(bre * pr - bim * pi).astype(BF16)
            w1_scr[ip * LANES:(ip + 1) * LANES, yw + ns:] = block_diag(bre * pi + bim * pr).astype(BF16)
    for ip in range(1, t):
        w1_scr[ip * LANES:(ip + 1) * LANES, 0:ip * LANES] = jnp.zeros((LANES, ip * LANES), BF16)


def _s5_kernel(*refs, n_seq, n_jobs):
    u_ref, d_ref, bcmp_ref, ccmp_ref, pw_ref = refs[:5]
    job_in = refs[5:5 + 2 * n_jobs]
    z_ref = refs[5 + 2 * n_jobs]
    job_out = refs[6 + 2 * n_jobs:6 + 3 * n_jobs]
    w1_scr, w2_scr, lhs_scr, s_scr, yc_scr, y_scr = refs[6 + 3 * n_jobs:]
    for k in range(n_jobs):
        job_out[k][...] = (job_in[2 * k][...] * job_in[2 * k + 1][...]).astype(BF16)
    t = SSM_CHUNK
    cpg = S5_GROUP // t
    rows = lhs_scr.shape[0]
    ngroups = rows // (n_seq * cpg)
    nsl = s_scr.shape[0] // 2
    yw = t * LANES

    _s5_build_tables(bcmp_ref, ccmp_ref, pw_ref, w1_scr, w2_scr)

    for i in range(t):
        lhs_scr[:, i * LANES:(i + 1) * LANES] = u_ref[pl.ds(i, rows, stride=t), :].astype(BF16)
    strip = 2 * LANES
    for c0 in range(0, yw, strip):
        c1 = c0 + strip
        yc_scr[:, c0:c1] = jnp.dot(lhs_scr[:, :c1], w1_scr[:c1, c0:c1], preferred_element_type=F32)
    s_in_all = jnp.dot(lhs_scr[...], w1_scr[:, yw:], preferred_element_type=F32)
    for q in range(2 * nsl):
        s_scr[q] = s_in_all[:, q * LANES:(q + 1) * LANES]

    a_re = [pw_ref[0, t:t + 1, q * LANES:(q + 1) * LANES] for q in range(nsl)]
    a_im = [pw_ref[0, t:t + 1, (nsl + q) * LANES:(nsl + q + 1) * LANES] for q in range(nsl)]

    def scan_group(g, carry):
        e = list(carry)
        for c in range(cpg):
            sel = pl.ds(g * (n_seq * cpg) + c, n_seq, stride=cpg)
            for q in range(nsl):
                e_re, e_im = e[q], e[nsl + q]
                in_re, in_im = s_scr[q, sel, :], s_scr[nsl + q, sel, :]
                s_scr[q, sel, :] = e_re
                s_scr[nsl + q, sel, :] = e_im
                e[q] = a_re[q] * e_re - a_im[q] * e_im + in_re
                e[nsl + q] = a_re[q] * e_im + a_im[q] * e_re + in_im
        return tuple(e)

    lax.fori_loop(0, ngroups, scan_group, (jnp.zeros((n_seq, LANES), F32),) * (2 * nsl))

    s_prev = jnp.concatenate([s_scr[q].astype(BF16) for q in range(2 * nsl)], axis=1)
    y = yc_scr[...] + lax.dot_general(s_prev, w2_scr[...], (((1,), (1,)), ((), ())), preferred_element_type=F32)
    for i in range(t):
        y_scr[pl.ds(i, rows, stride=t), :] = y[:, i * LANES:(i + 1) * LANES]

    yy = y_scr[...] + d_ref[...] * u_ref[...]
    z_ref[...] = jax.nn.gelu(yy).astype(z_ref.dtype)


def _s5_mixer(u4d, d_row, bcmp, ccmp, pw, weight_jobs=()):
    ngr, bsz, grp, d = u4d.shape
    m = ngr * bsz * grp
    assert grp == S5_GROUP and d % LANES == 0
    nj = d // LANES
    rows = m // SSM_CHUNK
    cg, ns2 = bcmp.shape[1:]
    yw = SSM_CHUNK * LANES
    job_specs, job_args, job_out_specs, job_out_shapes = [], [], [], []
    for w, layer, gain_col in weight_jobs:
        k, n = w.shape[-2:]
        assert k % (nj * 16) == 0
        if layer is None:
            job_specs.append(pl.BlockSpec((k // nj, n), lambda j: (j, 0)))
        else:
            job_specs.append(pl.BlockSpec((None, k // nj, n), lambda j, layer=layer: (layer, j, 0)))
        job_specs.append(pl.BlockSpec((k // nj, 1), lambda j: (j, 0)))
        job_args += [w, gain_col]
        job_out_specs.append(pl.BlockSpec((k // nj, n), lambda j: (j, 0)))
        job_out_shapes.append(jax.ShapeDtypeStruct((k, n), BF16))
    z, *cast = pl.pallas_call(
        functools.partial(_s5_kernel, n_seq=bsz, n_jobs=len(weight_jobs)),
        grid=(nj,),
        in_specs=[
            pl.BlockSpec((m, LANES), lambda j: (0, j)),
            pl.BlockSpec((1, LANES), lambda j: (0, j)),
            pl.BlockSpec((1, cg, ns2), lambda j: (j, 0, 0)),
            pl.BlockSpec((1, cg, ns2), lambda j: (j, 0, 0)),
            pl.BlockSpec((1, 16, ns2), lambda j: (j, 0, 0)),
        ] + job_specs,
        out_specs=[pl.BlockSpec((m, LANES), lambda j: (0, j))] + job_out_specs,
        out_shape=[jax.ShapeDtypeStruct((m, d), BF16)] + job_out_shapes,
        scratch_shapes=[
            pltpu.VMEM((yw, yw + ns2), BF16),
            pltpu.VMEM((yw, ns2), BF16),
            pltpu.VMEM((rows, yw), BF16),
            pltpu.VMEM((ns2 // LANES, rows, LANES), F32),
            pltpu.VMEM((rows, yw), F32),
            pltpu.VMEM((m, LANES), F32),
        ],
        compiler_params=_cparams(("parallel",)),
        name="s5_mixer",
    )(u4d.reshape(m, d), d_row, bcmp, ccmp, pw, *job_args)
    return z.reshape(ngr, bsz, grp, d), cast


def _glu_kernel(z_ref, w_ref, b_ref, res_ref, o_ref, w16_scr):
    tm, n = o_ref.shape

    @pl.when((pl.program_id(0) == 0) & (pl.program_id(1) == 0))
    def _():
        w16_scr[...] = w_ref[...].astype(BF16)

    z = z_ref[...].reshape(tm, n)
    a = jnp.dot(z, w16_scr[...], preferred_element_type=F32) + b_ref[...]
    o_ref[...] = res_ref[...] + z.astype(F32) * jax.nn.sigmoid(a)


def _glu_residual(z4d, w, layer, b_row, res, tm=512):
    ngr, bsz, grp, k = z4d.shape
    m = ngr * bsz * grp
    n = w.shape[2]
    assert k == n and (ngr * grp) % tm == 0 and tm % grp == 0
    tiles = ngr * grp // tm
    return pl.pallas_call(
        _glu_kernel,
        grid=(bsz, tiles),
        in_specs=[
            pl.BlockSpec((tm // grp, None, grp, k), lambda b, i: (i, b, 0, 0)),
            pl.BlockSpec((None, k, n), lambda b, i: (layer, 0, 0), pipeline_mode=pl.Buffered(1)),
            pl.BlockSpec((1, n), lambda b, i: (0, 0)),
            pl.BlockSpec((tm, n), lambda b, i: (b * tiles + i, 0)),
        ],
        out_specs=pl.BlockSpec((tm, n), lambda b, i: (b * tiles + i, 0)),
        out_shape=jax.ShapeDtypeStruct((m, n), F32),
        scratch_shapes=[pltpu.VMEM((k, n), BF16)],
        compiler_params=_cparams(("arbitrary", "arbitrary")),
        name="glu_residual",
    )(z4d, w, b_row, res)


def _proj_res_kernel(x_ref, w_ref, res_ref, o_ref, w16_scr):
    @pl.when(pl.program_id(0) == 0)
    def _():
        w16_scr[...] = w_ref[...].astype(BF16)

    o_ref[...] = res_ref[...] + jnp.dot(x_ref[...], w16_scr[...], preferred_element_type=F32)


def _proj_residual(x, w, layer, res, tm=512):
    m, k = x.shape
    n = w.shape[2]
    assert m % tm == 0
    return pl.pallas_call(
        _proj_res_kernel,
        grid=(m // tm,),
        in_specs=[
            pl.BlockSpec((tm, k), lambda i: (i, 0)),
            pl.BlockSpec((None, k, n), lambda i: (layer, 0, 0), pipeline_mode=pl.Buffered(1)),
            pl.BlockSpec((tm, n), lambda i: (i, 0)),
        ],
        out_specs=pl.BlockSpec((tm, n), lambda i: (i, 0)),
        out_shape=jax.ShapeDtypeStruct((m, n), F32),
        scratch_shapes=[pltpu.VMEM((k, n), BF16)],
        compiler_params=_cparams(("arbitrary",)),
        name="proj_residual",
    )(x, w, res)


def _qkv_kernel(x_ref, wq_ref, wkv_ref, o_ref, xn_scr, *, n_q_tiles, q_scale):
    j = pl.program_id(1)

    @pl.when(j == 0)
    def _():
        x = x_ref[...]
        xn_scr[...] = (x * _rms_scale(x)).astype(BF16)

    @pl.when(j < n_q_tiles)
    def _():
        q = jnp.dot(xn_scr[...], wq_ref[...], preferred_element_type=F32)
        o_ref[...] = (q * q_scale).astype(o_ref.dtype)

    @pl.when(j >= n_q_tiles)
    def _():
        o_ref[...] = jnp.dot(xn_scr[...], wkv_ref[...], preferred_element_type=F32).astype(o_ref.dtype)


def _qkv_proj(h2d, wq16, wkv16, q_scale, tm=1024, tn=1024):
    m, k = h2d.shape
    assert m % tm == 0 and wq16.shape[1] % tn == 0 and wkv16.shape[1] % tn == 0
    nq = wq16.shape[1] // tn
    nkv = wkv16.shape[1] // tn
    return pl.pallas_call(
        functools.partial(_qkv_kernel, n_q_tiles=nq, q_scale=q_scale),
        grid=(m // tm, nq + nkv),
        in_specs=[
            pl.BlockSpec((tm, k), lambda i, j: (i, 0)),
            pl.BlockSpec((k, tn), lambda i, j: (0, jnp.minimum(j, nq - 1))),
            pl.BlockSpec((k, tn), lambda i, j: (0, jnp.maximum(j - nq, 0))),
        ],
        out_specs=pl.BlockSpec((tm, tn), lambda i, j: (i, j)),
        out_shape=jax.ShapeDtypeStruct((m, (nq + nkv) * tn), BF16),
        scratch_shapes=[pltpu.VMEM((tm, k), BF16)],
        compiler_params=_cparams(("parallel", "arbitrary")),
        name="qkv_proj",
    )(h2d, wq16, wkv16)


def _ffn_kernel(h_hbm, g_ref, wg_ref, wu_ref, wd_ref, gfin_ref, o_ref, u_scr, h_buf, h_sem, *, final_norm):
    i = pl.program_id(0)
    f = pl.program_id(1)
    tm = h_buf.shape[0]

    def fetch(tile):
        return pltpu.make_async_copy(h_hbm.at[pl.ds(tile * tm, tm), :], h_buf, h_sem)

    @pl.when((i == 0) & (f == 0))
    def _():
        fetch(0).start()

    @pl.when(f == 0)
    def _():
        fetch(i).wait()
        x = h_buf[...]
        u_scr[...] = (x * _rms_scale(x) * g_ref[...]).astype(BF16)
        o_ref[...] = x

    @pl.when((f == 1) & (i + 1 < pl.num_programs(0)))
    def _():
        fetch(i + 1).start()

    u = u_scr[...]
    gate = jnp.dot(u, wg_ref[...].astype(BF16), preferred_element_type=F32)
    up = jnp.dot(u, wu_ref[...].astype(BF16), preferred_element_type=F32)
    act = (jax.nn.silu(gate) * up).astype(BF16)
    o_ref[...] += jnp.dot(act, wd_ref[...].astype(BF16), preferred_element_type=F32)

    if final_norm:
        @pl.when(f == pl.num_programs(1) - 1)
        def _():
            y = o_ref[...]
            o_ref[...] = y * _rms_scale(y) * gfin_ref[...]


def _ffn_residual(h2d, g_row, w_gate, w_up, w_down, layer, gfin_row, final_norm, tm=1024, tf=512):
    m, d = h2d.shape
    ff = w_gate.shape[2]
    assert m % tm == 0 and ff % tf == 0 and ff // tf >= 2
    return pl.pallas_call(
        functools.partial(_ffn_kernel, final_norm=final_norm),
        grid=(m // tm, ff // tf),
        in_specs=[
            pl.BlockSpec(memory_space=pl.ANY),
            pl.BlockSpec((1, d), lambda i, f: (0, 0)),
            pl.BlockSpec((None, d, tf), lambda i, f: (layer, 0, f)),
            pl.BlockSpec((None, d, tf), lambda i, f: (layer, 0, f)),
            pl.BlockSpec((None, tf, d), lambda i, f: (layer, f, 0)),
            pl.BlockSpec((1, d), lambda i, f: (0, 0)),
        ],
        out_specs=pl.BlockSpec((tm, d), lambda i, f: (i, 0)),
        out_shape=jax.ShapeDtypeStruct((m, d), F32),
        scratch_shapes=[pltpu.VMEM((tm, d), BF16), pltpu.VMEM((tm, d), F32), pltpu.SemaphoreType.DMA],
        compiler_params=_cparams(("arbitrary", "arbitrary"), FFN_VMEM_LIMIT),
        name="ffn_residual",
    )(h2d, g_row, w_gate, w_up, w_down, gfin_row)


def _moba_tables(n_heads, seq_len):
    nb = seq_len // KEY_BLOCK
    nt = MOBA_BIAS_TERMS
    assert nb + 2 * nt <= LANES
    slopes = np.exp2(-8.0 * np.arange(1, n_heads + 1, dtype=np.float64) / n_heads) * LOG2_E
    bias = slopes[:, None] * np.arange(seq_len, dtype=np.float64)[None, :]
    top = 2.0 ** math.ceil(math.log2(LOG2_E * seq_len))

    terms, rest, quantum = [], bias, top / 256.0
    for _ in range(nt - 1):
        term = np.floor(rest / quantum) * quantum
        terms.append(term)
        rest = rest - term
        quantum = quantum / 256.0
    terms.append(rest)

    qx = np.zeros((n_heads, seq_len, LANES), np.float64)
    kx = np.zeros((n_heads, seq_len, LANES), np.float64)
    kx[:, np.arange(seq_len), np.arange(seq_len) // KEY_BLOCK] = 1.0
    for c, term in enumerate(terms):
        qx[:, :, nb + c] = -term
        qx[:, :, nb + nt + c] = 1.0
        kx[:, :, nb + c] = 1.0
        kx[:, :, nb + nt + c] = term
    return jnp.asarray(qx.astype(jnp.bfloat16)), jnp.asarray(kx.astype(jnp.bfloat16))


def _moba_kernel(q_ref, k_ref, v_ref, qx_ref, kx_ref, o_ref, kmean_scr, causal_scr, qa_scr, ka_scr, va_scr,
                 s_scr, p_scr):
    kb = KEY_BLOCK
    hd = HEAD_DIM
    rc_rows = MOBA_ROW_CHUNK
    seq = k_ref.shape[1]
    nb = seq // kb
    nh = qx_ref.shape[0]
    neg_inf = -jnp.inf
    nt = (((1,), (1,)), ((), ()))

    row = lax.broadcasted_iota(jnp.int32, (kb, kb), 0)
    colk = lax.broadcasted_iota(jnp.int32, (kb, kb), 1)
    causal_scr[...] = jnp.where(row >= colk, 0.0, neg_inf)
    for hh in range(nh):
        _moba_prepare(hh, q_ref, k_ref, v_ref, qx_ref, kx_ref, kmean_scr, qa_scr, ka_scr, va_scr)

    for slot, t0 in enumerate(range(0, nb, MOBA_TILE_GROUP)):
        slot = slot % s_scr.shape[1]
        tiles = range(t0, min(t0 + MOBA_TILE_GROUP, nb))
        q0 = t0 * kb
        rows = len(tiles) * kb
        width = (tiles[-1] + 1) * kb
        shared = (t0 + 1) * kb
        for hh in range(nh):
            s_scr[hh, slot, 0:rows, 0:shared] = lax.dot_general(
                qa_scr[hh, q0:q0 + rows, :], ka_scr[hh, 0:shared, :], nt, preferred_element_type=F32)
            for own in tiles[1:]:
                base = (own - t0) * kb
                s_scr[hh, slot, base:rows, own * kb:(own + 1) * kb] = lax.dot_general(
                    qa_scr[hh, q0 + base:q0 + rows, :], ka_scr[hh, own * kb:(own + 1) * kb, :], nt,
                    preferred_element_type=F32)
        for hh in range(nh):
            for own in tiles:
                base = (own - t0) * kb
                for rc in range(kb // rc_rows):
                    r0, r1 = base + rc * rc_rows, base + (rc + 1) * rc_rows
                    parts = [s_scr[hh, slot, r0:r1, n * kb:(n + 1) * kb] for n in range(own)]
                    parts.append(s_scr[hh, slot, r0:r1, own * kb:(own + 1) * kb]
                                 + causal_scr[r0 - base:r1 - base, :])
                    m = jnp.max(functools.reduce(jnp.maximum, parts), axis=-1, keepdims=True)
                    for n in range(own + 1):
                        p_scr[hh, slot, r0:r1, n * kb:(n + 1) * kb] = jnp.exp2(parts[n] - m).astype(BF16)
        for hh in range(nh):
            for own in tiles:
                base = (own - t0) * kb
                res = jnp.dot(p_scr[hh, slot, base:base + kb, 0:(own + 1) * kb], va_scr[hh, 0:(own + 1) * kb, :],
                              preferred_element_type=F32)
                o_ref[0, own * kb:(own + 1) * kb, hh * hd:(hh + 1) * hd] = (
                    res[:, :hd] / res[:, hd:]).astype(o_ref.dtype)


def _moba_prepare(hh, q_ref, k_ref, v_ref, qx_ref, kx_ref, kmean_scr, qa_scr, ka_scr, va_scr):
    kb = KEY_BLOCK
    hd = HEAD_DIM
    seq = k_ref.shape[1]
    nb = seq // kb
    neg_inf = -jnp.inf
    nt = (((1,), (1,)), ((), ()))
    cols = slice(hh * hd, (hh + 1) * hd)

    qa_scr[hh, :, :hd] = q_ref[0, :, cols]
    ka_scr[hh, :, :hd] = k_ref[0, :, cols]
    ka_scr[hh, :, hd:] = kx_ref[hh]
    va_scr[hh, :, :hd] = v_ref[0, :, cols]
    va_scr[hh, :, hd:] = jnp.ones((seq, hd), BF16)
    kmean_scr[hh] = jnp.zeros(kmean_scr.shape[1:], F32)
    for n in range(nb):
        kmean_scr[hh, n:n + 1, :] = jnp.mean(k_ref[0, n * kb:(n + 1) * kb, cols].astype(F32), axis=0, keepdims=True)

    g0 = min((TOP_K_BLOCKS + 1) * kb, seq)
    qa_scr[hh, :g0, hd:] = qx_ref[hh, :g0, :]
    if seq > g0:
        ng = seq - g0
        q_gated = q_ref[0, g0:, cols]
        rest = kmean_scr[hh]
        gate = jnp.zeros((ng, LANES), F32)
        for _ in range(3):
            term = rest.astype(BF16)
            gate = gate + lax.dot_general(q_gated, term, nt, preferred_element_type=F32)
            rest = rest - term.astype(F32)
        col = lax.broadcasted_iota(jnp.int32, (ng, LANES), 1)
        colf = col.astype(F32)
        own_v = lax.div(lax.broadcasted_iota(jnp.int32, (ng, LANES), 0), jnp.int32(kb)) + (TOP_K_BLOCKS + 1)
        past = col < own_v
        gate = jnp.where(past, gate, neg_inf)
        mask = jnp.where(past, MASK_BIG, 0.0)
        for _ in range(TOP_K_BLOCKS):
            mx = jnp.max(gate, axis=-1, keepdims=True)
            idx = jnp.min(jnp.where(gate == mx, colf, float(LANES)), axis=-1, keepdims=True)
            hit = colf == idx
            mask = jnp.where(hit, 0.0, mask)
            gate = jnp.where(hit, neg_inf, gate)
        qa_scr[hh, g0:, hd:] = (qx_ref[hh, g0:, :].astype(F32) + mask).astype(BF16)


def _moba_attention(qkv3d, d_model):
    b, l, _ = qkv3d.shape
    n_heads = d_model // HEAD_DIM
    kb = KEY_BLOCK
    nh = math.gcd(n_heads, MOBA_HEADS)
    wide = nh * HEAD_DIM
    qx, kx = _moba_tables(n_heads, l)
    return pl.pallas_call(
        _moba_kernel,
        grid=(b, n_heads // nh),
        in_specs=[
            pl.BlockSpec((1, l, wide), lambda i, h: (i, 0, h)),
            pl.BlockSpec((1, l, wide), lambda i, h: (i, 0, n_heads // nh + h)),
            pl.BlockSpec((1, l, wide), lambda i, h: (i, 0, 2 * (n_heads // nh) + h)),
            pl.BlockSpec((nh, l, LANES), lambda i, h: (h, 0, 0)),
            pl.BlockSpec((nh, l, LANES), lambda i, h: (h, 0, 0)),
        ],
        out_specs=pl.BlockSpec((1, l, wide), lambda i, h: (i, 0, h)),
        out_shape=jax.ShapeDtypeStruct((b, l, d_model), BF16),
        scratch_shapes=[
            pltpu.VMEM((nh, LANES, HEAD_DIM), F32),
            pltpu.VMEM((kb, kb), F32),
            pltpu.VMEM((nh, l, HEAD_DIM + LANES), BF16),
            pltpu.VMEM((nh, l, HEAD_DIM + LANES), BF16),
            pltpu.VMEM((nh, l, 2 * HEAD_DIM), BF16),
            pltpu.VMEM((nh, MOBA_SLOTS, MOBA_TILE_GROUP * kb, l), F32),
            pltpu.VMEM((nh, MOBA_SLOTS, MOBA_TILE_GROUP * kb, l), BF16),
        ],
        compiler_params=_cparams(("parallel", "parallel")),
        name="moba_attention",
    )(qkv3d, qkv3d, qkv3d, qx, kx)


def kernel(x, g_mix, lambda_re, lambda_im, log_dt, b_re, b_im, c_re, c_im, d_skip, w_glu, b_glu, g_kv, w_kv,
           w_q, w_o, g_ffn, w_gate, w_up, w_down, g_final):
    bsz, l, d = x.shape
    depth = g_mix.shape[0]
    n_a = lambda_re.shape[0]
    m = bsz * l
    assert l % KEY_BLOCK == 0 and d % HEAD_DIM == 0 and l // KEY_BLOCK <= LANES
    row = lambda v: v.reshape(1, -1).astype(F32)
    col = lambda v: v.reshape(-1, 1).astype(F32)

    h = x.astype(F32).reshape(m, d)
    qkv = None
    proj_weights = None
    for i in range(depth):
        if i < n_a:
            tables = _s5_tables(lambda_re[i], lambda_im[i], log_dt[i], b_re[i], b_im[i], c_re[i], c_im[i])
            u = _rmsnorm_interleaved(h, row(g_mix[i]), bsz)
            jobs = ()
            if i == n_a - 1 and depth > n_a:
                jobs = ((w_q, 0, col(g_mix[n_a])), (w_kv, None, col(g_kv)))
            z, cast = _s5_mixer(u, row(d_skip[i]), *tables, weight_jobs=jobs)
            proj_weights = cast or None
            h = _glu_residual(z, w_glu, i, row(b_glu[i]), h)
        else:
            j = i - n_a
            if j > 0 or proj_weights is None:
                proj_weights = [(w_q[j] * col(g_mix[i])).astype(BF16), (w_kv * col(g_kv)).astype(BF16)]
            qkv_i = _qkv_proj(h, *proj_weights, HEAD_DIM ** -0.5 * LOG2_E)
            if qkv is None:
                qkv = qkv_i
            else:
                qkv = jnp.concatenate([qkv_i[:, :d], qkv[:, d:]], axis=1)
            attn = _moba_attention(qkv.reshape(bsz, l, 3 * d), d)
            h = _proj_residual(attn.reshape(m, d), w_o, j, h)
        h = _ffn_residual(h, row(g_ffn[i]), w_gate, w_up, w_down, i, row(g_final), final_norm=(i == depth - 1))
    return h.reshape(bsz, l, d).astype(x.dtype)
```

```python
import functools
import math

import jax
import jax.numpy as jnp
import numpy as np
from jax import lax
from jax.experimental import pallas as pl
from jax.experimental.pallas import tpu as pltpu

F32 = jnp.float32
BF16 = jnp.bfloat16

EPS = 1e-6
LANES = 128
HEAD_DIM = 128
KEY_BLOCK = 256
TOP_K_BLOCKS = 3
MOBA_ROW_CHUNK = 16
MOBA_TILE_GROUP = 2
MOBA_HEADS = 4
MOBA_SLOTS = 1
MOBA_BIAS_TERMS = 4
LOG2_E = math.log2(math.e)
MASK_BIG = -2.0 ** 100
SSM_CHUNK = 8
S5_GROUP = 16
V7X_VMEM_BYTES = 64 * 1024 * 1024
VMEM_LIMIT = V7X_VMEM_BYTES // 8 * 7


FFN_VMEM_LIMIT = V7X_VMEM_BYTES // 16 * 15


def _cparams(sem, vmem_limit=VMEM_LIMIT):
    return pltpu.CompilerParams(dimension_semantics=sem, vmem_limit_bytes=vmem_limit)


def _rms_scale(x):
    return lax.rsqrt(jnp.mean(x * x, axis=-1, keepdims=True) + EPS)


def _rmsnorm_kernel(x_ref, g_ref, o_ref):
    x = x_ref[...]
    o_ref[...] = (x * _rms_scale(x) * g_ref[...]).reshape(o_ref.shape)


def _rmsnorm_interleaved(x2d, g_row, bsz, tm=1024):
    m, d = x2d.shape
    assert m % (bsz * tm) == 0 and tm % S5_GROUP == 0
    tiles = m // bsz // tm
    return pl.pallas_call(
        _rmsnorm_kernel,
        grid=(bsz, tiles),
        in_specs=[pl.BlockSpec((tm, d), lambda b, i: (b * tiles + i, 0)), pl.BlockSpec((1, d), lambda b, i: (0, 0))],
        out_specs=pl.BlockSpec((tm // S5_GROUP, None, S5_GROUP, d), lambda b, i: (i, b, 0, 0)),
        out_shape=jax.ShapeDtypeStruct((m // bsz // S5_GROUP, bsz, S5_GROUP, d), F32),
        compiler_params=_cparams(("parallel", "parallel")),
        name="rmsnorm",
    )(x2d, g_row)


def _s5_tables(lam_re, lam_im, log_dt, b_re, b_im, c_re, c_im):
    g, p, c = b_re.shape
    t = SSM_CHUNK
    gl = LANES // c
    nj = g // gl
    ns = gl * p
    lr, li = lam_re.astype(F32), lam_im.astype(F32)
    dt = jnp.exp(log_dt.astype(F32))[:, None]
    ar, ai = lr * dt, li * dt
    mag = jnp.exp(ar)
    nr, ni = mag * jnp.cos(ai) - 1.0, mag * jnp.sin(ai)
    den = lr * lr + li * li
    fr, fi = (nr * lr + ni * li) / den, (ni * lr - nr * li) / den
    br, bi = b_re.astype(F32), b_im.astype(F32)
    bbr = fr[..., None] * br - fi[..., None] * bi
    bbi = fr[..., None] * bi + fi[..., None] * br
    dd = jnp.arange(t + 1, dtype=F32)[:, None, None]
    pmag = jnp.exp(ar[None] * dd)
    pr, pi = pmag * jnp.cos(ai[None] * dd), pmag * jnp.sin(ai[None] * dd)

    def per_tile(x):
        return jnp.transpose(x.reshape(nj, gl, c, p), (0, 2, 1, 3)).reshape(nj, c, ns)

    bcmp = jnp.concatenate([per_tile(jnp.swapaxes(bbr, 1, 2)), per_tile(jnp.swapaxes(bbi, 1, 2))], axis=-1)
    ccmp = jnp.concatenate([per_tile(c_re.astype(F32)), per_tile(c_im.astype(F32))], axis=-1)
    pw = jnp.concatenate([pr.reshape(t + 1, nj, ns), pi.reshape(t + 1, nj, ns)], axis=-1)
    pw = jnp.pad(jnp.swapaxes(pw, 0, 1), ((0, 0), (0, 16 - (t + 1)), (0, 0)))
    return bcmp, ccmp, pw


def _s5_build_tables(bcmp_ref, ccmp_ref, pw_ref, w1_scr, w2_scr):
    t = SSM_CHUNK
    cg = bcmp_ref.shape[1]
    ns = bcmp_ref.shape[2] // 2
    sg = ns * cg // LANES
    yw = t * LANES
    same_group = (lax.broadcasted_iota(jnp.int32, (LANES, ns), 0) // cg
                  == lax.broadcasted_iota(jnp.int32, (LANES, ns), 1) // sg)

    def block_diag(x):
        return jnp.where(same_group, jnp.tile(x, (LANES // cg, 1)), 0.0)

    bre, bim = bcmp_ref[0, :, :ns], bcmp_ref[0, :, ns:]
    cre, cim = ccmp_ref[0, :, :ns], ccmp_ref[0, :, ns:]
    bcat16 = jnp.concatenate([block_diag(bre), block_diag(bim)], axis=1).astype(BF16)
    nt = (((1,), (1,)), ((), ()))
    for d in range(t + 1):
        pr, pi = pw_ref[0, d:d + 1, :ns], pw_ref[0, d:d + 1, ns:]
        mo = jnp.concatenate([block_diag(cre * pr - cim * pi), block_diag(-(cre * pi + cim * pr))],
                             axis=1).astype(BF16)
        if d >= 1:
            w2_scr[(d - 1) * LANES:d * LANES, :] = mo
        if d < t:
            kd = lax.dot_general(bcat16, mo, nt, preferred_element_type=F32).astype(BF16)
            for ip in range(t - d):
                w1_scr[ip * LANES:(ip + 1) * LANES, (ip + d) * LANES:(ip + d + 1) * LANES] = kd
            ip = t - 1 - d
            w1_scr[ip * LANES:(ip + 1) * LANES, yw:yw + ns] = block_diag(bre * pr - bim * pi).astype(BF16)
            w1_scr[ip * LANES:(ip + 1) * LANES, yw + ns:] = block_diag(bre * pi + bim * pr).astype(BF16)
    for ip in range(1, t):
        w1_scr[ip * LANES:(ip + 1) * LANES, 0:ip * LANES] = jnp.zeros((LANES, ip * LANES), BF16)


def _s5_kernel(*refs, n_seq, n_jobs):
    u_ref, d_ref, bcmp_ref, ccmp_ref, pw_ref = refs[:5]
    job_in = refs[5:5 + 2 * n_jobs]
    z_ref = refs[5 + 2 * n_jobs]
    job_out = refs[6 + 2 * n_jobs:6 + 3 * n_jobs]
    w1_scr, w2_scr, lhs_scr, s_scr, yc_scr, y_scr = refs[6 + 3 * n_jobs:]
    for k in range(n_jobs):
        job_out[k][...] = (job_in[2 * k][...] * job_in[2 * k + 1][...]).astype(BF16)
    t = SSM_CHUNK
    cpg = S5_GROUP // t
    rows = lhs_scr.shape[0]
    ngroups = rows // (n_seq * cpg)
    nsl = s_scr.shape[0] // 2
    yw = t * LANES

    _s5_build_tables(bcmp_ref, ccmp_ref, pw_ref, w1_scr, w2_scr)

    for i in range(t):
        lhs_scr[:, i * LANES:(i + 1) * LANES] = u_ref[pl.ds(i, rows, stride=t), :].astype(BF16)
    strip = 2 * LANES
    for c0 in range(0, yw, strip):
        c1 = c0 + strip
        yc_scr[:, c0:c1] = jnp.dot(lhs_scr[:, :c1], w1_scr[:c1, c0:c1], preferred_element_type=F32)
    s_in_all = jnp.dot(lhs_scr[...], w1_scr[:, yw:], preferred_element_type=F32)
    for q in range(2 * nsl):
        s_scr[q] = s_in_all[:, q * LANES:(q + 1) * LANES]

    a_re = [pw_ref[0, t:t + 1, q * LANES:(q + 1) * LANES] for q in range(nsl)]
    a_im = [pw_ref[0, t:t + 1, (nsl + q) * LANES:(nsl + q + 1) * LANES] for q in range(nsl)]

    def scan_group(g, carry):
        e = list(carry)
        for c in range(cpg):
            sel = pl.ds(g * (n_seq * cpg) + c, n_seq, stride=cpg)
            for q in range(nsl):
                e_re, e_im = e[q], e[nsl + q]
                in_re, in_im = s_scr[q, sel, :], s_scr[nsl + q, sel, :]
                s_scr[q, sel, :] = e_re
                s_scr[nsl + q, sel, :] = e_im
                e[q] = a_re[q] * e_re - a_im[q] * e_im + in_re
                e[nsl + q] = a_re[q] * e_im + a_im[q] * e_re + in_im
        return tuple(e)

    lax.fori_loop(0, ngroups, scan_group, (jnp.zeros((n_seq, LANES), F32),) * (2 * nsl))

    s_prev = jnp.concatenate([s_scr[q].astype(BF16) for q in range(2 * nsl)], axis=1)
    y = yc_scr[...] + lax.dot_general(s_prev, w2_scr[...], (((1,), (1,)), ((), ())), preferred_element_type=F32)
    for i in range(t):
        y_scr[pl.ds(i, rows, stride=t), :] = y[:, i * LANES:(i + 1) * LANES]

    yy = y_scr[...] + d_ref[...] * u_ref[...]
    z_ref[...] = jax.nn.gelu(yy).astype(z_ref.dtype)


def _s5_mixer(u4d, d_row, bcmp, ccmp, pw, weight_jobs=()):
    ngr, bsz, grp, d = u4d.shape
    m = ngr * bsz * grp
    assert grp == S5_GROUP and d % LANES == 0
    nj = d // LANES
    rows = m // SSM_CHUNK
    cg, ns2 = bcmp.shape[1:]
    yw = SSM_CHUNK * LANES
    job_specs, job_args, job_out_specs, job_out_shapes = [], [], [], []
    for w, layer, gain_col in weight_jobs:
        k, n = w.shape[-2:]
        assert k % (nj * 16) == 0
        if layer is None:
            job_specs.append(pl.BlockSpec((k // nj, n), lambda j: (j, 0)))
        else:
            job_specs.append(pl.BlockSpec((None, k // nj, n), lambda j, layer=layer: (layer, j, 0)))
        job_specs.append(pl.BlockSpec((k // nj, 1), lambda j: (j, 0)))
        job_args += [w, gain_col]
        job_out_specs.append(pl.BlockSpec((k // nj, n), lambda j: (j, 0)))
        job_out_shapes.append(jax.ShapeDtypeStruct((k, n), BF16))
    z, *cast = pl.pallas_call(
        functools.partial(_s5_kernel, n_seq=bsz, n_jobs=len(weight_jobs)),
        grid=(nj,),
        in_specs=[
            pl.BlockSpec((m, LANES), lambda j: (0, j)),
            pl.BlockSpec((1, LANES), lambda j: (0, j)),
            pl.BlockSpec((1, cg, ns2), lambda j: (j, 0, 0)),
            pl.BlockSpec((1, cg, ns2), lambda j: (j, 0, 0)),
            pl.BlockSpec((1, 16, ns2), lambda j: (j, 0, 0)),
        ] + job_specs,
        out_specs=[pl.BlockSpec((m, LANES), lambda j: (0, j))] + job_out_specs,
        out_shape=[jax.ShapeDtypeStruct((m, d), BF16)] + job_out_shapes,
        scratch_shapes=[
            pltpu.VMEM((yw, yw + ns2), BF16),
            pltpu.VMEM((yw, ns2), BF16),
            pltpu.VMEM((rows, yw), BF16),
            pltpu.VMEM((ns2 // LANES, rows, LANES), F32),
            pltpu.VMEM((rows, yw), F32),
            pltpu.VMEM((m, LANES), F32),
        ],
        compiler_params=_cparams(("parallel",)),
        name="s5_mixer",
    )(u4d.reshape(m, d), d_row, bcmp, ccmp, pw, *job_args)
    return z.reshape(ngr, bsz, grp, d), cast


def _glu_kernel(z_ref, w_ref, b_ref, res_ref, o_ref, w16_scr):
    tm, n = o_ref.shape

    @pl.when((pl.program_id(0) == 0) & (pl.program_id(1) == 0))
    def _():
        w16_scr[...] = w_ref[...].astype(BF16)

    z = z_ref[...].reshape(tm, n)
    a = jnp.dot(z, w16_scr[...], preferred_element_type=F32) + b_ref[...]
    o_ref[...] = res_ref[...] + z.astype(F32) * jax.nn.sigmoid(a)


def _glu_residual(z4d, w, layer, b_row, res, tm=512):
    ngr, bsz, grp, k = z4d.shape
    m = ngr * bsz * grp
    n = w.shape[2]
    assert k == n and (ngr * grp) % tm == 0 and tm % grp == 0
    tiles = ngr * grp // tm
    return pl.pallas_call(
        _glu_kernel,
        grid=(bsz, tiles),
        in_specs=[
            pl.BlockSpec((tm // grp, None, grp, k), lambda b, i: (i, b, 0, 0)),
            pl.BlockSpec((None, k, n), lambda b, i: (layer, 0, 0), pipeline_mode=pl.Buffered(1)),
            pl.BlockSpec((1, n), lambda b, i: (0, 0)),
            pl.BlockSpec((tm, n), lambda b, i: (b * tiles + i, 0)),
        ],
        out_specs=pl.BlockSpec((tm, n), lambda b, i: (b * tiles + i, 0)),
        out_shape=jax.ShapeDtypeStruct((m, n), F32),
        scratch_shapes=[pltpu.VMEM((k, n), BF16)],
        compiler_params=_cparams(("arbitrary", "arbitrary")),
        name="glu_residual",
    )(z4d, w, b_row, res)


def _proj_res_kernel(x_ref, w_ref, res_ref, o_ref, w16_scr):
    @pl.when(pl.program_id(0) == 0)
    def _():
        w16_scr[...] = w_ref[...].astype(BF16)

    o_ref[...] = res_ref[...] + jnp.dot(x_ref[...], w16_scr[...], preferred_element_type=F32)


def _proj_residual(x, w, layer, res, tm=512):
    m, k = x.shape
    n = w.shape[2]
    assert m % tm == 0
    return pl.pallas_call(
        _proj_res_kernel,
        grid=(m // tm,),
        in_specs=[
            pl.BlockSpec((tm, k), lambda i: (i, 0)),
            pl.BlockSpec((None, k, n), lambda i: (layer, 0, 0), pipeline_mode=pl.Buffered(1)),
            pl.BlockSpec((tm, n), lambda i: (i, 0)),
        ],
        out_specs=pl.BlockSpec((tm, n), lambda i: (i, 0)),
        out_shape=jax.ShapeDtypeStruct((m, n), F32),
        scratch_shapes=[pltpu.VMEM((k, n), BF16)],
        compiler_params=_cparams(("arbitrary",)),
        name="proj_residual",
    )(x, w, res)


def _qkv_kernel(x_hbm, wq_ref, wkv_ref, o_ref, xn_scr, x_buf, x_sem, *, n_q_tiles, q_scale):
    i = pl.program_id(0)
    j = pl.program_id(1)
    tm = x_buf.shape[0]

    def fetch(tile):
        return pltpu.make_async_copy(x_hbm.at[pl.ds(tile * tm, tm), :], x_buf, x_sem)

    @pl.when((i == 0) & (j == 0))
    def _():
        fetch(0).start()

    @pl.when(j == 0)
    def _():
        fetch(i).wait()
        x = x_buf[...]
        xn_scr[...] = (x * _rms_scale(x)).astype(BF16)

    @pl.when((j == 1) & (i + 1 < pl.num_programs(0)))
    def _():
        fetch(i + 1).start()

    @pl.when(j < n_q_tiles)
    def _():
        q = jnp.dot(xn_scr[...], wq_ref[...], preferred_element_type=F32)
        o_ref[...] = (q * q_scale).astype(o_ref.dtype)

    @pl.when(j >= n_q_tiles)
    def _():
        o_ref[...] = jnp.dot(xn_scr[...], wkv_ref[...], preferred_element_type=F32).astype(o_ref.dtype)


def _qkv_proj(h2d, wq16, wkv16, q_scale, tm=2048, tn=1024):
    m, k = h2d.shape
    assert m % tm == 0 and wq16.shape[1] % tn == 0 and wkv16.shape[1] % tn == 0
    nq = wq16.shape[1] // tn
    nkv = wkv16.shape[1] // tn
    return pl.pallas_call(
        functools.partial(_qkv_kernel, n_q_tiles=nq, q_scale=q_scale),
        grid=(m // tm, nq + nkv),
        in_specs=[
            pl.BlockSpec(memory_space=pl.ANY),
            pl.BlockSpec((k, tn), lambda i, j: (0, jnp.minimum(j, nq - 1))),
            pl.BlockSpec((k, tn), lambda i, j: (0, jnp.maximum(j - nq, 0))),
        ],
        out_specs=pl.BlockSpec((tm, tn), lambda i, j: (i, j)),
        out_shape=jax.ShapeDtypeStruct((m, (nq + nkv) * tn), BF16),
        scratch_shapes=[pltpu.VMEM((tm, k), BF16), pltpu.VMEM((tm, k), F32), pltpu.SemaphoreType.DMA],
        compiler_params=_cparams(("arbitrary", "arbitrary"), FFN_VMEM_LIMIT),
        name="qkv_proj",
    )(h2d, wq16, wkv16)


def _ffn_kernel(h_hbm, g_ref, wg_ref, wu_ref, wd_ref, gfin_ref, o_ref, u_scr, h_buf, h_sem, *, final_norm):
    i = pl.program_id(0)
    f = pl.program_id(1)
    tm = h_buf.shape[0]

    def fetch(tile):
        return pltpu.make_async_copy(h_hbm.at[pl.ds(tile * tm, tm), :], h_buf, h_sem)

    @pl.when((i == 0) & (f == 0))
    def _():
        fetch(0).start()

    @pl.when(f == 0)
    def _():
        fetch(i).wait()
        x = h_buf[...]
        u_scr[...] = (x * _rms_scale(x) * g_ref[...]).astype(BF16)
        o_ref[...] = x

    @pl.when((f == 1) & (i + 1 < pl.num_programs(0)))
    def _():
        fetch(i + 1).start()

    u = u_scr[...]
    gate = jnp.dot(u, wg_ref[...].astype(BF16), preferred_element_type=F32)
    up = jnp.dot(u, wu_ref[...].astype(BF16), preferred_element_type=F32)
    act = (jax.nn.silu(gate) * up).astype(BF16)
    o_ref[...] += jnp.dot(act, wd_ref[...].astype(BF16), preferred_element_type=F32)

    if final_norm:
        @pl.when(f == pl.num_programs(1) - 1)
        def _():
            y = o_ref[...]
            o_ref[...] = y * _rms_scale(y) * gfin_ref[...]


def _ffn_residual(h2d, g_row, w_gate, w_up, w_down, layer, gfin_row, final_norm, tm=1024, tf=512):
    m, d = h2d.shape
    ff = w_gate.shape[2]
    assert m % tm == 0 and ff % tf == 0 and ff // tf >= 2
    return pl.pallas_call(
        functools.partial(_ffn_kernel, final_norm=final_norm),
        grid=(m // tm, ff // tf),
        in_specs=[
            pl.BlockSpec(memory_space=pl.ANY),
            pl.BlockSpec((1, d), lambda i, f: (0, 0)),
            pl.BlockSpec((None, d, tf), lambda i, f: (layer, 0, f)),
            pl.BlockSpec((None, d, tf), lambda i, f: (layer, 0, f)),
            pl.BlockSpec((None, tf, d), lambda i, f: (layer, f, 0)),
            pl.BlockSpec((1, d), lambda i, f: (0, 0)),
        ],
        out_specs=pl.BlockSpec((tm, d), lambda i, f: (i, 0)),
        out_shape=jax.ShapeDtypeStruct((m, d), F32),
        scratch_shapes=[pltpu.VMEM((tm, d), BF16), pltpu.VMEM((tm, d), F32), pltpu.SemaphoreType.DMA],
        compiler_params=_cparams(("arbitrary", "arbitrary"), FFN_VMEM_LIMIT),
        name="ffn_residual",
    )(h2d, g_row, w_gate, w_up, w_down, gfin_row)


def _moba_tables(n_heads, seq_len):
    nb = seq_len // KEY_BLOCK
    nt = MOBA_BIAS_TERMS
    assert nb + 2 * nt <= LANES
    slopes = np.exp2(-8.0 * np.arange(1, n_heads + 1, dtype=np.float64) / n_heads) * LOG2_E
    bias = slopes[:, None] * np.arange(seq_len, dtype=np.float64)[None, :]
    top = 2.0 ** math.ceil(math.log2(LOG2_E * seq_len))

    terms, rest, quantum = [], bias, top / 256.0
    for _ in range(nt - 1):
        term = np.floor(rest / quantum) * quantum
        terms.append(term)
        rest = rest - term
        quantum = quantum / 256.0
    terms.append(rest)

    qx = np.zeros((n_heads, seq_len, LANES), np.float64)
    kx = np.zeros((n_heads, seq_len, LANES), np.float64)
    kx[:, np.arange(seq_len), np.arange(seq_len) // KEY_BLOCK] = 1.0
    for c, term in enumerate(terms):
        qx[:, :, nb + c] = -term
        qx[:, :, nb + nt + c] = 1.0
        kx[:, :, nb + c] = 1.0
        kx[:, :, nb + nt + c] = term
    return jnp.asarray(qx.astype(jnp.bfloat16)), jnp.asarray(kx.astype(jnp.bfloat16))


def _moba_kernel(q_ref, k_ref, v_ref, qx_ref, kx_ref, o_ref, kmean_scr, causal_scr, qa_scr, ka_scr, va_scr,
                 s_scr, p_scr):
    kb = KEY_BLOCK
    hd = HEAD_DIM
    rc_rows = MOBA_ROW_CHUNK
    seq = k_ref.shape[1]
    nb = seq // kb
    nh = qx_ref.shape[0]
    neg_inf = -jnp.inf
    nt = (((1,), (1,)), ((), ()))

    row = lax.broadcasted_iota(jnp.int32, (kb, kb), 0)
    colk = lax.broadcasted_iota(jnp.int32, (kb, kb), 1)
    causal_scr[...] = jnp.where(row >= colk, 0.0, neg_inf)
    for hh in range(nh):
        _moba_prepare(hh, q_ref, k_ref, v_ref, qx_ref, kx_ref, kmean_scr, qa_scr, ka_scr, va_scr)

    for slot, t0 in enumerate(range(0, nb, MOBA_TILE_GROUP)):
        slot = slot % s_scr.shape[1]
        tiles = range(t0, min(t0 + MOBA_TILE_GROUP, nb))
        q0 = t0 * kb
        rows = len(tiles) * kb
        width = (tiles[-1] + 1) * kb
        shared = (t0 + 1) * kb
        for hh in range(nh):
            s_scr[hh, slot, 0:rows, 0:shared] = lax.dot_general(
                qa_scr[hh, q0:q0 + rows, :], ka_scr[hh, 0:shared, :], nt, preferred_element_type=F32)
            for own in tiles[1:]:
                base = (own - t0) * kb
                s_scr[hh, slot, base:rows, own * kb:(own + 1) * kb] = lax.dot_general(
                    qa_scr[hh, q0 + base:q0 + rows, :], ka_scr[hh, own * kb:(own + 1) * kb, :], nt,
                    preferred_element_type=F32)
        for hh in range(nh):
            for own in tiles:
                base = (own - t0) * kb
                for rc in range(kb // rc_rows):
                    r0, r1 = base + rc * rc_rows, base + (rc + 1) * rc_rows
                    parts = [s_scr[hh, slot, r0:r1, n * kb:(n + 1) * kb] for n in range(own)]
                    parts.append(s_scr[hh, slot, r0:r1, own * kb:(own + 1) * kb]
                                 + causal_scr[r0 - base:r1 - base, :])
                    m = jnp.max(functools.reduce(jnp.maximum, parts), axis=-1, keepdims=True)
                    for n in range(own + 1):
                        p_scr[hh, slot, r0:r1, n * kb:(n + 1) * kb] = jnp.exp2(parts[n] - m).astype(BF16)
        for hh in range(nh):
            for own in tiles:
                base = (own - t0) * kb
                res = jnp.dot(p_scr[hh, slot, base:base + kb, 0:(own + 1) * kb], va_scr[hh, 0:(own + 1) * kb, :],
                              preferred_element_type=F32)
                o_ref[0, own * kb:(own + 1) * kb, hh * hd:(hh + 1) * hd] = (
                    res[:, :hd] / res[:, hd:]).astype(o_ref.dtype)


def _moba_prepare(hh, q_ref, k_ref, v_ref, qx_ref, kx_ref, kmean_scr, qa_scr, ka_scr, va_scr):
    kb = KEY_BLOCK
    hd = HEAD_DIM
    seq = k_ref.shape[1]
    nb = seq // kb
    neg_inf = -jnp.inf
    nt = (((1,), (1,)), ((), ()))
    cols = slice(hh * hd, (hh + 1) * hd)

    qa_scr[hh, :, :hd] = q_ref[0, :, cols]
    ka_scr[hh, :, :hd] = k_ref[0, :, cols]
    ka_scr[hh, :, hd:] = kx_ref[hh]
    va_scr[hh, :, :hd] = v_ref[0, :, cols]
    va_scr[hh, :, hd:] = jnp.ones((seq, hd), BF16)
    kmean_scr[hh] = jnp.zeros(kmean_scr.shape[1:], F32)
    for n in range(nb):
        kmean_scr[hh, n:n + 1, :] = jnp.mean(k_ref[0, n * kb:(n + 1) * kb, cols].astype(F32), axis=0, keepdims=True)

    g0 = min((TOP_K_BLOCKS + 1) * kb, seq)
    qa_scr[hh, :g0, hd:] = qx_ref[hh, :g0, :]
    if seq > g0:
        ng = seq - g0
        q_gated = q_ref[0, g0:, cols]
        rest = kmean_scr[hh]
        gate = jnp.zeros((ng, LANES), F32)
        for _ in range(3):
            term = rest.astype(BF16)
            gate = gate + lax.dot_general(q_gated, term, nt, preferred_element_type=F32)
            rest = rest - term.astype(F32)
        col = lax.broadcasted_iota(jnp.int32, (ng, LANES), 1)
        colf = col.astype(F32)
        own_v = lax.div(lax.broadcasted_iota(jnp.int32, (ng, LANES), 0), jnp.int32(kb)) + (TOP_K_BLOCKS + 1)
        past = col < own_v
        gate = jnp.where(past, gate, neg_inf)
        mask = jnp.where(past, MASK_BIG, 0.0)
        for _ in range(TOP_K_BLOCKS):
            mx = jnp.max(gate, axis=-1, keepdims=True)
            idx = jnp.min(jnp.where(gate == mx, colf, float(LANES)), axis=-1, keepdims=True)
            hit = colf == idx
            mask = jnp.where(hit, 0.0, mask)
            gate = jnp.where(hit, neg_inf, gate)
        qa_scr[hh, g0:, hd:] = (qx_ref[hh, g0:, :].astype(F32) + mask).astype(BF16)


def _moba_attention(qkv3d, d_model):
    b, l, _ = qkv3d.shape
    n_heads = d_model // HEAD_DIM
    kb = KEY_BLOCK
    nh = math.gcd(n_heads, MOBA_HEADS)
    wide = nh * HEAD_DIM
    qx, kx = _moba_tables(n_heads, l)
    return pl.pallas_call(
        _moba_kernel,
        grid=(b, n_heads // nh),
        in_specs=[
            pl.BlockSpec((1, l, wide), lambda i, h: (i, 0, h)),
            pl.BlockSpec((1, l, wide), lambda i, h: (i, 0, n_heads // nh + h)),
            pl.BlockSpec((1, l, wide), lambda i, h: (i, 0, 2 * (n_heads // nh) + h)),
            pl.BlockSpec((nh, l, LANES), lambda i, h: (h, 0, 0)),
            pl.BlockSpec((nh, l, LANES), lambda i, h: (h, 0, 0)),
        ],
        out_specs=pl.BlockSpec((1, l, wide), lambda i, h: (i, 0, h)),
        out_shape=jax.ShapeDtypeStruct((b, l, d_model), BF16),
        scratch_shapes=[
            pltpu.VMEM((nh, LANES, HEAD_DIM), F32),
            pltpu.VMEM((kb, kb), F32),
            pltpu.VMEM((nh, l, HEAD_DIM + LANES), BF16),
            pltpu.VMEM((nh, l, HEAD_DIM + LANES), BF16),
            pltpu.VMEM((nh, l, 2 * HEAD_DIM), BF16),
            pltpu.VMEM((nh, MOBA_SLOTS, MOBA_TILE_GROUP * kb, l), F32),
            pltpu.VMEM((nh, MOBA_SLOTS, MOBA_TILE_GROUP * kb, l), BF16),
        ],
        compiler_params=_cparams(("parallel", "parallel")),
        name="moba_attention",
    )(qkv3d, qkv3d, qkv3d, qx, kx)


def kernel(x, g_mix, lambda_re, lambda_im, log_dt, b_re, b_im, c_re, c_im, d_skip, w_glu, b_glu, g_kv, w_kv,
           w_q, w_o, g_ffn, w_gate, w_up, w_down, g_final):
    bsz, l, d = x.shape
    depth = g_mix.shape[0]
    n_a = lambda_re.shape[0]
    m = bsz * l
    assert l % KEY_BLOCK == 0 and d % HEAD_DIM == 0 and l // KEY_BLOCK <= LANES
    row = lambda v: v.reshape(1, -1).astype(F32)
    col = lambda v: v.reshape(-1, 1).astype(F32)

    h = x.astype(F32).reshape(m, d)
    qkv = None
    proj_weights = None
    for i in range(depth):
        if i < n_a:
            tables = _s5_tables(lambda_re[i], lambda_im[i], log_dt[i], b_re[i], b_im[i], c_re[i], c_im[i])
            u = _rmsnorm_interleaved(h, row(g_mix[i]), bsz)
            jobs = ()
            if i == n_a - 1 and depth > n_a:
                jobs = ((w_q, 0, col(g_mix[n_a])), (w_kv, None, col(g_kv)))
            z, cast = _s5_mixer(u, row(d_skip[i]), *tables, weight_jobs=jobs)
            proj_weights = cast or None
            h = _glu_residual(z, w_glu, i, row(b_glu[i]), h)
        else:
            j = i - n_a
            if j > 0 or proj_weights is None:
                proj_weights = [(w_q[j] * col(g_mix[i])).astype(BF16), (w_kv * col(g_kv)).astype(BF16)]
            qkv_i = _qkv_proj(h, *proj_weights, HEAD_DIM ** -0.5 * LOG2_E)
            if qkv is None:
                qkv = qkv_i
            else:
                qkv = jnp.concatenate([qkv_i[:, :d], qkv[:, d:]], axis=1)
            attn = _moba_attention(qkv.reshape(bsz, l, 3 * d), d)
            h = _proj_residual(attn.reshape(m, d), w_o, j, h)
        h = _ffn_residual(h, row(g_ffn[i]), w_gate, w_up, w_down, i, row(g_final), final_norm=(i == depth - 1))
    return h.reshape(bsz, l, d).astype(x.dtype)
```
